```python
import jax, jax.numpy as jnp
from jax import lax
import numpy as np

D_MODEL = 1024
BATCH = 2
SEQ = 8192
DEPTH = 4

N_MIXERS = 2
N_A_LAYERS = (DEPTH + N_MIXERS - 1) // N_MIXERS
N_B_LAYERS = DEPTH // N_MIXERS

HG_HEADS = 8
HG_DK = D_MODEL // HG_HEADS
HG_DV = D_MODEL // HG_HEADS
HG_CHUNK = 128
LRU_WIDTH = D_MODEL
LRU_BLOCKS = 4
LRU_BW = LRU_WIDTH // LRU_BLOCKS
CONV_W = 4
CONV_LEFT = 2
LRU_C = 8.0
PEER_HEADS = 8
PEER_NKEYS = 128
PEER_NEXPERTS = PEER_NKEYS * PEER_NKEYS
PEER_DQ = 256
PEER_TOPK = 16
PEER_BLOCK = 128
EPS = 1e-6

kernel_name = "hybrid_hgrn2_rglru_peer_encoder"


def rms_norm(x, gain):
    xf = x.astype(jnp.float32)
    inv = lax.rsqrt(jnp.mean(xf * xf, axis=-1, keepdims=True) + EPS)
    return (xf * inv).astype(x.dtype) * gain


def hgrn2_chunk_scan(q, k, v, log_f):
    B, S, H, DK = q.shape
    DV = v.shape[-1]
    C = HG_CHUNK
    nc = S // C

    def to_chunks(t):
        return t.reshape(B, nc, C, H, t.shape[-1]).transpose(1, 0, 3, 2, 4)

    qc, kc, vc, gc = to_chunks(q), to_chunks(k), to_chunks(v), to_chunks(log_f)
    lower = jnp.tril(jnp.ones((C, C), dtype=bool))[:, :, None]

    def step(state, inp):
        qi, ki, vi, gi = inp
        b = jnp.cumsum(gi.astype(jnp.float32), axis=-2)
        b_last = b[..., -1:, :]
        rel = b[..., :, None, :] - b[..., None, :, :]
        decay = jnp.exp(jnp.where(lower, rel, -jnp.inf))
        attn = jnp.sum((qi[..., :, None, :] * ki[..., None, :, :]) * decay, axis=-1)
        o_intra = jnp.einsum('bhts,bhsv->bhtv', attn, vi)
        o_inter = jnp.einsum('bhtd,bhdv->bhtv', qi * jnp.exp(b), state)
        k_dec = ki * jnp.exp(b_last - b)
        new_state = state * jnp.exp(b_last[..., 0, :])[..., None] + jnp.einsum('bhsd,bhsv->bhdv', k_dec, vi)
        return new_state, o_intra + o_inter

    state0 = jnp.zeros((B, H, DK, DV), jnp.float32)
    _, o = lax.scan(step, state0, (qc, kc, vc, gc))
    return o.transpose(1, 0, 3, 2, 4).reshape(B, S, H, DV)


def hgrn2_mixer(h, w_in, lb, norm_g, w_out):
    B, S, _ = h.shape
    proj = h @ w_in
    q, f_fw, f_bw, i_in, g = jnp.split(proj, 5, axis=-1)

    def heads(t):
        return t.reshape(B, S, HG_HEADS, -1)

    q, v, g = heads(q), heads(i_in), heads(g)

    def gates(f_logit):
        f = lb + (1.0 - lb) * jax.nn.sigmoid(f_logit.astype(jnp.float32))
        return heads(jnp.log(f)), heads(1.0 - f)

    logf_fw, k_fw = gates(f_fw)
    logf_bw, k_bw = gates(f_bw)
    rev = lambda t: t[:, ::-1]
    o_fw = hgrn2_chunk_scan(q, k_fw, v, logf_fw)
    o_bw = rev(hgrn2_chunk_scan(rev(q), rev(k_bw), rev(v), rev(logf_bw)))
    o = rms_norm(o_fw + o_bw, norm_g.reshape(HG_HEADS, HG_DV)) * jax.nn.silu(g)
    return o.reshape(B, S, D_MODEL) @ w_out


def block_diag_linear(x, w, b):
    B, S, _ = x.shape
    xr = x.reshape(B, S, LRU_BLOCKS, LRU_BW)
    return jnp.einsum('bsnk,nkj->bsnj', xr, w).reshape(B, S, LRU_WIDTH) + b


def lin_combine(left, right):
    a_l, b_l = left
    a_r, b_r = right
    return a_l * a_r, a_r * b_l + b_r


def rglru_mixer(h, w_in, conv_w, conv_b, w_a, b_a, w_x, b_x, lam, w_out):
    B, S, _ = h.shape
    proj = h @ w_in
    xb, yb = jnp.split(proj, 2, axis=-1)
    y_gate = jax.nn.gelu(yb)
    xp = jnp.pad(xb, ((0, 0), (CONV_LEFT, CONV_W - 1 - CONV_LEFT), (0, 0)))
    xc = sum(xp[:, j:j + S] * conv_w[j] for j in range(CONV_W)) + conv_b

    def direction(d, reverse):
        r = jax.nn.sigmoid(block_diag_linear(xc, w_a[d], b_a[d]).astype(jnp.float32))
        ig = jax.nn.sigmoid(block_diag_linear(xc, w_x[d], b_x[d]).astype(jnp.float32))
        log_a = -LRU_C * r * jax.nn.softplus(-lam[d].astype(jnp.float32))
        a = jnp.exp(log_a)
        mult = jnp.sqrt(-jnp.expm1(2.0 * log_a))
        _, hs = lax.associative_scan(lin_combine, (a, mult * ig * xc), axis=1, reverse=reverse)
        return hs

    hsum = direction(0, False) + direction(1, True)
    return (hsum * y_gate) @ w_out


def peer_ffn(h, w_q, sub_keys, u_tab, v_tab):
    B, S, D = h.shape
    q = (h @ w_q).reshape(B, S, PEER_HEADS, 2, PEER_DQ // 2)
    scores = jnp.einsum('bshpd,hpkd->bshpk', q, sub_keys).astype(jnp.float32)
    s1, i1 = lax.top_k(scores[..., 0, :], PEER_TOPK)
    s2, i2 = lax.top_k(scores[..., 1, :], PEER_TOPK)
    cand_s = (s1[..., :, None] + s2[..., None, :]).reshape(B, S, PEER_HEADS, PEER_TOPK * PEER_TOPK)
    cand_id = (i1[..., :, None] * PEER_NKEYS + i2[..., None, :]).reshape(B, S, PEER_HEADS, PEER_TOPK * PEER_TOPK)
    top_s, top_pos = lax.top_k(cand_s, PEER_TOPK)
    ids = jnp.take_along_axis(cand_id, top_pos, axis=-1)
    gate = jax.nn.softmax(top_s, axis=-1)

    nb = (B * S) // PEER_BLOCK
    hb = h.reshape(nb, PEER_BLOCK, D)
    idb = ids.reshape(nb, PEER_BLOCK, PEER_HEADS, PEER_TOPK)
    gb = gate.reshape(nb, PEER_BLOCK, PEER_HEADS, PEER_TOPK)

    def block(args):
        hx, idx, g = args
        u = u_tab[idx]
        act = jax.nn.gelu(jnp.einsum('thkd,td->thk', u, hx).astype(jnp.float32))
        return jnp.einsum('thk,thkd->td', g * act, v_tab[idx])

    out = lax.map(block, (hb, idb, gb))
    return out.reshape(B, S, D)


def setup_inputs(seed: int = 0) -> dict:
    key = jax.random.key(seed)
    ks = iter(jax.random.split(key, 32))
    D = D_MODEL

    def nrm(shape, scale):
        return jax.random.normal(next(ks), shape, jnp.float32) * scale

    x = nrm((BATCH, SEQ, D), 1.0)
    c = nrm((BATCH, D), 1.0)
    w_ada = nrm((DEPTH, D, 6 * D), 0.5 * D ** -0.5)
    b_ada = nrm((DEPTH, 6 * D), 0.02)
    norm_g = 1.0 + nrm((DEPTH, 2, D), 0.02)
    hg_w_in = nrm((N_A_LAYERS, D, 5 * D), D ** -0.5)
    hg_lb = nrm((N_A_LAYERS, D), 1.0)
    hg_norm_g = 1.0 + nrm((N_A_LAYERS, D), 0.02)
    hg_w_out = nrm((N_A_LAYERS, D, D), D ** -0.5)
    lru_w_in = nrm((N_B_LAYERS, D, 2 * LRU_WIDTH), D ** -0.5)
    lru_conv_w = nrm((N_B_LAYERS, CONV_W, LRU_WIDTH), 0.5)
    lru_conv_b = nrm((N_B_LAYERS, LRU_WIDTH), 0.02)
    lru_w_a = nrm((N_B_LAYERS, 2, LRU_BLOCKS, LRU_BW, LRU_BW), LRU_BW ** -0.5)
    lru_b_a = nrm((N_B_LAYERS, 2, LRU_WIDTH), 0.02)
    lru_w_x = nrm((N_B_LAYERS, 2, LRU_BLOCKS, LRU_BW, LRU_BW), LRU_BW ** -0.5)
    lru_b_x = nrm((N_B_LAYERS, 2, LRU_WIDTH), 0.02)
    a_c = jax.random.uniform(next(ks), (N_B_LAYERS, 2, LRU_WIDTH), jnp.float32, 0.9, 0.999)
    a_base = a_c ** (1.0 / LRU_C)
    lru_lam = jnp.log(a_base) - jnp.log1p(-a_base)
    lru_w_out = nrm((N_B_LAYERS, LRU_WIDTH, D), LRU_WIDTH ** -0.5)
    peer_w_q = nrm((DEPTH, D, PEER_HEADS * PEER_DQ), D ** -0.5)
    peer_keys = nrm((DEPTH, PEER_HEADS, 2, PEER_NKEYS, PEER_DQ // 2), (PEER_DQ // 2) ** -0.5)
    peer_u = nrm((DEPTH, PEER_NEXPERTS, D), D ** -0.5)
    peer_v = nrm((DEPTH, PEER_NEXPERTS, D), PEER_HEADS ** -0.5)
    final_g = 1.0 + nrm((D,), 0.02)
    return {"x": x, "c": c, "w_ada": w_ada, "b_ada": b_ada, "norm_g": norm_g,
            "hg_w_in": hg_w_in, "hg_lb": hg_lb, "hg_norm_g": hg_norm_g, "hg_w_out": hg_w_out,
            "lru_w_in": lru_w_in, "lru_conv_w": lru_conv_w, "lru_conv_b": lru_conv_b,
            "lru_w_a": lru_w_a, "lru_b_a": lru_b_a, "lru_w_x": lru_w_x, "lru_b_x": lru_b_x,
            "lru_lam": lru_lam, "lru_w_out": lru_w_out,
            "peer_w_q": peer_w_q, "peer_keys": peer_keys, "peer_u": peer_u, "peer_v": peer_v,
            "final_g": final_g}


def reference(x, c, w_ada, b_ada, norm_g, hg_w_in, hg_lb, hg_norm_g, hg_w_out,
              lru_w_in, lru_conv_w, lru_conv_b, lru_w_a, lru_b_a, lru_w_x, lru_b_x,
              lru_lam, lru_w_out, peer_w_q, peer_keys, peer_u, peer_v, final_g):
    p = jax.nn.softmax(hg_lb.astype(jnp.float32), axis=0)
    lbs = jnp.cumsum(p, axis=0) - p[0]
    cond = jax.nn.silu(c)
    for i in range(DEPTH):
        mod = (cond @ w_ada[i] + b_ada[i])[:, None, :]
        sh1, sc1, g1, sh2, sc2, g2 = jnp.split(mod, 6, axis=-1)
        h = rms_norm(x, norm_g[i, 0]) * (1.0 + sc1) + sh1
        j = i // N_MIXERS
        if i % N_MIXERS == 0:
            y = hgrn2_mixer(h, hg_w_in[j], lbs[j], hg_norm_g[j], hg_w_out[j])
        else:
            y = rglru_mixer(h, lru_w_in[j], lru_conv_w[j], lru_conv_b[j], lru_w_a[j], lru_b_a[j],
                            lru_w_x[j], lru_b_x[j], lru_lam[j], lru_w_out[j])
        x = x + g1 * y
        h = rms_norm(x, norm_g[i, 1]) * (1.0 + sc2) + sh2
        x = x + g2 * peer_ffn(h, peer_w_q[i], peer_keys[i], peer_u[i], peer_v[i])
    return rms_norm(x, final_g)
```

```python
import functools

import numpy as np
import jax
import jax.numpy as jnp
from jax import lax
from jax.experimental import pallas as pl
from jax.experimental.pallas import tpu as pltpu

F32 = jnp.float32
BF16 = jnp.bfloat16

SUBLANES = 8
LANES = 128
VMEM_LIMIT = 56 * 1024 * 1024

EPS = 1e-6
HG_HEADS = 8
HG_CHUNK = 128
LRU_BLOCKS = 4
LRU_C = 8.0
CONV_W = 4
CONV_LEFT = 2
PEER_HEADS = 8
PEER_NKEYS = 128
PEER_TOPK = 16
HG_LEVELS = 7
NEG_INF = float("-inf")


def _cparams(sem):
    return pltpu.CompilerParams(dimension_semantics=sem, vmem_limit_bytes=VMEM_LIMIT)


def _split3(x):
    x1 = x.astype(BF16)
    r1 = x - x1.astype(F32)
    x2 = r1.astype(BF16)
    x3 = (r1 - x2.astype(F32)).astype(BF16)
    return x1, x2, x3


def _split2(x):
    x1 = x.astype(BF16)
    x2 = (x - x1.astype(F32)).astype(BF16)
    return x1, x2


def _dot(a, b):
    return jnp.dot(a, b, preferred_element_type=F32)


def _dot_nt(a, b):
    return lax.dot_general(a, b, (((1,), (1,)), ((), ())), preferred_element_type=F32)


def _dot_tn(a, b):
    return lax.dot_general(a, b, (((0,), (0,)), ((), ())), preferred_element_type=F32)


def _norm_mod(x, gain, sc, sh):
    inv = lax.rsqrt(jnp.mean(x * x, axis=-1, keepdims=True) + EPS)
    return (x * inv) * gain * (1.0 + sc) + sh


def _ada_kernel(c_ref, w_ref, b_ref, o_ref):
    c = c_ref[...]
    cond = c * jax.nn.sigmoid(c)
    c1, c2, c3 = _split3(cond)
    w1, w2, w3 = _split3(w_ref[0])
    acc = _dot(c1, w1) + _dot(c1, w2) + _dot(c2, w1)
    acc = acc + _dot(c1, w3) + _dot(c2, w2) + _dot(c3, w1)
    o_ref[0] = acc + b_ref[0]


def _ada_mod(c, w_ada, b_ada):
    depth, d, n = w_ada.shape
    b = c.shape[0]
    cp = jnp.zeros((SUBLANES, d), F32).at[:b].set(c)
    tn = 1536
    out = pl.pallas_call(
        _ada_kernel,
        grid=(depth, n // tn),
        in_specs=[pl.BlockSpec((SUBLANES, d), lambda i, j: (0, 0)),
                  pl.BlockSpec((1, d, tn), lambda i, j: (i, 0, j)),
                  pl.BlockSpec((1, 1, tn), lambda i, j: (i, 0, j))],
        out_specs=pl.BlockSpec((1, SUBLANES, tn), lambda i, j: (i, 0, j)),
        out_shape=jax.ShapeDtypeStruct((depth, SUBLANES, n), F32),
        compiler_params=_cparams(("arbitrary", "arbitrary")),
        name="ada_mod",
    )(cp, w_ada, b_ada.reshape(depth, 1, n))
    return out[:, :b]


def _nmm_kernel(x_ref, g_ref, sc_ref, sh_ref, w_ref, o_ref, h_scr, *, head_major):
    @pl.when(pl.program_id(1) == 0)
    def _():
        h_scr[...] = _norm_mod(x_ref[...], g_ref[...], sc_ref[0], sh_ref[0]).astype(BF16)

    res = _dot(h_scr[...], w_ref[...])
    if head_major:
        for k in range(o_ref.shape[0]):
            o_ref[k] = res[:, k * LANES:(k + 1) * LANES]
    else:
        o_ref[...] = res


def _norm_mod_matmul(x, gain, sc, sh, w_bf16, seq, *, head_major, tm=512, tn=512):
    t, d = x.shape
    n = w_bf16.shape[1]
    tpb = seq // tm
    if head_major:
        out_shape = jax.ShapeDtypeStruct((n // LANES, t, LANES), F32)
        out_spec = pl.BlockSpec((tn // LANES, tm, LANES), lambda i, j: (j, i, 0))
    else:
        out_shape = jax.ShapeDtypeStruct((t, n), F32)
        out_spec = pl.BlockSpec((tm, tn), lambda i, j: (i, j))
    return pl.pallas_call(
        functools.partial(_nmm_kernel, head_major=head_major),
        grid=(t // tm, n // tn),
        in_specs=[pl.BlockSpec((tm, d), lambda i, j: (i, 0)),
                  pl.BlockSpec((1, d), lambda i, j: (0, 0)),
                  pl.BlockSpec((1, 1, d), lambda i, j: (i // tpb, 0, 0)),
                  pl.BlockSpec((1, 1, d), lambda i, j: (i // tpb, 0, 0)),
                  pl.BlockSpec((d, tn), lambda i, j: (0, j))],
        out_specs=out_spec,
        out_shape=out_shape,
        scratch_shapes=[pltpu.VMEM((tm, d), BF16)],
        compiler_params=_cparams(("arbitrary", "arbitrary")),
        name="norm_mod_matmul",
    )(x, gain.reshape(1, d), sc, sh, w_bf16)


def _hgrn_sum_mats(reverse):
    c = HG_CHUNK
    mats = []
    for lvl in range(1, HG_LEVELS + 1):
        m = 1 << lvl
        half = m >> 1
        a = np.zeros((c, c), np.float32)
        for t in range(c):
            mid = (t // m) * m + half
            if not reverse:
                if t >= mid:
                    a[t, mid:t + 1] = 1.0
                else:
                    a[t, t + 1:mid] = 1.0
            else:
                if t < mid:
                    a[t, t:mid] = 1.0
                else:
                    a[t, mid:t] = 1.0
        mats.append(a)
    ones = np.ones((c, c), np.float32)
    if not reverse:
        mats += [np.tril(ones), np.triu(ones, 1)]
    else:
        mats += [np.triu(ones), np.tril(ones, -1)]
    return np.concatenate(mats, axis=0)


def _hgrn_chain(q, flog, v, lb, a_ref, st_ref, d, h, reverse):
    c = HG_CHUNK
    f = lb + (1.0 - lb) * jax.nn.sigmoid(flog)
    g = jnp.log(f)
    k = 1.0 - f
    g1, g2, g3 = _split3(g)
    a = a_ref[...]
    e = _dot(a, g1) + _dot(a, g2) + _dot(a, g3)
    xdec = jnp.exp(e)
    rows = lax.broadcasted_iota(jnp.int32, (c, c), 0)
    cols = lax.broadcasted_iota(jnp.int32, (c, c), 1)
    attn = jnp.where(rows == cols, jnp.sum(q * k, axis=-1, keepdims=True), 0.0)
    for lvl in range(1, HG_LEVELS + 1):
        xl = xdec[(lvl - 1) * c:lvl * c]
        upper = ((rows >> (lvl - 1)) & 1) == 1
        qmask = jnp.logical_not(upper) if reverse else upper
        ql = jnp.where(qmask, q * xl, 0.0).astype(BF16)
        kl = jnp.where(qmask, 0.0, k * xl).astype(BF16)
        p = _dot_nt(ql, kl)
        attn = attn + jnp.where((rows >> lvl) == (cols >> lvl), p, 0.0)
    eq = e[HG_LEVELS * c:(HG_LEVELS + 1) * c]
    qd = (q * xdec[HG_LEVELS * c:(HG_LEVELS + 1) * c]).astype(BF16)
    kd = (k * xdec[(HG_LEVELS + 1) * c:(HG_LEVELS + 2) * c]).astype(BF16)
    tot = eq[0:1] if reverse else eq[c - 1:c]
    st = st_ref[d, h]
    vb = v.astype(BF16)
    o = _dot(attn.astype(BF16), vb) + _dot_nt(qd, st.astype(BF16))
    st_ref[d, h] = st * jnp.exp(tot) + _dot_tn(vb, kd)
    return o


def _hgrn_scan_kernel(lb_ref, afw_ref, abw_ref, qf_ref, ff_ref, vf_ref, qb_ref, fb_ref, vb_ref,
                      of_ref, ob_ref, st_ref, lbs_ref, *, layer):
    @pl.when(pl.program_id(1) == 0)
    def _():
        st_ref[...] = jnp.zeros_like(st_ref)
        lb = lb_ref[...]
        ex = jnp.exp(lb - jnp.max(lb, axis=0, keepdims=True))
        p = ex / jnp.sum(ex, axis=0, keepdims=True)
        acc = p[0]
        for jj in range(1, layer + 1):
            acc = acc + p[jj]
        lbs_ref[...] = acc - p[0]

    def body(h, carry):
        lb = lbs_ref[pl.ds(h, 1), :]
        of_ref[h] = _hgrn_chain(qf_ref[h], ff_ref[h], vf_ref[h], lb, afw_ref, st_ref, 0, h, False)
        ob_ref[h] = _hgrn_chain(qb_ref[h], fb_ref[h], vb_ref[h], lb, abw_ref, st_ref, 1, h, True)
        return carry

    lax.fori_loop(0, HG_HEADS, body, 0)


def _hgrn_scan(proj_hm, hg_lb, layer, batch, seq):
    hh = HG_HEADS
    t = proj_hm.shape[1]
    nc = seq // HG_CHUNK
    c = HG_CHUNK
    na = hg_lb.shape[0]
    afw = jnp.asarray(_hgrn_sum_mats(False), BF16)
    abw = jnp.asarray(_hgrn_sum_mats(True), BF16)
    nrow = afw.shape[0]

    def fwd(part):
        return pl.BlockSpec((hh, c, LANES), lambda b, cc: (part, b * nc + cc, 0))

    def bwd(part):
        return pl.BlockSpec((hh, c, LANES), lambda b, cc: (part, b * nc + nc - 1 - cc, 0))

    out_sd = jax.ShapeDtypeStruct((hh, t, LANES), F32)
    return pl.pallas_call(
        functools.partial(_hgrn_scan_kernel, layer=layer),
        grid=(batch, nc),
        in_specs=[pl.BlockSpec((na, hh, LANES), lambda b, cc: (0, 0, 0)),
                  pl.BlockSpec((nrow, c), lambda b, cc: (0, 0)),
                  pl.BlockSpec((nrow, c), lambda b, cc: (0, 0)),
                  fwd(0), fwd(1), fwd(3), bwd(0), bwd(2), bwd(3)],
        out_specs=[pl.BlockSpec((hh, c, LANES), lambda b, cc: (0, b * nc + cc, 0)),
                   pl.BlockSpec((hh, c, LANES), lambda b, cc: (0, b * nc + nc - 1 - cc, 0))],
        out_shape=[out_sd, out_sd],
        scratch_shapes=[pltpu.VMEM((2, hh, LANES, LANES), F32), pltpu.VMEM((hh, LANES), F32)],
        compiler_params=_cparams(("arbitrary", "arbitrary")),
        name="hgrn_scan",
    )(hg_lb.reshape(na, hh, LANES), afw, abw, proj_hm, proj_hm, proj_hm, proj_hm, proj_hm, proj_hm)


def _hgrn_out_kernel(of_ref, ob_ref, gg_ref, ng_ref, w_ref, x_ref, g1_ref, o_ref):
    parts = []
    for h in range(HG_HEADS):
        o = of_ref[h] + ob_ref[h]
        inv = lax.rsqrt(jnp.mean(o * o, axis=-1, keepdims=True) + EPS)
        gg = gg_ref[h]
        parts.append(((o * inv) * ng_ref[pl.ds(h, 1), :] * (gg * jax.nn.sigmoid(gg))).astype(BF16))
    y = _dot(jnp.concatenate(parts, axis=-1), w_ref[...])
    o_ref[...] = x_ref[...] + g1_ref[0] * y


def _hgrn_out(o_fw, o_bw, proj_hm, norm_g, w_out_bf16, x, g1, seq, tm=512):
    hh = HG_HEADS
    t, d = x.shape
    tpb = seq // tm
    return pl.pallas_call(
        _hgrn_out_kernel,
        grid=(t // tm,),
        in_specs=[pl.BlockSpec((hh, tm, LANES), lambda i: (0, i, 0)),
                  pl.BlockSpec((hh, tm, LANES), lambda i: (0, i, 0)),
                  pl.BlockSpec((hh, tm, LANES), lambda i: (4, i, 0)),
                  pl.BlockSpec((hh, LANES), lambda i: (0, 0)),
                  pl.BlockSpec((d, d), lambda i: (0, 0)),
                  pl.BlockSpec((tm, d), lambda i: (i, 0)),
                  pl.BlockSpec((1, 1, d), lambda i: (i // tpb, 0, 0))],
        out_specs=pl.BlockSpec((tm, d), lambda i: (i, 0)),
        out_shape=jax.ShapeDtypeStruct((t, d), F32),
        compiler_params=_cparams(("arbitrary",)),
        name="hgrn_out",
    )(o_fw, o_bw, proj_hm, norm_g.reshape(hh, LANES), w_out_bf16, x, g1)


def _lru_group_scan(a, x, carry, rows, reverse):
    for s in (1, 2, 4):
        if not reverse:
            keep = rows >= s
            a_sh = jnp.where(keep, pltpu.roll(a, s, 0), 1.0)
            x_sh = jnp.where(keep, pltpu.roll(x, s, 0), 0.0)
        else:
            keep = rows < SUBLANES - s
            a_sh = jnp.where(keep, pltpu.roll(a, SUBLANES - s, 0), 1.0)
            x_sh = jnp.where(keep, pltpu.roll(x, SUBLANES - s, 0), 0.0)
        x = x + a * x_sh
        a = a * a_sh
    hs = x + a * carry
    new_carry = hs[0:1] if reverse else hs[SUBLANES - 1:SUBLANES]
    return hs, new_carry


def _lru_kernel(xc_ref, xp_ref, xn_ref, cw_ref, cb_ref, wa_ref, ba_ref, wx_ref, bx_ref, lam_ref,
                o_ref, a_scr, b_scr, carry_scr, *, tm, nchunk):
    d = pl.program_id(0)
    cc = pl.program_id(2)
    chunk = jnp.where(d == 0, cc, nchunk - 1 - cc)

    @pl.when(cc == 0)
    def _():
        carry_scr[...] = jnp.zeros_like(carry_scr)

    xcur = xc_ref[...]
    prev = jnp.where(chunk == 0, 0.0, xp_ref[...])
    nxt = jnp.where(chunk == nchunk - 1, 0.0, xn_ref[...])
    ext = jnp.concatenate([prev, xcur, nxt], axis=0)
    cw = cw_ref[...]
    xc = cb_ref[...]
    for j in range(CONV_W):
        off = SUBLANES - CONV_LEFT + j
        xc = xc + ext[off:off + tm] * cw[j:j + 1]

    xcb = xc.astype(BF16)
    bw = xcb.shape[1] // LRU_BLOCKS
    ra, rx = [], []
    for n in range(LRU_BLOCKS):
        blk = xcb[:, n * bw:(n + 1) * bw]
        ra.append(_dot(blk, wa_ref[0, n]))
        rx.append(_dot(blk, wx_ref[0, n]))
    r = jax.nn.sigmoid(jnp.concatenate(ra, axis=-1) + ba_ref[0])
    ig = jax.nn.sigmoid(jnp.concatenate(rx, axis=-1) + bx_ref[0])
    lam = lam_ref[0]
    softplus_neg = jnp.maximum(-lam, 0.0) + jnp.log(1.0 + jnp.exp(-jnp.abs(lam)))
    log_a = -LRU_C * r * softplus_neg
    a = jnp.exp(log_a)
    a_scr[...] = a
    b_scr[...] = jnp.sqrt(-jnp.tanh(log_a) * (a * a + 1.0)) * ig * xc

    ngroups = tm // SUBLANES
    rows = lax.broadcasted_iota(jnp.int32, (SUBLANES, xc.shape[1]), 0)

    def run(reverse):
        def body(i, carry):
            gi = (ngroups - 1 - i) if reverse else i
            sl = pl.ds(pl.multiple_of(gi * SUBLANES, SUBLANES), SUBLANES)
            hs, carry = _lru_group_scan(a_scr[sl, :], b_scr[sl, :], carry, rows, reverse)
            o_ref[0, sl, :] = hs
            return carry
        carry_scr[...] = lax.fori_loop(0, ngroups, body, carry_scr[...])

    @pl.when(d == 0)
    def _():
        run(False)

    @pl.when(d == 1)
    def _():
        run(True)


def _lru_scan(proj, conv_w, conv_b, w_a, b_a, w_x, b_x, lam, batch, seq, tm=256):
    t = proj.shape[0]
    w = proj.shape[1] // 2
    nchunk = seq // tm
    hb = tm // SUBLANES
    nhalo = t // SUBLANES

    def pos(d, b, cc):
        return b * nchunk + jnp.where(d == 0, cc, nchunk - 1 - cc)

    return pl.pallas_call(
        functools.partial(_lru_kernel, tm=tm, nchunk=nchunk),
        grid=(2, batch, nchunk),
        in_specs=[pl.BlockSpec((tm, w), lambda d, b, cc: (pos(d, b, cc), 0)),
                  pl.BlockSpec((SUBLANES, w), lambda d, b, cc: (jnp.maximum(pos(d, b, cc) * hb - 1, 0), 0)),
                  pl.BlockSpec((SUBLANES, w),
                               lambda d, b, cc: (jnp.minimum((pos(d, b, cc) + 1) * hb, nhalo - 1), 0)),
                  pl.BlockSpec((CONV_W, w), lambda d, b, cc: (0, 0)),
                  pl.BlockSpec((1, w), lambda d, b, cc: (0, 0)),
                  pl.BlockSpec((1, LRU_BLOCKS, w // LRU_BLOCKS, w // LRU_BLOCKS), lambda d, b, cc: (d, 0, 0, 0)),
                  pl.BlockSpec((1, 1, w), lambda d, b, cc: (d, 0, 0)),
                  pl.BlockSpec((1, LRU_BLOCKS, w // LRU_BLOCKS, w // LRU_BLOCKS), lambda d, b, cc: (d, 0, 0, 0)),
                  pl.BlockSpec((1, 1, w), lambda d, b, cc: (d, 0, 0)),
                  pl.BlockSpec((1, 1, w), lambda d, b, cc: (d, 0, 0))],
        out_specs=pl.BlockSpec((1, tm, w), lambda d, b, cc: (d, pos(d, b, cc), 0)),
        out_shape=jax.ShapeDtypeStruct((2, t, w), F32),
        scratch_shapes=[pltpu.VMEM((tm, w), F32), pltpu.VMEM((tm, w), F32), pltpu.VMEM((1, w), F32)],
        compiler_params=_cparams(("arbitrary", "arbitrary", "arbitrary")),
        name="lru_scan",
    )(proj, proj, proj, conv_w, conv_b.reshape(1, w), w_a.astype(BF16), b_a.reshape(2, 1, w),
      w_x.astype(BF16), b_x.reshape(2, 1, w), lam.reshape(2, 1, w))


def _lru_out_kernel(hs_ref, yb_ref, w_ref, x_ref, g1_ref, o_ref):
    u = (hs_ref[0] + hs_ref[1]) * jax.nn.gelu(yb_ref[...])
    o_ref[...] = x_ref[...] + g1_ref[0] * _dot(u.astype(BF16), w_ref[...])


def _lru_out(hs, proj, w_out_bf16, x, g1, seq, tm=512):
    t, d = x.shape
    w = hs.shape[2]
    tpb = seq // tm
    return pl.pallas_call(
        _lru_out_kernel,
        grid=(t // tm,),
        in_specs=[pl.BlockSpec((2, tm, w), lambda i: (0, i, 0)),
                  pl.BlockSpec((tm, w), lambda i: (i, 1)),
                  pl.BlockSpec((w, d), lambda i: (0, 0)),
                  pl.BlockSpec((tm, d), lambda i: (i, 0)),
                  pl.BlockSpec((1, 1, d), lambda i: (i // tpb, 0, 0))],
        out_specs=pl.BlockSpec((tm, d), lambda i: (i, 0)),
        out_shape=jax.ShapeDtypeStruct((t, d), F32),
        compiler_params=_cparams(("arbitrary",)),
        name="lru_out",
    )(hs, proj, w_out_bf16, x, g1)


def _peer_q_kernel(x_ref, g_ref, sc_ref, sh_ref, w1_ref, w2_ref, k1_ref, k2_ref, hb_ref, sc_out_ref):
    h = _norm_mod(x_ref[...], g_ref[...], sc_ref[0], sh_ref[0])
    h1, h2 = _split2(h)
    hb_ref[...] = h1
    q = _dot(h1, w1_ref[...]) + _dot(h1, w2_ref[...]) + _dot(h2, w1_ref[...])
    nhp = k1_ref.shape[0]
    nch = sc_out_ref.shape[1]
    for hp in range(nhp):
        qa, qb = _split2(q[:, hp * LANES:(hp + 1) * LANES])
        s = _dot_nt(k1_ref[hp], qa) + _dot_nt(k1_ref[hp], qb) + _dot_nt(k2_ref[hp], qa)
        for ch in range(nch):
            sc_out_ref[hp, ch] = s[:, ch * LANES:(ch + 1) * LANES]


def _peer_q(x, gain, sc, sh, wq1, wq2, keys1, keys2, seq, tm=256):
    t, d = x.shape
    nq = wq1.shape[1]
    nhp = keys1.shape[0]
    tpb = seq // tm
    nch = tm // LANES
    return pl.pallas_call(
        _peer_q_kernel,
        grid=(t // tm,),
        in_specs=[pl.BlockSpec((tm, d), lambda i: (i, 0)),
                  pl.BlockSpec((1, d), lambda i: (0, 0)),
                  pl.BlockSpec((1, 1, d), lambda i: (i // tpb, 0, 0)),
                  pl.BlockSpec((1, 1, d), lambda i: (i // tpb, 0, 0)),
                  pl.BlockSpec((d, nq), lambda i: (0, 0)),
                  pl.BlockSpec((d, nq), lambda i: (0, 0)),
                  pl.BlockSpec((nhp, PEER_NKEYS, LANES), lambda i: (0, 0, 0)),
                  pl.BlockSpec((nhp, PEER_NKEYS, LANES), lambda i: (0, 0, 0))],
        out_specs=[pl.BlockSpec((tm, d), lambda i: (i, 0)),
                   pl.BlockSpec((nhp, nch, PEER_NKEYS, LANES), lambda i: (0, i, 0, 0))],
        out_shape=[jax.ShapeDtypeStruct((t, d), BF16),
                   jax.ShapeDtypeStruct((nhp, t // LANES, PEER_NKEYS, LANES), F32)],
        compiler_params=_cparams(("arbitrary",)),
        name="peer_q",
    )(x, gain.reshape(1, d), sc, sh, wq1, wq2, keys1, keys2)


def _top16(s, rows):
    work = s
    rank = jnp.full(s.shape, PEER_TOPK, jnp.int32)
    vals = []
    for kk in range(PEER_TOPK):
        m = jnp.max(work, axis=0, keepdims=True)
        idx = jnp.min(jnp.where(work == m, rows, PEER_NKEYS), axis=0, keepdims=True)
        sel = rows == idx
        rank = jnp.where(sel, kk, rank)
        work = jnp.where(sel, NEG_INF, work)
        vals.append(m)
    return rank, jnp.concatenate(vals, axis=0)


def _peer_topk_kernel(s_ref, e1_ref, n_ref, e2_ref, r2_ref):
    nch = s_ref.shape[1]
    rows = lax.broadcasted_iota(jnp.int32, (PEER_NKEYS, LANES), 0)
    r16 = lax.broadcasted_iota(jnp.int32, (PEER_TOPK, LANES), 0)
    r8 = lax.broadcasted_iota(jnp.int32, (SUBLANES, LANES), 0)
    flat = jnp.concatenate([r16] + [r8 + PEER_TOPK * k1 for k1 in range(1, 8)]
                           + [(r8 + 8) * PEER_TOPK], axis=0)

    def body(ch, carry):
        s1 = s_ref[0, ch]
        s2 = s_ref[1, ch]
        rank1, v1 = _top16(s1, rows)
        rank2, v2 = _top16(s2, rows)
        cand = jnp.concatenate([v1[0:1] + v2] + [v1[k1:k1 + 1] + v2[0:8] for k1 in range(1, 8)]
                               + [v1[8:16] + v2[0:1]], axis=0)
        mtop = v1[0:1] + v2[0:1]
        cnt = jnp.zeros((PEER_TOPK, LANES), jnp.int32)
        z = jnp.zeros((1, LANES), F32)
        for _ in range(PEER_TOPK):
            m = jnp.max(cand, axis=0, keepdims=True)
            idx = jnp.min(jnp.where(cand == m, flat, PEER_TOPK * PEER_TOPK), axis=0, keepdims=True)
            cand = jnp.where(flat == idx, NEG_INF, cand)
            cnt = cnt + jnp.where(r16 == (idx >> 4), 1, 0)
            z = z + jnp.exp(m - mtop)
        nrow = jnp.zeros((PEER_NKEYS, LANES), jnp.int32)
        for kk in range(PEER_TOPK):
            nrow = jnp.where(rank1 == kk, cnt[kk:kk + 1], nrow)
        e1_ref[0, ch] = jnp.where(rank1 < PEER_TOPK, jnp.exp(s1 - v1[0:1]) / z, 0.0)
        n_ref[0, ch] = nrow.astype(F32)
        e2_ref[0, ch] = jnp.where(rank2 < PEER_TOPK, jnp.exp(s2 - v2[0:1]), 0.0)
        r2_ref[0, ch] = rank2.astype(F32)
        return carry

    lax.fori_loop(0, nch, body, 0)


def _peer_topk(scores_t, nch=4):
    nhp, ntc, nk, _ = scores_t.shape
    hh = nhp // 2
    spec = pl.BlockSpec((1, nch, nk, LANES), lambda i, h: (h, i, 0, 0))
    sd = jax.ShapeDtypeStruct((hh, ntc, nk, LANES), F32)
    return pl.pallas_call(
        _peer_topk_kernel,
        grid=(ntc // nch, hh),
        in_specs=[pl.BlockSpec((2, nch, nk, LANES), lambda i, h: (h, i, 0, 0))],
        out_specs=[spec, spec, spec, spec],
        out_shape=[sd, sd, sd, sd],
        compiler_params=_cparams(("arbitrary", "arbitrary")),
        name="peer_topk",
    )(scores_t)


def _peer_dense_kernel(hb_ref, u_ref, vt_ref, e1_ref, n_ref, e2_ref, r2_ref, x_ref, g2_ref,
                       o_ref, acc_ref, at_ref):
    j = pl.program_id(1)

    @pl.when(j == 0)
    def _():
        acc_ref[...] = jnp.zeros_like(acc_ref)

    act = jax.nn.gelu(_dot_nt(u_ref[...], hb_ref[...]))
    nch = e2_ref.shape[1]
    na = e1_ref.shape[2]
    for ch in range(nch):
        for a in range(na):
            w = jnp.zeros((PEER_NKEYS, LANES), F32)
            for h in range(PEER_HEADS):
                keep = r2_ref[h, ch] < n_ref[h, ch, a:a + 1, :]
                w = w + jnp.where(keep, e2_ref[h, ch], 0.0) * e1_ref[h, ch, a:a + 1, :]
            blk = act[a * PEER_NKEYS:(a + 1) * PEER_NKEYS, ch * LANES:(ch + 1) * LANES] * w
            at_ref[a * PEER_NKEYS:(a + 1) * PEER_NKEYS, ch * LANES:(ch + 1) * LANES] = blk.astype(BF16)
    acc_ref[...] += _dot(vt_ref[...], at_ref[...])

    @pl.when(j == pl.num_programs(1) - 1)
    def _():
        o_ref[...] = x_ref[...] + g2_ref[0] * acc_ref[...].T


def _peer_dense(hb, u_bf16, vt_bf16, e1, nrow, e2, rank2, x, g2, seq, tm=512, na=8):
    t, d = x.shape
    ne = u_bf16.shape[0]
    te = na * PEER_NKEYS
    nch = tm // LANES
    tpb = seq // tm
    hh = PEER_HEADS
    small = pl.BlockSpec((hh, nch, na, LANES), lambda i, j: (0, i, j, 0))
    big = pl.BlockSpec((hh, nch, PEER_NKEYS, LANES), lambda i, j: (0, i, 0, 0))
    return pl.pallas_call(
        _peer_dense_kernel,
        grid=(t // tm, ne // te),
        in_specs=[pl.BlockSpec((tm, d), lambda i, j: (i, 0)),
                  pl.BlockSpec((te, d), lambda i, j: (j, 0)),
                  pl.BlockSpec((d, te), lambda i, j: (0, j)),
                  small, small, big, big,
                  pl.BlockSpec((tm, d), lambda i, j: (i, 0)),
                  pl.BlockSpec((1, 1, d), lambda i, j: (i // tpb, 0, 0))],
        out_specs=pl.BlockSpec((tm, d), lambda i, j: (i, 0)),
        out_shape=jax.ShapeDtypeStruct((t, d), F32),
        scratch_shapes=[pltpu.VMEM((d, tm), F32), pltpu.VMEM((te, tm), BF16)],
        compiler_params=_cparams(("arbitrary", "arbitrary")),
        name="peer_dense",
    )(hb, u_bf16, vt_bf16, e1, nrow, e2, rank2, x, g2)


def _final_norm_kernel(x_ref, g_ref, o_ref):
    x = x_ref[...]
    inv = lax.rsqrt(jnp.mean(x * x, axis=-1, keepdims=True) + EPS)
    o_ref[...] = (x * inv) * g_ref[...]


def _final_norm(x, gain, tm=1024):
    t, d = x.shape
    return pl.pallas_call(
        _final_norm_kernel,
        grid=(t // tm,),
        in_specs=[pl.BlockSpec((tm, d), lambda i: (i, 0)), pl.BlockSpec((1, d), lambda i: (0, 0))],
        out_specs=pl.BlockSpec((tm, d), lambda i: (i, 0)),
        out_shape=jax.ShapeDtypeStruct((t, d), F32),
        compiler_params=_cparams(("arbitrary",)),
        name="final_norm",
    )(x, gain.reshape(1, d))


def _peer_layer(x, gain, sc, sh, g2, w_q, keys, u_tab, v_tab, seq):
    wq1, wq2 = _split2(w_q)
    kf = keys.reshape(-1, PEER_NKEYS, keys.shape[-1])
    k1, k2 = _split2(kf)
    hb, scores_t = _peer_q(x, gain, sc, sh, wq1, wq2, k1, k2, seq)
    e1, nrow, e2, rank2 = _peer_topk(scores_t)
    return _peer_dense(hb, u_tab.astype(BF16), v_tab.T.astype(BF16), e1, nrow, e2, rank2, x, g2, seq)


def kernel(x, c, w_ada, b_ada, norm_g, hg_w_in, hg_lb, hg_norm_g, hg_w_out, lru_w_in, lru_conv_w,
           lru_conv_b, lru_w_a, lru_b_a, lru_w_x, lru_b_x, lru_lam, lru_w_out, peer_w_q, peer_keys,
           peer_u, peer_v, final_g):
    batch, seq, d = x.shape
    depth = w_ada.shape[0]
    n_mixers = 2
    mod = _ada_mod(c, w_ada, b_ada)
    xt = x.reshape(batch * seq, d)
    for i in range(depth):
        parts = [mod[i, :, k * d:(k + 1) * d].reshape(batch, 1, d) for k in range(6)]
        sh1, sc1, g1, sh2, sc2, g2 = parts
        j = i // n_mixers
        if i % n_mixers == 0:
            proj = _norm_mod_matmul(xt, norm_g[i, 0], sc1, sh1, hg_w_in[j].astype(BF16), seq,
                                    head_major=True)
            o_fw, o_bw = _hgrn_scan(proj, hg_lb, j, batch, seq)
            xt = _hgrn_out(o_fw, o_bw, proj, hg_norm_g[j], hg_w_out[j].astype(BF16), xt, g1, seq)
        else:
            proj = _norm_mod_matmul(xt, norm_g[i, 0], sc1, sh1, lru_w_in[j].astype(BF16), seq,
                                    head_major=False)
            hs = _lru_scan(proj, lru_conv_w[j], lru_conv_b[j], lru_w_a[j], lru_b_a[j], lru_w_x[j],
                           lru_b_x[j], lru_lam[j], batch, seq)
            xt = _lru_out(hs, proj, lru_w_out[j].astype(BF16), xt, g1, seq)
        xt = _peer_layer(xt, norm_g[i, 1], sc2, sh2, g2, peer_w_q[i], peer_keys[i], peer_u[i],
                         peer_v[i], seq)
    return _final_norm(xt, final_g).reshape(batch, seq, d)
```

```python
import functools

import numpy as np
import jax
import jax.numpy as jnp
from jax import lax
from jax.experimental import pallas as pl
from jax.experimental.pallas import tpu as pltpu

F32 = jnp.float32
BF16 = jnp.bfloat16

SUBLANES = 8
LANES = 128
VMEM_LIMIT = 56 * 1024 * 1024

EPS = 1e-6
HG_HEADS = 8
HG_CHUNK = 128
LRU_BLOCKS = 4
LRU_C = 8.0
CONV_W = 4
CONV_LEFT = 2
PEER_HEADS = 8
PEER_NKEYS = 128
PEER_TOPK = 16
HG_LEVELS = 7
NEG_INF = float("-inf")


def _cparams(sem):
    return pltpu.CompilerParams(dimension_semantics=sem, vmem_limit_bytes=VMEM_LIMIT)


def _split3(x):
    x1 = x.astype(BF16)
    r1 = x - x1.astype(F32)
    x2 = r1.astype(BF16)
    x3 = (r1 - x2.astype(F32)).astype(BF16)
    return x1, x2, x3


def _split2(x):
    x1 = x.astype(BF16)
    x2 = (x - x1.astype(F32)).astype(BF16)
    return x1, x2


def _dot(a, b):
    return jnp.dot(a, b, preferred_element_type=F32)


def _dot_nt(a, b):
    return lax.dot_general(a, b, (((1,), (1,)), ((), ())), preferred_element_type=F32)


def _dot_tn(a, b):
    return lax.dot_general(a, b, (((0,), (0,)), ((), ())), preferred_element_type=F32)


def _norm_mod(x, gain, sc, sh):
    inv = lax.rsqrt(jnp.mean(x * x, axis=-1, keepdims=True) + EPS)
    return (x * inv) * gain * (1.0 + sc) + sh


def _ada_kernel(c_ref, w_ref, b_ref, o_ref):
    c = c_ref[...]
    cond = c * jax.nn.sigmoid(c)
    c1, c2, c3 = _split3(cond)
    w1, w2, w3 = _split3(w_ref[0])
    acc = _dot(c1, w1) + _dot(c1, w2) + _dot(c2, w1)
    acc = acc + _dot(c1, w3) + _dot(c2, w2) + _dot(c3, w1)
    o_ref[0] = acc + b_ref[0]


def _ada_mod(c, w_ada, b_ada):
    depth, d, n = w_ada.shape
    b = c.shape[0]
    cp = jnp.zeros((SUBLANES, d), F32).at[:b].set(c)
    tn = 1536
    out = pl.pallas_call(
        _ada_kernel,
        grid=(depth, n // tn),
        in_specs=[pl.BlockSpec((SUBLANES, d), lambda i, j: (0, 0)),
                  pl.BlockSpec((1, d, tn), lambda i, j: (i, 0, j)),
                  pl.BlockSpec((1, 1, tn), lambda i, j: (i, 0, j))],
        out_specs=pl.BlockSpec((1, SUBLANES, tn), lambda i, j: (i, 0, j)),
        out_shape=jax.ShapeDtypeStruct((depth, SUBLANES, n), F32),
        compiler_params=_cparams(("arbitrary", "arbitrary")),
        name="ada_mod",
    )(cp, w_ada, b_ada.reshape(depth, 1, n))
    return out[:, :b]


def _nmm_kernel(x_ref, g_ref, sc_ref, sh_ref, w_ref, o_ref, h_scr, *, head_major):
    @pl.when(pl.program_id(1) == 0)
    def _():
        h_scr[...] = _norm_mod(x_ref[...], g_ref[...], sc_ref[0], sh_ref[0]).astype(BF16)

    res = _dot(h_scr[...], w_ref[...])
    if head_major:
        for k in range(o_ref.shape[0]):
            o_ref[k] = res[:, k * LANES:(k + 1) * LANES]
    else:
        o_ref[...] = res


def _norm_mod_matmul(x, gain, sc, sh, w_bf16, seq, *, head_major, tm=512, tn=512):
    t, d = x.shape
    n = w_bf16.shape[1]
    tpb = seq // tm
    if head_major:
        out_shape = jax.ShapeDtypeStruct((n // LANES, t, LANES), F32)
        out_spec = pl.BlockSpec((tn // LANES, tm, LANES), lambda i, j: (j, i, 0))
    else:
        out_shape = jax.ShapeDtypeStruct((t, n), F32)
        out_spec = pl.BlockSpec((tm, tn), lambda i, j: (i, j))
    return pl.pallas_call(
        functools.partial(_nmm_kernel, head_major=head_major),
        grid=(t // tm, n // tn),
        in_specs=[pl.BlockSpec((tm, d), lambda i, j: (i, 0)),
                  pl.BlockSpec((1, d), lambda i, j: (0, 0)),
                  pl.BlockSpec((1, 1, d), lambda i, j: (i // tpb, 0, 0)),
                  pl.BlockSpec((1, 1, d), lambda i, j: (i // tpb, 0, 0)),
                  pl.BlockSpec((d, tn), lambda i, j: (0, j))],
        out_specs=out_spec,
        out_shape=out_shape,
        scratch_shapes=[pltpu.VMEM((tm, d), BF16)],
        compiler_params=_cparams(("arbitrary", "arbitrary")),
        name="norm_mod_matmul",
    )(x, gain.reshape(1, d), sc, sh, w_bf16)


def _hgrn_sum_mats(reverse):
    c = HG_CHUNK
    mats = []
    for lvl in range(1, HG_LEVELS + 1):
        m = 1 << lvl
        half = m >> 1
        a = np.zeros((c, c), np.float32)
        for t in range(c):
            mid = (t // m) * m + half
            if not reverse:
                if t >= mid:
                    a[t, mid:t + 1] = 1.0
                else:
                    a[t, t + 1:mid] = 1.0
            else:
                if t < mid:
                    a[t, t:mid] = 1.0
                else:
                    a[t, mid:t] = 1.0
        mats.append(a)
    ones = np.ones((c, c), np.float32)
    if not reverse:
        mats += [np.tril(ones), np.triu(ones, 1)]
    else:
        mats += [np.triu(ones), np.tril(ones, -1)]
    return np.concatenate(mats, axis=0)


def _hgrn_chain(q, flog, v, lb, a_ref, st_ref, d, h, reverse):
    c = HG_CHUNK
    f = lb + (1.0 - lb) * jax.nn.sigmoid(flog)
    g = jnp.log(f)
    k = 1.0 - f
    g1, g2 = _split2(g)
    a = a_ref[...]
    e = _dot(a, g1) + _dot(a, g2)
    xdec = jnp.exp(e)
    rows = lax.broadcasted_iota(jnp.int32, (c, c), 0)
    cols = lax.broadcasted_iota(jnp.int32, (c, c), 1)
    attn = jnp.where(rows == cols, jnp.sum(q * k, axis=-1, keepdims=True), 0.0)
    for lvl in range(1, HG_LEVELS + 1):
        xl = xdec[(lvl - 1) * c:lvl * c]
        upper = ((rows >> (lvl - 1)) & 1) == 1
        qmask = jnp.logical_not(upper) if reverse else upper
        ql = jnp.where(qmask, q * xl, 0.0).astype(BF16)
        kl = jnp.where(qmask, 0.0, k * xl).astype(BF16)
        p = _dot_nt(ql, kl)
        attn = attn + jnp.where((rows >> lvl) == (cols >> lvl), p, 0.0)
    eq = e[HG_LEVELS * c:(HG_LEVELS + 1) * c]
    qd = (q * xdec[HG_LEVELS * c:(HG_LEVELS + 1) * c]).astype(BF16)
    kd = (k * xdec[(HG_LEVELS + 1) * c:(HG_LEVELS + 2) * c]).astype(BF16)
    tot = eq[0:1] if reverse else eq[c - 1:c]
    st = st_ref[d, h]
    vb = v.astype(BF16)
    o = _dot(attn.astype(BF16), vb) + _dot_nt(qd, st.astype(BF16))
    st_ref[d, h] = st * jnp.exp(tot) + _dot_tn(vb, kd)
    return o


def _hgrn_scan_kernel(lb_ref, afw_ref, abw_ref, qf_ref, ff_ref, vf_ref, qb_ref, fb_ref, vb_ref,
                      of_ref, ob_ref, st_ref, lbs_ref, *, layer):
    @pl.when(pl.program_id(1) == 0)
    def _():
        st_ref[...] = jnp.zeros_like(st_ref)
        lb = lb_ref[...]
        ex = jnp.exp(lb - jnp.max(lb, axis=0, keepdims=True))
        p = ex / jnp.sum(ex, axis=0, keepdims=True)
        acc = p[0]
        for jj in range(1, layer + 1):
            acc = acc + p[jj]
        lbs_ref[...] = acc - p[0]

    def body(h, carry):
        lb = lbs_ref[pl.ds(h, 1), :]
        of_ref[h] = _hgrn_chain(qf_ref[h], ff_ref[h], vf_ref[h], lb, afw_ref, st_ref, 0, h, False)
        ob_ref[h] = _hgrn_chain(qb_ref[h], fb_ref[h], vb_ref[h], lb, abw_ref, st_ref, 1, h, True)
        return carry

    lax.fori_loop(0, HG_HEADS, body, 0, unroll=8)


def _hgrn_scan(proj_hm, hg_lb, layer, batch, seq):
    hh = HG_HEADS
    t = proj_hm.shape[1]
    nc = seq // HG_CHUNK
    c = HG_CHUNK
    na = hg_lb.shape[0]
    afw = jnp.asarray(_hgrn_sum_mats(False), BF16)
    abw = jnp.asarray(_hgrn_sum_mats(True), BF16)
    nrow = afw.shape[0]

    def fwd(part):
        return pl.BlockSpec((hh, c, LANES), lambda b, cc: (part, b * nc + cc, 0))

    def bwd(part):
        return pl.BlockSpec((hh, c, LANES), lambda b, cc: (part, b * nc + nc - 1 - cc, 0))

    out_sd = jax.ShapeDtypeStruct((hh, t, LANES), F32)
    return pl.pallas_call(
        functools.partial(_hgrn_scan_kernel, layer=layer),
        grid=(batch, nc),
        in_specs=[pl.BlockSpec((na, hh, LANES), lambda b, cc: (0, 0, 0)),
                  pl.BlockSpec((nrow, c), lambda b, cc: (0, 0)),
                  pl.BlockSpec((nrow, c), lambda b, cc: (0, 0)),
                  fwd(0), fwd(1), fwd(3), bwd(0), bwd(2), bwd(3)],
        out_specs=[pl.BlockSpec((hh, c, LANES), lambda b, cc: (0, b * nc + cc, 0)),
                   pl.BlockSpec((hh, c, LANES), lambda b, cc: (0, b * nc + nc - 1 - cc, 0))],
        out_shape=[out_sd, out_sd],
        scratch_shapes=[pltpu.VMEM((2, hh, LANES, LANES), F32), pltpu.VMEM((hh, LANES), F32)],
        compiler_params=_cparams(("arbitrary", "arbitrary")),
        name="hgrn_scan",
    )(hg_lb.reshape(na, hh, LANES), afw, abw, proj_hm, proj_hm, proj_hm, proj_hm, proj_hm, proj_hm)


def _hgrn_out_kernel(of_ref, ob_ref, gg_ref, ng_ref, w_ref, x_ref, g1_ref, o_ref):
    parts = []
    for h in range(HG_HEADS):
        o = of_ref[h] + ob_ref[h]
        inv = lax.rsqrt(jnp.mean(o * o, axis=-1, keepdims=True) + EPS)
        gg = gg_ref[h]
        parts.append(((o * inv) * ng_ref[pl.ds(h, 1), :] * (gg * jax.nn.sigmoid(gg))).astype(BF16))
    y = _dot(jnp.concatenate(parts, axis=-1), w_ref[...])
    o_ref[...] = x_ref[...] + g1_ref[0] * y


def _hgrn_out(o_fw, o_bw, proj_hm, norm_g, w_out_bf16, x, g1, seq, tm=512):
    hh = HG_HEADS
    t, d = x.shape
    tpb = seq // tm
    return pl.pallas_call(
        _hgrn_out_kernel,
        grid=(t // tm,),
        in_specs=[pl.BlockSpec((hh, tm, LANES), lambda i: (0, i, 0)),
                  pl.BlockSpec((hh, tm, LANES), lambda i: (0, i, 0)),
                  pl.BlockSpec((hh, tm, LANES), lambda i: (4, i, 0)),
                  pl.BlockSpec((hh, LANES), lambda i: (0, 0)),
                  pl.BlockSpec((d, d), lambda i: (0, 0)),
                  pl.BlockSpec((tm, d), lambda i: (i, 0)),
                  pl.BlockSpec((1, 1, d), lambda i: (i // tpb, 0, 0))],
        out_specs=pl.BlockSpec((tm, d), lambda i: (i, 0)),
        out_shape=jax.ShapeDtypeStruct((t, d), F32),
        compiler_params=_cparams(("arbitrary",)),
        name="hgrn_out",
    )(o_fw, o_bw, proj_hm, norm_g.reshape(hh, LANES), w_out_bf16, x, g1)


def _lru_group_scan(a, x, carry, rows, reverse):
    for s in (1, 2, 4):
        if not reverse:
            keep = rows >= s
            a_sh = jnp.where(keep, pltpu.roll(a, s, 0), 1.0)
            x_sh = jnp.where(keep, pltpu.roll(x, s, 0), 0.0)
        else:
            keep = rows < SUBLANES - s
            a_sh = jnp.where(keep, pltpu.roll(a, SUBLANES - s, 0), 1.0)
            x_sh = jnp.where(keep, pltpu.roll(x, SUBLANES - s, 0), 0.0)
        x = x + a * x_sh
        a = a * a_sh
    hs = x + a * carry
    new_carry = hs[0:1] if reverse else hs[SUBLANES - 1:SUBLANES]
    return hs, new_carry


def _lru_kernel(xc_ref, xp_ref, xn_ref, cw_ref, cb_ref, wa_ref, ba_ref, wx_ref, bx_ref, lam_ref,
                o_ref, a_scr, b_scr, carry_scr, *, tm, nchunk):
    d = pl.program_id(0)
    cc = pl.program_id(2)
    chunk = jnp.where(d == 0, cc, nchunk - 1 - cc)

    @pl.when(cc == 0)
    def _():
        carry_scr[...] = jnp.zeros_like(carry_scr)

    xcur = xc_ref[...]
    prev = jnp.where(chunk == 0, 0.0, xp_ref[...])
    nxt = jnp.where(chunk == nchunk - 1, 0.0, xn_ref[...])
    ext = jnp.concatenate([prev, xcur, nxt], axis=0)
    cw = cw_ref[...]
    xc = cb_ref[...]
    for j in range(CONV_W):
        off = SUBLANES - CONV_LEFT + j
        xc = xc + ext[off:off + tm] * cw[j:j + 1]

    xcb = xc.astype(BF16)
    bw = xcb.shape[1] // LRU_BLOCKS
    ra, rx = [], []
    for n in range(LRU_BLOCKS):
        blk = xcb[:, n * bw:(n + 1) * bw]
        ra.append(_dot(blk, wa_ref[0, n]))
        rx.append(_dot(blk, wx_ref[0, n]))
    r = jax.nn.sigmoid(jnp.concatenate(ra, axis=-1) + ba_ref[0])
    ig = jax.nn.sigmoid(jnp.concatenate(rx, axis=-1) + bx_ref[0])
    lam = lam_ref[0]
    softplus_neg = jnp.maximum(-lam, 0.0) + jnp.log(1.0 + jnp.exp(-jnp.abs(lam)))
    log_a = -LRU_C * r * softplus_neg
    a = jnp.exp(log_a)
    a_scr[...] = a
    b_scr[...] = jnp.sqrt(-jnp.tanh(log_a) * (a * a + 1.0)) * ig * xc

    ngroups = tm // SUBLANES
    rows = lax.broadcasted_iota(jnp.int32, (SUBLANES, xc.shape[1]), 0)

    def run(reverse):
        def body(i, carry):
            gi = (ngroups - 1 - i) if reverse else i
            sl = pl.ds(pl.multiple_of(gi * SUBLANES, SUBLANES), SUBLANES)
            hs, carry = _lru_group_scan(a_scr[sl, :], b_scr[sl, :], carry, rows, reverse)
            o_ref[0, sl, :] = hs
            return carry
        carry_scr[...] = lax.fori_loop(0, ngroups, body, carry_scr[...])

    @pl.when(d == 0)
    def _():
        run(False)

    @pl.when(d == 1)
    def _():
        run(True)


def _lru_scan(proj, conv_w, conv_b, w_a, b_a, w_x, b_x, lam, batch, seq, tm=256):
    t = proj.shape[0]
    w = proj.shape[1] // 2
    nchunk = seq // tm
    hb = tm // SUBLANES
    nhalo = t // SUBLANES

    def pos(d, b, cc):
        return b * nchunk + jnp.where(d == 0, cc, nchunk - 1 - cc)

    return pl.pallas_call(
        functools.partial(_lru_kernel, tm=tm, nchunk=nchunk),
        grid=(2, batch, nchunk),
        in_specs=[pl.BlockSpec((tm, w), lambda d, b, cc: (pos(d, b, cc), 0)),
                  pl.BlockSpec((SUBLANES, w), lambda d, b, cc: (jnp.maximum(pos(d, b, cc) * hb - 1, 0), 0)),
                  pl.BlockSpec((SUBLANES, w),
                               lambda d, b, cc: (jnp.minimum((pos(d, b, cc) + 1) * hb, nhalo - 1), 0)),
                  pl.BlockSpec((CONV_W, w), lambda d, b, cc: (0, 0)),
                  pl.BlockSpec((1, w), lambda d, b, cc: (0, 0)),
                  pl.BlockSpec((1, LRU_BLOCKS, w // LRU_BLOCKS, w // LRU_BLOCKS), lambda d, b, cc: (d, 0, 0, 0)),
                  pl.BlockSpec((1, 1, w), lambda d, b, cc: (d, 0, 0)),
                  pl.BlockSpec((1, LRU_BLOCKS, w // LRU_BLOCKS, w // LRU_BLOCKS), lambda d, b, cc: (d, 0, 0, 0)),
                  pl.BlockSpec((1, 1, w), lambda d, b, cc: (d, 0, 0)),
                  pl.BlockSpec((1, 1, w), lambda d, b, cc: (d, 0, 0))],
        out_specs=pl.BlockSpec((1, tm, w), lambda d, b, cc: (d, pos(d, b, cc), 0)),
        out_shape=jax.ShapeDtypeStruct((2, t, w), F32),
        scratch_shapes=[pltpu.VMEM((tm, w), F32), pltpu.VMEM((tm, w), F32), pltpu.VMEM((1, w), F32)],
        compiler_params=_cparams(("arbitrary", "arbitrary", "arbitrary")),
        name="lru_scan",
    )(proj, proj, proj, conv_w, conv_b.reshape(1, w), w_a.astype(BF16), b_a.reshape(2, 1, w),
      w_x.astype(BF16), b_x.reshape(2, 1, w), lam.reshape(2, 1, w))


def _lru_out_kernel(hs_ref, yb_ref, w_ref, x_ref, g1_ref, o_ref):
    u = (hs_ref[0] + hs_ref[1]) * jax.nn.gelu(yb_ref[...])
    o_ref[...] = x_ref[...] + g1_ref[0] * _dot(u.astype(BF16), w_ref[...])


def _lru_out(hs, proj, w_out_bf16, x, g1, seq, tm=512):
    t, d = x.shape
    w = hs.shape[2]
    tpb = seq // tm
    return pl.pallas_call(
        _lru_out_kernel,
        grid=(t // tm,),
        in_specs=[pl.BlockSpec((2, tm, w), lambda i: (0, i, 0)),
                  pl.BlockSpec((tm, w), lambda i: (i, 1)),
                  pl.BlockSpec((w, d), lambda i: (0, 0)),
                  pl.BlockSpec((tm, d), lambda i: (i, 0)),
                  pl.BlockSpec((1, 1, d), lambda i: (i // tpb, 0, 0))],
        out_specs=pl.BlockSpec((tm, d), lambda i: (i, 0)),
        out_shape=jax.ShapeDtypeStruct((t, d), F32),
        compiler_params=_cparams(("arbitrary",)),
        name="lru_out",
    )(hs, proj, w_out_bf16, x, g1)


def _peer_q_kernel(x_ref, g_ref, sc_ref, sh_ref, w1_ref, w2_ref, k1_ref, k2_ref, hb_ref, sc_out_ref):
    h = _norm_mod(x_ref[...], g_ref[...], sc_ref[0], sh_ref[0])
    h1, h2 = _split2(h)
    hb_ref[...] = h.T.astype(BF16)
    q = _dot(h1, w1_ref[...]) + _dot(h1, w2_ref[...]) + _dot(h2, w1_ref[...])
    nhp = k1_ref.shape[0]
    nch = sc_out_ref.shape[1]
    for hp in range(nhp):
        qa, qb = _split2(q[:, hp * LANES:(hp + 1) * LANES])
        s = _dot_nt(k1_ref[hp], qa) + _dot_nt(k1_ref[hp], qb) + _dot_nt(k2_ref[hp], qa)
        for ch in range(nch):
            sc_out_ref[hp, ch] = s[:, ch * LANES:(ch + 1) * LANES]


def _peer_q(x, gain, sc, sh, wq1, wq2, keys1, keys2, seq, tm=256):
    t, d = x.shape
    nq = wq1.shape[1]
    nhp = keys1.shape[0]
    tpb = seq // tm
    nch = tm // LANES
    return pl.pallas_call(
        _peer_q_kernel,
        grid=(t // tm,),
        in_specs=[pl.BlockSpec((tm, d), lambda i: (i, 0)),
                  pl.BlockSpec((1, d), lambda i: (0, 0)),
                  pl.BlockSpec((1, 1, d), lambda i: (i // tpb, 0, 0)),
                  pl.BlockSpec((1, 1, d), lambda i: (i // tpb, 0, 0)),
                  pl.BlockSpec((d, nq), lambda i: (0, 0)),
                  pl.BlockSpec((d, nq), lambda i: (0, 0)),
                  pl.BlockSpec((nhp, PEER_NKEYS, LANES), lambda i: (0, 0, 0)),
                  pl.BlockSpec((nhp, PEER_NKEYS, LANES), lambda i: (0, 0, 0))],
        out_specs=[pl.BlockSpec((d, tm), lambda i: (0, i)),
                   pl.BlockSpec((nhp, nch, PEER_NKEYS, LANES), lambda i: (0, i, 0, 0))],
        out_shape=[jax.ShapeDtypeStruct((d, t), BF16),
                   jax.ShapeDtypeStruct((nhp, t // LANES, PEER_NKEYS, LANES), F32)],
        compiler_params=_cparams(("arbitrary",)),
        name="peer_q",
    )(x, gain.reshape(1, d), sc, sh, wq1, wq2, keys1, keys2)


def _top16(s, rows, exact):
    work = s
    rank = jnp.full(s.shape, PEER_TOPK, jnp.int32)
    vals = []
    for kk in range(PEER_TOPK):
        m = jnp.max(work, axis=0, keepdims=True)
        if exact:
            idx = jnp.min(jnp.where(work == m, rows, PEER_NKEYS), axis=0, keepdims=True)
            sel = rows == idx
        else:
            sel = work == m
        rank = jnp.where(sel, kk, rank)
        work = jnp.where(sel, NEG_INF, work)
        vals.append(m)
    return rank, jnp.concatenate(vals, axis=0)


def _col_count(mask):
    return jnp.sum(jnp.where(mask, 1.0, 0.0), axis=0, keepdims=True)


def _any_lane_differs(count, want):
    return jnp.max(jnp.where(count == want, 0.0, 1.0)) > 0.5


def _cand16(cand, flat, r16, mtop, exact):
    cnt = jnp.zeros((PEER_TOPK, LANES), F32)
    z = jnp.zeros((1, LANES), F32)
    for _ in range(PEER_TOPK):
        m = jnp.max(cand, axis=0, keepdims=True)
        if exact:
            idx = jnp.min(jnp.where(cand == m, flat, PEER_TOPK * PEER_TOPK), axis=0, keepdims=True)
            cand = jnp.where(flat == idx, NEG_INF, cand)
            cnt = cnt + jnp.where(r16 == (idx >> 4), 1.0, 0.0)
        else:
            cand = jnp.where(cand == m, NEG_INF, cand)
        z = z + jnp.exp(m - mtop)
    if not exact:
        gone = cand == NEG_INF
        cnt = jnp.concatenate(
            [_col_count(gone[0:PEER_TOPK])]
            + [_col_count(gone[PEER_TOPK + SUBLANES * (k1 - 1):PEER_TOPK + SUBLANES * k1]) for k1 in range(1, 8)]
            + [jnp.where(gone[PEER_TOPK + 7 * SUBLANES:], 1.0, 0.0)], axis=0)
    return cnt, z


def _bf16_pair(x):
    u = pltpu.bitcast(x.astype(BF16).astype(F32), jnp.uint32)
    return u | (u >> 16)


def _peer_topk_kernel(s_ref, e1_ref, n_ref, e2_ref, r2_ref):
    nch = s_ref.shape[1]
    rows = lax.broadcasted_iota(jnp.int32, (PEER_NKEYS, LANES), 0)
    r16 = lax.broadcasted_iota(jnp.int32, (PEER_TOPK, LANES), 0)
    r8 = lax.broadcasted_iota(jnp.int32, (SUBLANES, LANES), 0)
    flat = jnp.concatenate([r16] + [r8 + PEER_TOPK * k1 for k1 in range(1, 8)]
                           + [(r8 + 8) * PEER_TOPK], axis=0)
    want = float(PEER_TOPK)

    def body(ch, carry):
        s1 = s_ref[0, ch]
        s2 = s_ref[1, ch]
        rank1, v1 = _top16(s1, rows, False)
        rank2, v2 = _top16(s2, rows, False)
        tied = jnp.logical_or(_any_lane_differs(_col_count(rank1 < PEER_TOPK), want),
                              _any_lane_differs(_col_count(rank2 < PEER_TOPK), want))
        rank1, v1, rank2, v2 = lax.cond(
            tied, lambda: _top16(s1, rows, True) + _top16(s2, rows, True), lambda: (rank1, v1, rank2, v2))
        cand = jnp.concatenate([v1[0:1] + v2] + [v1[k1:k1 + 1] + v2[0:8] for k1 in range(1, 8)]
                               + [v1[8:16] + v2[0:1]], axis=0)
        mtop = v1[0:1] + v2[0:1]
        cnt, z = _cand16(cand, flat, r16, mtop, False)
        tied = _any_lane_differs(jnp.sum(cnt, axis=0, keepdims=True), want)
        cnt, z = lax.cond(tied, lambda: _cand16(cand, flat, r16, mtop, True), lambda: (cnt, z))
        nrow = jnp.zeros((PEER_NKEYS, LANES), F32)
        for kk in range(PEER_TOPK):
            nrow = jnp.where(rank1 == kk, cnt[kk:kk + 1], nrow)
        e1 = jnp.where(rank1 < PEER_TOPK, jnp.exp(s1 - v1[0:1]) / z, 0.0)
        e1_ref[0, ch] = _bf16_pair(e1)
        n_ref[0, ch] = _bf16_pair(nrow)
        e2 = jnp.where(rank2 < PEER_TOPK, jnp.exp(s2 - v2[0:1]), 0.0)
        e2_ref[0, ch] = pltpu.bitcast(e2.astype(BF16), jnp.uint32)
        r2_ref[0, ch] = pltpu.bitcast(rank2.astype(F32).astype(BF16), jnp.uint32)
        return carry

    lax.fori_loop(0, nch, body, 0)


def _peer_topk(scores_t, nch=4):
    nhp, ntc, nk, _ = scores_t.shape
    hh = nhp // 2
    spec = pl.BlockSpec((1, nch, nk, LANES), lambda i, h: (h, i, 0, 0))
    spec_p = pl.BlockSpec((1, nch, nk // 2, LANES), lambda i, h: (h, i, 0, 0))
    sd_u = jax.ShapeDtypeStruct((hh, ntc, nk, LANES), jnp.uint32)
    sd_b = jax.ShapeDtypeStruct((hh, ntc, nk // 2, LANES), jnp.uint32)
    return pl.pallas_call(
        _peer_topk_kernel,
        grid=(ntc // nch, hh),
        in_specs=[pl.BlockSpec((2, nch, nk, LANES), lambda i, h: (h, i, 0, 0))],
        out_specs=[spec, spec, spec_p, spec_p],
        out_shape=[sd_u, sd_u, sd_b, sd_b],
        compiler_params=_cparams(("arbitrary", "arbitrary")),
        name="peer_topk",
    )(scores_t)


def _dup_rows(row_u32, nrows):
    return pltpu.bitcast(jnp.broadcast_to(row_u32, (nrows // 2, row_u32.shape[1])), BF16)


PEER_PB = 2


def _peer_dense_kernel(hbt_ref, u_ref, vt_ref, e1_ref, n_ref, e2_ref, r2_ref, x_ref, g2_ref,
                       o_ref, acc_ref, act0, act1, ab0, ab1):
    j = pl.program_id(1)

    @pl.when(j == 0)
    def _():
        acc_ref[...] = jnp.zeros_like(acc_ref)

    nch = e2_ref.shape[1]
    nb = u_ref.shape[0]
    acts = (act0, act1)
    abs_ = (ab0, ab1)
    zero = jnp.zeros((), BF16)

    def act_mm(p):
        acts[p % 2][...] = _dot(u_ref[p], hbt_ref[...])

    def out_mm(p):
        acc_ref[...] += _dot(vt_ref[p], abs_[p % 2][...])

    def gate(p):
        for ch in range(nch):
            for a2 in range(PEER_PB):
                a = p * PEER_PB + a2
                w = jnp.zeros((PEER_NKEYS, LANES), BF16)
                for h in range(PEER_HEADS):
                    keep = pltpu.bitcast(r2_ref[h, ch], BF16) < _dup_rows(n_ref[h, ch, a:a + 1, :], PEER_NKEYS)
                    w = w + (jnp.where(keep, pltpu.bitcast(e2_ref[h, ch], BF16), zero)
                             * _dup_rows(e1_ref[h, ch, a:a + 1, :], PEER_NKEYS))
                rows = slice(a2 * PEER_NKEYS, (a2 + 1) * PEER_NKEYS)
                cols = slice(ch * LANES, (ch + 1) * LANES)
                abs_[p % 2][rows, cols] = jax.nn.gelu(acts[p % 2][rows, cols].astype(BF16)) * w

    act_mm(0)
    for p in range(nb):
        if p + 1 < nb:
            act_mm(p + 1)
        if p >= 1:
            out_mm(p - 1)
        gate(p)
    out_mm(nb - 1)

    @pl.when(j == pl.num_programs(1) - 1)
    def _():
        o_ref[...] = x_ref[...] + g2_ref[0] * acc_ref[...].T


def _peer_dense(hbt, u_blk, vt_blk, e1, nrow, e2, rank2, x, g2, seq, tm=512, na=8):
    t, d = x.shape
    nblk, rows, _ = u_blk.shape
    nb = na // PEER_PB
    nch = tm // LANES
    tpb = seq // tm
    hh = PEER_HEADS
    small = pl.BlockSpec((hh, nch, na, LANES), lambda i, j: (0, i, j, 0))
    big = pl.BlockSpec((hh, nch, PEER_NKEYS // 2, LANES), lambda i, j: (0, i, 0, 0))
    return pl.pallas_call(
        _peer_dense_kernel,
        grid=(t // tm, nblk // nb),
        in_specs=[pl.BlockSpec((d, tm), lambda i, j: (0, i)),
                  pl.BlockSpec((nb, rows, d), lambda i, j: (j, 0, 0)),
                  pl.BlockSpec((nb, d, rows), lambda i, j: (j, 0, 0)),
                  small, small, big, big,
                  pl.BlockSpec((tm, d), lambda i, j: (i, 0)),
                  pl.BlockSpec((1, 1, d), lambda i, j: (i // tpb, 0, 0))],
        out_specs=pl.BlockSpec((tm, d), lambda i, j: (i, 0)),
        out_shape=jax.ShapeDtypeStruct((t, d), F32),
        scratch_shapes=[pltpu.VMEM((d, tm), F32), pltpu.VMEM((rows, tm), F32), pltpu.VMEM((rows, tm), F32),
                        pltpu.VMEM((rows, tm), BF16), pltpu.VMEM((rows, tm), BF16)],
        compiler_params=_cparams(("arbitrary", "arbitrary")),
        name="peer_dense",
    )(hbt, u_blk, vt_blk, e1, nrow, e2, rank2, x, g2)


def _final_norm_kernel(x_ref, g_ref, o_ref):
    x = x_ref[...]
    inv = lax.rsqrt(jnp.mean(x * x, axis=-1, keepdims=True) + EPS)
    o_ref[...] = (x * inv) * g_ref[...]


def _final_norm(x, gain, tm=1024):
    t, d = x.shape
    return pl.pallas_call(
        _final_norm_kernel,
        grid=(t // tm,),
        in_specs=[pl.BlockSpec((tm, d), lambda i: (i, 0)), pl.BlockSpec((1, d), lambda i: (0, 0))],
        out_specs=pl.BlockSpec((tm, d), lambda i: (i, 0)),
        out_shape=jax.ShapeDtypeStruct((t, d), F32),
        compiler_params=_cparams(("arbitrary",)),
        name="final_norm",
    )(x, gain.reshape(1, d))


def _peer_layer(x, gain, sc, sh, g2, w_q, keys, u_tab, v_tab, seq):
    wq1, wq2 = _split2(w_q)
    kf = keys.reshape(-1, PEER_NKEYS, keys.shape[-1])
    k1, k2 = _split2(kf)
    hb, scores_t = _peer_q(x, gain, sc, sh, wq1, wq2, k1, k2, seq)
    e1, nrow, e2, rank2 = _peer_topk(scores_t)
    rows = PEER_PB * PEER_NKEYS
    ne, d = u_tab.shape
    u_blk = u_tab.astype(BF16).reshape(ne // rows, rows, d)
    vt_blk = v_tab.reshape(ne // rows, rows, d).transpose(0, 2, 1).astype(BF16)
    return _peer_dense(hb, u_blk, vt_blk, e1, nrow, e2, rank2, x, g2, seq)


def kernel(x, c, w_ada, b_ada, norm_g, hg_w_in, hg_lb, hg_norm_g, hg_w_out, lru_w_in, lru_conv_w,
           lru_conv_b, lru_w_a, lru_b_a, lru_w_x, lru_b_x, lru_lam, lru_w_out, peer_w_q, peer_keys,
           peer_u, peer_v, final_g):
    batch, seq, d = x.shape
    depth = w_ada.shape[0]
    n_mixers = 2
    mod = _ada_mod(c, w_ada, b_ada)
    xt = x.reshape(batch * seq, d)
    for i in range(depth):
        parts = [mod[i, :, k * d:(k + 1) * d].reshape(batch, 1, d) for k in range(6)]
        sh1, sc1, g1, sh2, sc2, g2 = parts
        j = i // n_mixers
        if i % n_mixers == 0:
            proj = _norm_mod_matmul(xt, norm_g[i, 0], sc1, sh1, hg_w_in[j].astype(BF16), seq,
                                    head_major=True)
            o_fw, o_bw = _hgrn_scan(proj, hg_lb, j, batch, seq)
            xt = _hgrn_out(o_fw, o_bw, proj, hg_norm_g[j], hg_w_out[j].astype(BF16), xt, g1, seq)
        else:
            proj = _norm_mod_matmul(xt, norm_g[i, 0], sc1, sh1, lru_w_in[j].astype(BF16), seq,
                                    head_major=False)
            hs = _lru_scan(proj, lru_conv_w[j], lru_conv_b[j], lru_w_a[j], lru_b_a[j], lru_w_x[j],
                           lru_b_x[j], lru_lam[j], batch, seq)
            xt = _lru_out(hs, proj, lru_w_out[j].astype(BF16), xt, g1, seq)
        xt = _peer_layer(xt, norm_g[i, 1], sc2, sh2, g2, peer_w_q[i], peer_keys[i], peer_u[i],
                         peer_v[i], seq)
    return _final_norm(xt, final_g).reshape(batch, seq, d)
```

```python
import functools

import numpy as np
import jax
import jax.numpy as jnp
from jax import lax
from jax.experimental import pallas as pl
from jax.experimental.pallas import tpu as pltpu

F32 = jnp.float32
BF16 = jnp.bfloat16

SUBLANES = 8
LANES = 128
VMEM_LIMIT = 56 * 1024 * 1024

EPS = 1e-6
HG_HEADS = 8
HG_CHUNK = 128
LRU_BLOCKS = 4
LRU_C = 8.0
CONV_W = 4
CONV_LEFT = 2
PEER_HEADS = 8
PEER_NKEYS = 128
PEER_TOPK = 16
HG_LEVELS = 7
NEG_INF = float("-inf")


def _cparams(sem):
    return pltpu.CompilerParams(dimension_semantics=sem, vmem_limit_bytes=VMEM_LIMIT)


def _split3(x):
    x1 = x.astype(BF16)
    r1 = x - x1.astype(F32)
    x2 = r1.astype(BF16)
    x3 = (r1 - x2.astype(F32)).astype(BF16)
    return x1, x2, x3


def _split2(x):
    x1 = x.astype(BF16)
    x2 = (x - x1.astype(F32)).astype(BF16)
    return x1, x2


def _dot(a, b):
    return jnp.dot(a, b, preferred_element_type=F32)


def _dot_nt(a, b):
    return lax.dot_general(a, b, (((1,), (1,)), ((), ())), preferred_element_type=F32)


def _dot_tn(a, b):
    return lax.dot_general(a, b, (((0,), (0,)), ((), ())), preferred_element_type=F32)


def _norm_mod(x, gain, sc, sh):
    inv = lax.rsqrt(jnp.mean(x * x, axis=-1, keepdims=True) + EPS)
    return (x * inv) * gain * (1.0 + sc) + sh


def _ada_kernel(c_ref, w_ref, b_ref, o_ref):
    c = c_ref[...]
    cond = c * jax.nn.sigmoid(c)
    c1, c2, c3 = _split3(cond)
    w1, w2, w3 = _split3(w_ref[0])
    acc = _dot(c1, w1) + _dot(c1, w2) + _dot(c2, w1)
    acc = acc + _dot(c1, w3) + _dot(c2, w2) + _dot(c3, w1)
    o_ref[0] = acc + b_ref[0]


def _ada_mod(c, w_ada, b_ada):
    depth, d, n = w_ada.shape
    b = c.shape[0]
    cp = jnp.zeros((SUBLANES, d), F32).at[:b].set(c)
    tn = 1536
    out = pl.pallas_call(
        _ada_kernel,
        grid=(depth, n // tn),
        in_specs=[pl.BlockSpec((SUBLANES, d), lambda i, j: (0, 0)),
                  pl.BlockSpec((1, d, tn), lambda i, j: (i, 0, j)),
                  pl.BlockSpec((1, 1, tn), lambda i, j: (i, 0, j))],
        out_specs=pl.BlockSpec((1, SUBLANES, tn), lambda i, j: (i, 0, j)),
        out_shape=jax.ShapeDtypeStruct((depth, SUBLANES, n), F32),
        compiler_params=_cparams(("arbitrary", "arbitrary")),
        name="ada_mod",
    )(cp, w_ada, b_ada.reshape(depth, 1, n))
    return out[:, :b]


def _nmm_kernel(x_ref, g_ref, sc_ref, sh_ref, w_ref, o_ref, h_scr, *, head_major):
    @pl.when(pl.program_id(1) == 0)
    def _():
        h_scr[...] = _norm_mod(x_ref[...], g_ref[...], sc_ref[0], sh_ref[0]).astype(BF16)

    res = _dot(h_scr[...], w_ref[...])
    if head_major:
        for k in range(o_ref.shape[0]):
            o_ref[k] = res[:, k * LANES:(k + 1) * LANES]
    else:
        o_ref[...] = res


def _norm_mod_matmul(x, gain, sc, sh, w_bf16, seq, *, head_major, tm=1024, tn=1024):
    t, d = x.shape
    n = w_bf16.shape[1]
    tpb = seq // tm
    if head_major:
        out_shape = jax.ShapeDtypeStruct((n // LANES, t, LANES), F32)
        out_spec = pl.BlockSpec((tn // LANES, tm, LANES), lambda i, j: (j, i, 0))
    else:
        out_shape = jax.ShapeDtypeStruct((t, n), F32)
        out_spec = pl.BlockSpec((tm, tn), lambda i, j: (i, j))
    return pl.pallas_call(
        functools.partial(_nmm_kernel, head_major=head_major),
        grid=(t // tm, n // tn),
        in_specs=[pl.BlockSpec((tm, d), lambda i, j: (i, 0)),
                  pl.BlockSpec((1, d), lambda i, j: (0, 0)),
                  pl.BlockSpec((1, 1, d), lambda i, j: (i // tpb, 0, 0)),
                  pl.BlockSpec((1, 1, d), lambda i, j: (i // tpb, 0, 0)),
                  pl.BlockSpec((d, tn), lambda i, j: (0, j))],
        out_specs=out_spec,
        out_shape=out_shape,
        scratch_shapes=[pltpu.VMEM((tm, d), BF16)],
        compiler_params=_cparams(("arbitrary", "arbitrary")),
        name="norm_mod_matmul",
    )(x, gain.reshape(1, d), sc, sh, w_bf16)


def _hgrn_sum_mats(reverse):
    c = HG_CHUNK
    mats = []
    for lvl in range(1, HG_LEVELS + 1):
        m = 1 << lvl
        half = m >> 1
        a = np.zeros((c, c), np.float32)
        for t in range(c):
            mid = (t // m) * m + half
            if not reverse:
                if t >= mid:
                    a[t, mid:t + 1] = 1.0
                else:
                    a[t, t + 1:mid] = 1.0
            else:
                if t < mid:
                    a[t, t:mid] = 1.0
                else:
                    a[t, mid:t] = 1.0
        mats.append(a)
    ones = np.ones((c, c), np.float32)
    if not reverse:
        mats += [np.tril(ones), np.triu(ones, 1)]
    else:
        mats += [np.triu(ones), np.tril(ones, -1)]
    return np.concatenate(mats, axis=0)


def _hgrn_chain(q, flog, v, lb, a_ref, st_ref, d, h, reverse):
    c = HG_CHUNK
    f = lb + (1.0 - lb) * jax.nn.sigmoid(flog)
    g = jnp.log(f)
    k = 1.0 - f
    g1, g2 = _split2(g)
    e2 = _dot(a_ref[...], jnp.concatenate([g1, g2], axis=1))
    e = e2[:, :LANES] + e2[:, LANES:]
    xdec = jnp.exp(e)
    rows = lax.broadcasted_iota(jnp.int32, (c, c), 0)
    cols = lax.broadcasted_iota(jnp.int32, (c, c), 1)
    attn = jnp.where(rows == cols, jnp.sum(q * k, axis=-1, keepdims=True), 0.0)
    for lvl in range(1, HG_LEVELS + 1):
        xl = xdec[(lvl - 1) * c:lvl * c]
        upper = ((rows >> (lvl - 1)) & 1) == 1
        qmask = jnp.logical_not(upper) if reverse else upper
        ql = jnp.where(qmask, q * xl, 0.0).astype(BF16)
        kl = jnp.where(qmask, 0.0, k * xl).astype(BF16)
        p = _dot_nt(ql, kl)
        attn = attn + jnp.where((rows >> lvl) == (cols >> lvl), p, 0.0)
    eq = e[HG_LEVELS * c:(HG_LEVELS + 1) * c]
    qd = (q * xdec[HG_LEVELS * c:(HG_LEVELS + 1) * c]).astype(BF16)
    kd = (k * xdec[(HG_LEVELS + 1) * c:(HG_LEVELS + 2) * c]).astype(BF16)
    tot = eq[0:1] if reverse else eq[c - 1:c]
    st = st_ref[d, h]
    vb = v.astype(BF16)
    o = _dot(attn.astype(BF16), vb) + _dot_nt(qd, st.astype(BF16))
    st_ref[d, h] = st * jnp.exp(tot) + _dot_tn(vb, kd)
    return o


def _hgrn_scan_kernel(lb_ref, afw_ref, abw_ref, qf_ref, ff_ref, vf_ref, qb_ref, fb_ref, vb_ref,
                      of_ref, ob_ref, st_ref, lbs_ref, *, layer):
    @pl.when(pl.program_id(1) == 0)
    def _():
        st_ref[...] = jnp.zeros_like(st_ref)
        lb = lb_ref[...]
        ex = jnp.exp(lb - jnp.max(lb, axis=0, keepdims=True))
        p = ex / jnp.sum(ex, axis=0, keepdims=True)
        acc = p[0]
        for jj in range(1, layer + 1):
            acc = acc + p[jj]
        lbs_ref[...] = acc - p[0]

    def body(h, carry):
        lb = lbs_ref[pl.ds(h, 1), :]
        of_ref[h] = _hgrn_chain(qf_ref[h], ff_ref[h], vf_ref[h], lb, afw_ref, st_ref, 0, h, False)
        ob_ref[h] = _hgrn_chain(qb_ref[h], fb_ref[h], vb_ref[h], lb, abw_ref, st_ref, 1, h, True)
        return carry

    lax.fori_loop(0, HG_HEADS, body, 0, unroll=8)


def _hgrn_scan(proj_hm, hg_lb, layer, batch, seq):
    hh = HG_HEADS
    t = proj_hm.shape[1]
    nc = seq // HG_CHUNK
    c = HG_CHUNK
    na = hg_lb.shape[0]
    afw = jnp.asarray(_hgrn_sum_mats(False), BF16)
    abw = jnp.asarray(_hgrn_sum_mats(True), BF16)
    nrow = afw.shape[0]

    def fwd(part):
        return pl.BlockSpec((hh, c, LANES), lambda b, cc: (part, b * nc + cc, 0))

    def bwd(part):
        return pl.BlockSpec((hh, c, LANES), lambda b, cc: (part, b * nc + nc - 1 - cc, 0))

    out_sd = jax.ShapeDtypeStruct((hh, t, LANES), F32)
    return pl.pallas_call(
        functools.partial(_hgrn_scan_kernel, layer=layer),
        grid=(batch, nc),
        in_specs=[pl.BlockSpec((na, hh, LANES), lambda b, cc: (0, 0, 0)),
                  pl.BlockSpec((nrow, c), lambda b, cc: (0, 0)),
                  pl.BlockSpec((nrow, c), lambda b, cc: (0, 0)),
                  fwd(0), fwd(1), fwd(3), bwd(0), bwd(2), bwd(3)],
        out_specs=[pl.BlockSpec((hh, c, LANES), lambda b, cc: (0, b * nc + cc, 0)),
                   pl.BlockSpec((hh, c, LANES), lambda b, cc: (0, b * nc + nc - 1 - cc, 0))],
        out_shape=[out_sd, out_sd],
        scratch_shapes=[pltpu.VMEM((2, hh, LANES, LANES), F32), pltpu.VMEM((hh, LANES), F32)],
        compiler_params=_cparams(("arbitrary", "arbitrary")),
        name="hgrn_scan",
    )(hg_lb.reshape(na, hh, LANES), afw, abw, proj_hm, proj_hm, proj_hm, proj_hm, proj_hm, proj_hm)


def _hgrn_out_kernel(of_ref, ob_ref, gg_ref, ng_ref, w_ref, x_ref, g1_ref, o_ref):
    parts = []
    for h in range(HG_HEADS):
        o = of_ref[h] + ob_ref[h]
        inv = lax.rsqrt(jnp.mean(o * o, axis=-1, keepdims=True) + EPS)
        gg = gg_ref[h]
        parts.append(((o * inv) * ng_ref[pl.ds(h, 1), :] * (gg * jax.nn.sigmoid(gg))).astype(BF16))
    y = _dot(jnp.concatenate(parts, axis=-1), w_ref[...])
    o_ref[...] = x_ref[...] + g1_ref[0] * y


def _hgrn_out(o_fw, o_bw, proj_hm, norm_g, w_out_bf16, x, g1, seq, tm=512):
    hh = HG_HEADS
    t, d = x.shape
    tpb = seq // tm
    return pl.pallas_call(
        _hgrn_out_kernel,
        grid=(t // tm,),
        in_specs=[pl.BlockSpec((hh, tm, LANES), lambda i: (0, i, 0)),
                  pl.BlockSpec((hh, tm, LANES), lambda i: (0, i, 0)),
                  pl.BlockSpec((hh, tm, LANES), lambda i: (4, i, 0)),
                  pl.BlockSpec((hh, LANES), lambda i: (0, 0)),
                  pl.BlockSpec((d, d), lambda i: (0, 0)),
                  pl.BlockSpec((tm, d), lambda i: (i, 0)),
                  pl.BlockSpec((1, 1, d), lambda i: (i // tpb, 0, 0))],
        out_specs=pl.BlockSpec((tm, d), lambda i: (i, 0)),
        out_shape=jax.ShapeDtypeStruct((t, d), F32),
        compiler_params=_cparams(("arbitrary",)),
        name="hgrn_out",
    )(o_fw, o_bw, proj_hm, norm_g.reshape(hh, LANES), w_out_bf16, x, g1)


def _lru_group_scan(a, x, carry, rows, reverse):
    for s in (1, 2, 4):
        if not reverse:
            keep = rows >= s
            a_sh = jnp.where(keep, pltpu.roll(a, s, 0), 1.0)
            x_sh = jnp.where(keep, pltpu.roll(x, s, 0), 0.0)
        else:
            keep = rows < SUBLANES - s
            a_sh = jnp.where(keep, pltpu.roll(a, SUBLANES - s, 0), 1.0)
            x_sh = jnp.where(keep, pltpu.roll(x, SUBLANES - s, 0), 0.0)
        x = x + a * x_sh
        a = a * a_sh
    hs = x + a * carry
    new_carry = hs[0:1] if reverse else hs[SUBLANES - 1:SUBLANES]
    return hs, new_carry


def _lru_kernel(xc_ref, xp_ref, xn_ref, cw_ref, cb_ref, wa_ref, ba_ref, wx_ref, bx_ref, lam_ref,
                o_ref, a_scr, b_scr, carry_scr, *, tm, nchunk):
    d = pl.program_id(0)
    cc = pl.program_id(2)
    chunk = jnp.where(d == 0, cc, nchunk - 1 - cc)

    @pl.when(cc == 0)
    def _():
        carry_scr[...] = jnp.zeros_like(carry_scr)

    xcur = xc_ref[...]
    prev = jnp.where(chunk == 0, 0.0, xp_ref[...])
    nxt = jnp.where(chunk == nchunk - 1, 0.0, xn_ref[...])
    ext = jnp.concatenate([prev, xcur, nxt], axis=0)
    cw = cw_ref[...]
    xc = cb_ref[...]
    for j in range(CONV_W):
        off = SUBLANES - CONV_LEFT + j
        xc = xc + ext[off:off + tm] * cw[j:j + 1]

    xcb = xc.astype(BF16)
    bw = xcb.shape[1] // LRU_BLOCKS
    ra, rx = [], []
    for n in range(LRU_BLOCKS):
        blk = xcb[:, n * bw:(n + 1) * bw]
        ra.append(_dot(blk, wa_ref[0, n]))
        rx.append(_dot(blk, wx_ref[0, n]))
    r = jax.nn.sigmoid(jnp.concatenate(ra, axis=-1) + ba_ref[0])
    ig = jax.nn.sigmoid(jnp.concatenate(rx, axis=-1) + bx_ref[0])
    lam = lam_ref[0]
    softplus_neg = jnp.maximum(-lam, 0.0) + jnp.log(1.0 + jnp.exp(-jnp.abs(lam)))
    log_a = -LRU_C * r * softplus_neg
    a = jnp.exp(log_a)
    a_scr[...] = a
    b_scr[...] = jnp.sqrt(-jnp.tanh(log_a) * (a * a + 1.0)) * ig * xc

    ngroups = tm // SUBLANES
    rows = lax.broadcasted_iota(jnp.int32, (SUBLANES, xc.shape[1]), 0)

    def run(reverse):
        def body(i, carry):
            gi = (ngroups - 1 - i) if reverse else i
            sl = pl.ds(pl.multiple_of(gi * SUBLANES, SUBLANES), SUBLANES)
            hs, carry = _lru_group_scan(a_scr[sl, :], b_scr[sl, :], carry, rows, reverse)
            o_ref[0, sl, :] = hs
            return carry
        carry_scr[...] = lax.fori_loop(0, ngroups, body, carry_scr[...])

    @pl.when(d == 0)
    def _():
        run(False)

    @pl.when(d == 1)
    def _():
        run(True)


def _lru_scan(proj, conv_w, conv_b, w_a, b_a, w_x, b_x, lam, batch, seq, tm=256):
    t = proj.shape[0]
    w = proj.shape[1] // 2
    nchunk = seq // tm
    hb = tm // SUBLANES
    nhalo = t // SUBLANES

    def pos(d, b, cc):
        return b * nchunk + jnp.where(d == 0, cc, nchunk - 1 - cc)

    return pl.pallas_call(
        functools.partial(_lru_kernel, tm=tm, nchunk=nchunk),
        grid=(2, batch, nchunk),
        in_specs=[pl.BlockSpec((tm, w), lambda d, b, cc: (pos(d, b, cc), 0)),
                  pl.BlockSpec((SUBLANES, w), lambda d, b, cc: (jnp.maximum(pos(d, b, cc) * hb - 1, 0), 0)),
                  pl.BlockSpec((SUBLANES, w),
                               lambda d, b, cc: (jnp.minimum((pos(d, b, cc) + 1) * hb, nhalo - 1), 0)),
                  pl.BlockSpec((CONV_W, w), lambda d, b, cc: (0, 0)),
                  pl.BlockSpec((1, w), lambda d, b, cc: (0, 0)),
                  pl.BlockSpec((1, LRU_BLOCKS, w // LRU_BLOCKS, w // LRU_BLOCKS), lambda d, b, cc: (d, 0, 0, 0)),
                  pl.BlockSpec((1, 1, w), lambda d, b, cc: (d, 0, 0)),
                  pl.BlockSpec((1, LRU_BLOCKS, w // LRU_BLOCKS, w // LRU_BLOCKS), lambda d, b, cc: (d, 0, 0, 0)),
                  pl.BlockSpec((1, 1, w), lambda d, b, cc: (d, 0, 0)),
                  pl.BlockSpec((1, 1, w), lambda d, b, cc: (d, 0, 0))],
        out_specs=pl.BlockSpec((1, tm, w), lambda d, b, cc: (d, pos(d, b, cc), 0)),
        out_shape=jax.ShapeDtypeStruct((2, t, w), F32),
        scratch_shapes=[pltpu.VMEM((tm, w), F32), pltpu.VMEM((tm, w), F32), pltpu.VMEM((1, w), F32)],
        compiler_params=_cparams(("arbitrary", "arbitrary", "arbitrary")),
        name="lru_scan",
    )(proj, proj, proj, conv_w, conv_b.reshape(1, w), w_a.astype(BF16), b_a.reshape(2, 1, w),
      w_x.astype(BF16), b_x.reshape(2, 1, w), lam.reshape(2, 1, w))


def _lru_out_kernel(hs_ref, yb_ref, w_ref, x_ref, g1_ref, o_ref):
    u = (hs_ref[0] + hs_ref[1]) * jax.nn.gelu(yb_ref[...])
    o_ref[...] = x_ref[...] + g1_ref[0] * _dot(u.astype(BF16), w_ref[...])


def _lru_out(hs, proj, w_out_bf16, x, g1, seq, tm=512):
    t, d = x.shape
    w = hs.shape[2]
    tpb = seq // tm
    return pl.pallas_call(
        _lru_out_kernel,
        grid=(t // tm,),
        in_specs=[pl.BlockSpec((2, tm, w), lambda i: (0, i, 0)),
                  pl.BlockSpec((tm, w), lambda i: (i, 1)),
                  pl.BlockSpec((w, d), lambda i: (0, 0)),
                  pl.BlockSpec((tm, d), lambda i: (i, 0)),
                  pl.BlockSpec((1, 1, d), lambda i: (i // tpb, 0, 0))],
        out_specs=pl.BlockSpec((tm, d), lambda i: (i, 0)),
        out_shape=jax.ShapeDtypeStruct((t, d), F32),
        compiler_params=_cparams(("arbitrary",)),
        name="lru_out",
    )(hs, proj, w_out_bf16, x, g1)


def _peer_q_kernel(x_ref, g_ref, sc_ref, sh_ref, w1_ref, w2_ref, k1_ref, k2_ref, hb_ref, sc_out_ref):
    h = _norm_mod(x_ref[...], g_ref[...], sc_ref[0], sh_ref[0])
    h1, h2 = _split2(h)
    hb_ref[...] = h.T.astype(BF16)
    q = _dot(h1, w1_ref[...]) + _dot(h1, w2_ref[...]) + _dot(h2, w1_ref[...])
    nhp = k1_ref.shape[0]
    nch = sc_out_ref.shape[1]
    for hp in range(nhp):
        qa, qb = _split2(q[:, hp * LANES:(hp + 1) * LANES])
        s = _dot_nt(k1_ref[hp], qa) + _dot_nt(k1_ref[hp], qb) + _dot_nt(k2_ref[hp], qa)
        for ch in range(nch):
            sc_out_ref[hp, ch] = s[:, ch * LANES:(ch + 1) * LANES]


def _peer_q(x, gain, sc, sh, wq1, wq2, keys1, keys2, seq, tm=256):
    t, d = x.shape
    nq = wq1.shape[1]
    nhp = keys1.shape[0]
    tpb = seq // tm
    nch = tm // LANES
    return pl.pallas_call(
        _peer_q_kernel,
        grid=(t // tm,),
        in_specs=[pl.BlockSpec((tm, d), lambda i: (i, 0)),
                  pl.BlockSpec((1, d), lambda i: (0, 0)),
                  pl.BlockSpec((1, 1, d), lambda i: (i // tpb, 0, 0)),
                  pl.BlockSpec((1, 1, d), lambda i: (i // tpb, 0, 0)),
                  pl.BlockSpec((d, nq), lambda i: (0, 0)),
                  pl.BlockSpec((d, nq), lambda i: (0, 0)),
                  pl.BlockSpec((nhp, PEER_NKEYS, LANES), lambda i: (0, 0, 0)),
                  pl.BlockSpec((nhp, PEER_NKEYS, LANES), lambda i: (0, 0, 0))],
        out_specs=[pl.BlockSpec((d, tm), lambda i: (0, i)),
                   pl.BlockSpec((nhp, nch, PEER_NKEYS, LANES), lambda i: (0, i, 0, 0))],
        out_shape=[jax.ShapeDtypeStruct((d, t), BF16),
                   jax.ShapeDtypeStruct((nhp, t // LANES, PEER_NKEYS, LANES), F32)],
        compiler_params=_cparams(("arbitrary",)),
        name="peer_q",
    )(x, gain.reshape(1, d), sc, sh, wq1, wq2, keys1, keys2)


def _top16(s, rows, exact):
    work = s
    rank = jnp.full(s.shape, PEER_TOPK, jnp.int32)
    vals = []
    for kk in range(PEER_TOPK):
        m = jnp.max(work, axis=0, keepdims=True)
        if exact:
            idx = jnp.min(jnp.where(work == m, rows, PEER_NKEYS), axis=0, keepdims=True)
            sel = rows == idx
        else:
            sel = work == m
        rank = jnp.where(sel, kk, rank)
        work = jnp.where(sel, NEG_INF, work)
        vals.append(m)
    return rank, jnp.concatenate(vals, axis=0)


def _col_count(mask):
    return jnp.sum(jnp.where(mask, 1.0, 0.0), axis=0, keepdims=True)


def _any_lane_differs(count, want):
    return jnp.max(jnp.where(count == want, 0.0, 1.0)) > 0.5


def _cand16(cand, flat, r16, mtop, exact):
    cnt = jnp.zeros((PEER_TOPK, LANES), F32)
    z = jnp.zeros((1, LANES), F32)
    for _ in range(PEER_TOPK):
        m = jnp.max(cand, axis=0, keepdims=True)
        if exact:
            idx = jnp.min(jnp.where(cand == m, flat, PEER_TOPK * PEER_TOPK), axis=0, keepdims=True)
            cand = jnp.where(flat == idx, NEG_INF, cand)
            cnt = cnt + jnp.where(r16 == (idx >> 4), 1.0, 0.0)
        else:
            cand = jnp.where(cand == m, NEG_INF, cand)
        z = z + jnp.exp(m - mtop)
    if not exact:
        gone = cand == NEG_INF
        cnt = jnp.concatenate(
            [_col_count(gone[0:PEER_TOPK])]
            + [_col_count(gone[PEER_TOPK + SUBLANES * (k1 - 1):PEER_TOPK + SUBLANES * k1]) for k1 in range(1, 8)]
            + [jnp.where(gone[PEER_TOPK + 7 * SUBLANES:], 1.0, 0.0)], axis=0)
    return cnt, z


def _bf16_pair(x):
    u = pltpu.bitcast(x.astype(BF16).astype(F32), jnp.uint32)
    return u | (u >> 16)


def _peer_topk_kernel(s_ref, e1_ref, n_ref, e2_ref, r2_ref):
    nch = s_ref.shape[1]
    rows = lax.broadcasted_iota(jnp.int32, (PEER_NKEYS, LANES), 0)
    r16 = lax.broadcasted_iota(jnp.int32, (PEER_TOPK, LANES), 0)
    r8 = lax.broadcasted_iota(jnp.int32, (SUBLANES, LANES), 0)
    flat = jnp.concatenate([r16] + [r8 + PEER_TOPK * k1 for k1 in range(1, 8)]
                           + [(r8 + 8) * PEER_TOPK], axis=0)
    want = float(PEER_TOPK)

    def body(ch, carry):
        s1 = s_ref[0, ch]
        s2 = s_ref[1, ch]
        rank1, v1 = _top16(s1, rows, False)
        rank2, v2 = _top16(s2, rows, False)
        tied = jnp.logical_or(_any_lane_differs(_col_count(rank1 < PEER_TOPK), want),
                              _any_lane_differs(_col_count(rank2 < PEER_TOPK), want))
        rank1, v1, rank2, v2 = lax.cond(
            tied, lambda: _top16(s1, rows, True) + _top16(s2, rows, True), lambda: (rank1, v1, rank2, v2))
        cand = jnp.concatenate([v1[0:1] + v2] + [v1[k1:k1 + 1] + v2[0:8] for k1 in range(1, 8)]
                               + [v1[8:16] + v2[0:1]], axis=0)
        mtop = v1[0:1] + v2[0:1]
        cnt, z = _cand16(cand, flat, r16, mtop, False)
        tied = _any_lane_differs(jnp.sum(cnt, axis=0, keepdims=True), want)
        cnt, z = lax.cond(tied, lambda: _cand16(cand, flat, r16, mtop, True), lambda: (cnt, z))
        nrow = jnp.zeros((PEER_NKEYS, LANES), F32)
        for kk in range(PEER_TOPK):
            nrow = jnp.where(rank1 == kk, cnt[kk:kk + 1], nrow)
        e1 = jnp.where(rank1 < PEER_TOPK, jnp.exp(s1 - v1[0:1]) / z, 0.0)
        e1_ref[0, ch] = _bf16_pair(e1)
        n_ref[0, ch] = _bf16_pair(nrow)
        e2 = jnp.where(rank2 < PEER_TOPK, jnp.exp(s2 - v2[0:1]), 0.0)
        e2_ref[0, ch] = pltpu.bitcast(e2.astype(BF16), jnp.uint32)
        r2_ref[0, ch] = pltpu.bitcast(rank2.astype(F32).astype(BF16), jnp.uint32)
        return carry

    lax.fori_loop(0, nch, body, 0)


def _peer_topk(scores_t, nch=4):
    nhp, ntc, nk, _ = scores_t.shape
    hh = nhp // 2
    spec = pl.BlockSpec((1, nch, nk, LANES), lambda i, h: (h, i, 0, 0))
    spec_p = pl.BlockSpec((1, nch, nk // 2, LANES), lambda i, h: (h, i, 0, 0))
    sd_u = jax.ShapeDtypeStruct((hh, ntc, nk, LANES), jnp.uint32)
    sd_b = jax.ShapeDtypeStruct((hh, ntc, nk // 2, LANES), jnp.uint32)
    return pl.pallas_call(
        _peer_topk_kernel,
        grid=(ntc // nch, hh),
        in_specs=[pl.BlockSpec((2, nch, nk, LANES), lambda i, h: (h, i, 0, 0))],
        out_specs=[spec, spec, spec_p, spec_p],
        out_shape=[sd_u, sd_u, sd_b, sd_b],
        compiler_params=_cparams(("arbitrary", "arbitrary")),
        name="peer_topk",
    )(scores_t)


def _dup_rows(row_u32, nrows):
    return pltpu.bitcast(jnp.broadcast_to(row_u32, (nrows // 2, row_u32.shape[1])), BF16)


PEER_PB = 2


def _peer_dense_kernel(hbt_ref, u_ref, v_ref, e1_ref, n_ref, e2_ref, r2_ref, x_ref, g2_ref,
                       o_ref, acc_ref, act0, act1, ab0, ab1):
    j = pl.program_id(1)

    @pl.when(j == 0)
    def _():
        acc_ref[...] = jnp.zeros_like(acc_ref)

    nch = e2_ref.shape[1]
    nb = u_ref.shape[0]
    acts = (act0, act1)
    abs_ = (ab0, ab1)
    zero = jnp.zeros((), BF16)

    def act_mm(p):
        acts[p % 2][...] = _dot(u_ref[p], hbt_ref[...])

    def out_mm(p):
        acc_ref[...] += _dot_tn(abs_[p % 2][...], v_ref[p])

    def gate(p):
        for ch in range(nch):
            for a2 in range(PEER_PB):
                a = p * PEER_PB + a2
                w = jnp.zeros((PEER_NKEYS, LANES), BF16)
                for h in range(PEER_HEADS):
                    keep = pltpu.bitcast(r2_ref[h, ch], BF16) < _dup_rows(n_ref[h, ch, a:a + 1, :], PEER_NKEYS)
                    w = w + (jnp.where(keep, pltpu.bitcast(e2_ref[h, ch], BF16), zero)
                             * _dup_rows(e1_ref[h, ch, a:a + 1, :], PEER_NKEYS))
                rows = slice(a2 * PEER_NKEYS, (a2 + 1) * PEER_NKEYS)
                cols = slice(ch * LANES, (ch + 1) * LANES)
                abs_[p % 2][rows, cols] = jax.nn.gelu(acts[p % 2][rows, cols].astype(BF16)) * w

    act_mm(0)
    for p in range(nb):
        if p + 1 < nb:
            act_mm(p + 1)
        if p >= 1:
            out_mm(p - 1)
        gate(p)
    out_mm(nb - 1)

    @pl.when(j == pl.num_programs(1) - 1)
    def _():
        o_ref[...] = x_ref[...] + g2_ref[0] * acc_ref[...]


def _peer_dense(hbt, u_blk, v_blk, e1, nrow, e2, rank2, x, g2, seq, tm=512, na=16):
    t, d = x.shape
    nblk, rows, _ = u_blk.shape
    nb = na // PEER_PB
    nch = tm // LANES
    tpb = seq // tm
    hh = PEER_HEADS
    small = pl.BlockSpec((hh, nch, na, LANES), lambda i, j: (0, i, j, 0))
    big = pl.BlockSpec((hh, nch, PEER_NKEYS // 2, LANES), lambda i, j: (0, i, 0, 0))
    return pl.pallas_call(
        _peer_dense_kernel,
        grid=(t // tm, nblk // nb),
        in_specs=[pl.BlockSpec((d, tm), lambda i, j: (0, i)),
                  pl.BlockSpec((nb, rows, d), lambda i, j: (j, 0, 0)),
                  pl.BlockSpec((nb, rows, d), lambda i, j: (j, 0, 0)),
                  small, small, big, big,
                  pl.BlockSpec((tm, d), lambda i, j: (i, 0)),
                  pl.BlockSpec((1, 1, d), lambda i, j: (i // tpb, 0, 0))],
        out_specs=pl.BlockSpec((tm, d), lambda i, j: (i, 0)),
        out_shape=jax.ShapeDtypeStruct((t, d), F32),
        scratch_shapes=[pltpu.VMEM((tm, d), F32), pltpu.VMEM((rows, tm), F32), pltpu.VMEM((rows, tm), F32),
                        pltpu.VMEM((rows, tm), BF16), pltpu.VMEM((rows, tm), BF16)],
        compiler_params=_cparams(("arbitrary", "arbitrary")),
        name="peer_dense",
    )(hbt, u_blk, v_blk, e1, nrow, e2, rank2, x, g2)


def _final_norm_kernel(x_ref, g_ref, o_ref):
    x = x_ref[...]
    inv = lax.rsqrt(jnp.mean(x * x, axis=-1, keepdims=True) + EPS)
    o_ref[...] = (x * inv) * g_ref[...]


def _final_norm(x, gain, tm=1024):
    t, d = x.shape
    return pl.pallas_call(
        _final_norm_kernel,
        grid=(t // tm,),
        in_specs=[pl.BlockSpec((tm, d), lambda i: (i, 0)), pl.BlockSpec((1, d), lambda i: (0, 0))],
        out_specs=pl.BlockSpec((tm, d), lambda i: (i, 0)),
        out_shape=jax.ShapeDtypeStruct((t, d), F32),
        compiler_params=_cparams(("arbitrary",)),
        name="final_norm",
    )(x, gain.reshape(1, d))


def _peer_layer(x, gain, sc, sh, g2, w_q, keys, u_tab, v_tab, seq):
    wq1, wq2 = _split2(w_q)
    kf = keys.reshape(-1, PEER_NKEYS, keys.shape[-1])
    k1, k2 = _split2(kf)
    hb, scores_t = _peer_q(x, gain, sc, sh, wq1, wq2, k1, k2, seq)
    e1, nrow, e2, rank2 = _peer_topk(scores_t)
    rows = PEER_PB * PEER_NKEYS
    ne, d = u_tab.shape
    u_blk = u_tab.astype(BF16).reshape(ne // rows, rows, d)
    v_blk = v_tab.astype(BF16).reshape(ne // rows, rows, d)
    return _peer_dense(hb, u_blk, v_blk, e1, nrow, e2, rank2, x, g2, seq)


def kernel(x, c, w_ada, b_ada, norm_g, hg_w_in, hg_lb, hg_norm_g, hg_w_out, lru_w_in, lru_conv_w,
           lru_conv_b, lru_w_a, lru_b_a, lru_w_x, lru_b_x, lru_lam, lru_w_out, peer_w_q, peer_keys,
           peer_u, peer_v, final_g):
    batch, seq, d = x.shape
    depth = w_ada.shape[0]
    n_mixers = 2
    mod = _ada_mod(c, w_ada, b_ada)
    xt = x.reshape(batch * seq, d)
    for i in range(depth):
        parts = [mod[i, :, k * d:(k + 1) * d].reshape(batch, 1, d) for k in range(6)]
        sh1, sc1, g1, sh2, sc2, g2 = parts
        j = i // n_mixers
        if i % n_mixers == 0:
            proj = _norm_mod_matmul(xt, norm_g[i, 0], sc1, sh1, hg_w_in[j].astype(BF16), seq,
                                    head_major=True)
            o_fw, o_bw = _hgrn_scan(proj, hg_lb, j, batch, seq)
            xt = _hgrn_out(o_fw, o_bw, proj, hg_norm_g[j], hg_w_out[j].astype(BF16), xt, g1, seq)
        else:
            proj = _norm_mod_matmul(xt, norm_g[i, 0], sc1, sh1, lru_w_in[j].astype(BF16), seq,
                                    head_major=False)
            hs = _lru_scan(proj, lru_conv_w[j], lru_conv_b[j], lru_w_a[j], lru_b_a[j], lru_w_x[j],
                           lru_b_x[j], lru_lam[j], batch, seq)
            xt = _lru_out(hs, proj, lru_w_out[j].astype(BF16), xt, g1, seq)
        xt = _peer_layer(xt, norm_g[i, 1], sc2, sh2, g2, peer_w_q[i], peer_keys[i], peer_u[i],
                         peer_v[i], seq)
    return _final_norm(xt, final_g).reshape(batch, seq, d)
```

```python
import functools

import numpy as np
import jax
import jax.numpy as jnp
from jax import lax
from jax.experimental import pallas as pl
from jax.experimental.pallas import tpu as pltpu

F32 = jnp.float32
BF16 = jnp.bfloat16

SUBLANES = 8
LANES = 128
VMEM_LIMIT = 56 * 1024 * 1024

EPS = 1e-6
HG_HEADS = 8
HG_CHUNK = 128
LRU_BLOCKS = 4
LRU_C = 8.0
CONV_W = 4
CONV_LEFT = 2
PEER_HEADS = 8
PEER_NKEYS = 128
PEER_TOPK = 16
HG_LEVELS = 7
NEG_INF = float("-inf")


def _cparams(sem):
    return pltpu.CompilerParams(dimension_semantics=sem, vmem_limit_bytes=VMEM_LIMIT)


def _split3(x):
    x1 = x.astype(BF16)
    r1 = x - x1.astype(F32)
    x2 = r1.astype(BF16)
    x3 = (r1 - x2.astype(F32)).astype(BF16)
    return x1, x2, x3


def _split2(x):
    x1 = x.astype(BF16)
    x2 = (x - x1.astype(F32)).astype(BF16)
    return x1, x2


def _dot(a, b):
    return jnp.dot(a, b, preferred_element_type=F32)


def _dot_nt(a, b):
    return lax.dot_general(a, b, (((1,), (1,)), ((), ())), preferred_element_type=F32)


def _dot_tn(a, b):
    return lax.dot_general(a, b, (((0,), (0,)), ((), ())), preferred_element_type=F32)


def _norm_mod(x, gain, sc, sh):
    inv = lax.rsqrt(jnp.mean(x * x, axis=-1, keepdims=True) + EPS)
    return (x * inv) * gain * (1.0 + sc) + sh


def _ada_kernel(c_ref, w_ref, b_ref, o_ref):
    c = c_ref[...]
    cond = c * jax.nn.sigmoid(c)
    c1, c2, c3 = _split3(cond)
    w1, w2, w3 = _split3(w_ref[0])
    acc = _dot(c1, w1) + _dot(c1, w2) + _dot(c2, w1)
    acc = acc + _dot(c1, w3) + _dot(c2, w2) + _dot(c3, w1)
    o_ref[0] = acc + b_ref[0]


def _ada_mod(c, w_ada, b_ada):
    depth, d, n = w_ada.shape
    b = c.shape[0]
    cp = jnp.zeros((SUBLANES, d), F32).at[:b].set(c)
    tn = 1536
    out = pl.pallas_call(
        _ada_kernel,
        grid=(depth, n // tn),
        in_specs=[pl.BlockSpec((SUBLANES, d), lambda i, j: (0, 0)),
                  pl.BlockSpec((1, d, tn), lambda i, j: (i, 0, j)),
                  pl.BlockSpec((1, 1, tn), lambda i, j: (i, 0, j))],
        out_specs=pl.BlockSpec((1, SUBLANES, tn), lambda i, j: (i, 0, j)),
        out_shape=jax.ShapeDtypeStruct((depth, SUBLANES, n), F32),
        compiler_params=_cparams(("arbitrary", "arbitrary")),
        name="ada_mod",
    )(cp, w_ada, b_ada.reshape(depth, 1, n))
    return out[:, :b]


def _nmm_kernel(x_ref, g_ref, sc_ref, sh_ref, w_ref, o_ref, h_scr, *, head_major):
    @pl.when(pl.program_id(1) == 0)
    def _():
        h_scr[...] = _norm_mod(x_ref[...], g_ref[...], sc_ref[0], sh_ref[0]).astype(BF16)

    res = _dot(h_scr[...], w_ref[...])
    if head_major:
        for k in range(o_ref.shape[0]):
            o_ref[k] = res[:, k * LANES:(k + 1) * LANES]
    else:
        o_ref[...] = res


def _norm_mod_matmul(x, gain, sc, sh, w_bf16, seq, *, head_major, tm=1024, tn=1024):
    t, d = x.shape
    n = w_bf16.shape[1]
    tpb = seq // tm
    if head_major:
        out_shape = jax.ShapeDtypeStruct((n // LANES, t, LANES), F32)
        out_spec = pl.BlockSpec((tn // LANES, tm, LANES), lambda i, j: (j, i, 0))
    else:
        out_shape = jax.ShapeDtypeStruct((t, n), F32)
        out_spec = pl.BlockSpec((tm, tn), lambda i, j: (i, j))
    return pl.pallas_call(
        functools.partial(_nmm_kernel, head_major=head_major),
        grid=(t // tm, n // tn),
        in_specs=[pl.BlockSpec((tm, d), lambda i, j: (i, 0)),
                  pl.BlockSpec((1, d), lambda i, j: (0, 0)),
                  pl.BlockSpec((1, 1, d), lambda i, j: (i // tpb, 0, 0)),
                  pl.BlockSpec((1, 1, d), lambda i, j: (i // tpb, 0, 0)),
                  pl.BlockSpec((d, tn), lambda i, j: (0, j))],
        out_specs=out_spec,
        out_shape=out_shape,
        scratch_shapes=[pltpu.VMEM((tm, d), BF16)],
        compiler_params=_cparams(("arbitrary", "arbitrary")),
        name="norm_mod_matmul",
    )(x, gain.reshape(1, d), sc, sh, w_bf16)


def _hgrn_sum_mats(reverse):
    c = HG_CHUNK
    mats = []
    for lvl in range(1, HG_LEVELS + 1):
        m = 1 << lvl
        half = m >> 1
        a = np.zeros((c, c), np.float32)
        for t in range(c):
            mid = (t // m) * m + half
            if not reverse:
                if t >= mid:
                    a[t, mid:t + 1] = 1.0
                else:
                    a[t, t + 1:mid] = 1.0
            else:
                if t < mid:
                    a[t, t:mid] = 1.0
                else:
                    a[t, mid:t] = 1.0
        mats.append(a)
    ones = np.ones((c, c), np.float32)
    if not reverse:
        mats += [np.tril(ones), np.triu(ones, 1)]
    else:
        mats += [np.triu(ones), np.tril(ones, -1)]
    return np.concatenate(mats, axis=0)


def _hgrn_chain(q, flog, v, lb, a_ref, st_ref, d, h, reverse):
    c = HG_CHUNK
    f = lb + (1.0 - lb) * jax.nn.sigmoid(flog)
    g = jnp.log(f)
    k = 1.0 - f
    g1, g2 = _split2(g)
    e2 = _dot(a_ref[...], jnp.concatenate([g1, g2], axis=1))
    e = e2[:, :LANES] + e2[:, LANES:]
    xdec = jnp.exp(e)
    rows = lax.broadcasted_iota(jnp.int32, (c, c), 0)
    cols = lax.broadcasted_iota(jnp.int32, (c, c), 1)
    attn = jnp.where(rows == cols, jnp.sum(q * k, axis=-1, keepdims=True), 0.0)
    for lvl in range(1, HG_LEVELS + 1):
        xl = xdec[(lvl - 1) * c:lvl * c]
        upper = ((rows >> (lvl - 1)) & 1) == 1
        qmask = jnp.logical_not(upper) if reverse else upper
        ql = jnp.where(qmask, q * xl, 0.0).astype(BF16)
        kl = jnp.where(qmask, 0.0, k * xl).astype(BF16)
        p = _dot_nt(ql, kl)
        attn = attn + jnp.where((rows >> lvl) == (cols >> lvl), p, 0.0)
    eq = e[HG_LEVELS * c:(HG_LEVELS + 1) * c]
    qd = (q * xdec[HG_LEVELS * c:(HG_LEVELS + 1) * c]).astype(BF16)
    kd = (k * xdec[(HG_LEVELS + 1) * c:(HG_LEVELS + 2) * c]).astype(BF16)
    tot = eq[0:1] if reverse else eq[c - 1:c]
    st = st_ref[d, h]
    vb = v.astype(BF16)
    o = _dot(attn.astype(BF16), vb) + _dot_nt(qd, st.astype(BF16))
    st_ref[d, h] = st * jnp.exp(tot) + _dot_tn(vb, kd)
    return o


def _hgrn_scan_kernel(lb_ref, afw_ref, abw_ref, qf_ref, ff_ref, vf_ref, qb_ref, fb_ref, vb_ref,
                      of_ref, ob_ref, st_ref, lbs_ref, *, layer):
    @pl.when(pl.program_id(1) == 0)
    def _():
        st_ref[...] = jnp.zeros_like(st_ref)
        lb = lb_ref[...]
        ex = jnp.exp(lb - jnp.max(lb, axis=0, keepdims=True))
        p = ex / jnp.sum(ex, axis=0, keepdims=True)
        acc = p[0]
        for jj in range(1, layer + 1):
            acc = acc + p[jj]
        lbs_ref[...] = acc - p[0]

    def body(h, carry):
        lb = lbs_ref[pl.ds(h, 1), :]
        of_ref[h] = _hgrn_chain(qf_ref[h], ff_ref[h], vf_ref[h], lb, afw_ref, st_ref, 0, h, False)
        ob_ref[h] = _hgrn_chain(qb_ref[h], fb_ref[h], vb_ref[h], lb, abw_ref, st_ref, 1, h, True)
        return carry

    lax.fori_loop(0, HG_HEADS, body, 0, unroll=8)


def _hgrn_scan(proj_hm, hg_lb, layer, batch, seq):
    hh = HG_HEADS
    t = proj_hm.shape[1]
    nc = seq // HG_CHUNK
    c = HG_CHUNK
    na = hg_lb.shape[0]
    afw = jnp.asarray(_hgrn_sum_mats(False), BF16)
    abw = jnp.asarray(_hgrn_sum_mats(True), BF16)
    nrow = afw.shape[0]

    def fwd(part):
        return pl.BlockSpec((hh, c, LANES), lambda b, cc: (part, b * nc + cc, 0))

    def bwd(part):
        return pl.BlockSpec((hh, c, LANES), lambda b, cc: (part, b * nc + nc - 1 - cc, 0))

    out_sd = jax.ShapeDtypeStruct((hh, t, LANES), F32)
    return pl.pallas_call(
        functools.partial(_hgrn_scan_kernel, layer=layer),
        grid=(batch, nc),
        in_specs=[pl.BlockSpec((na, hh, LANES), lambda b, cc: (0, 0, 0)),
                  pl.BlockSpec((nrow, c), lambda b, cc: (0, 0)),
                  pl.BlockSpec((nrow, c), lambda b, cc: (0, 0)),
                  fwd(0), fwd(1), fwd(3), bwd(0), bwd(2), bwd(3)],
        out_specs=[pl.BlockSpec((hh, c, LANES), lambda b, cc: (0, b * nc + cc, 0)),
                   pl.BlockSpec((hh, c, LANES), lambda b, cc: (0, b * nc + nc - 1 - cc, 0))],
        out_shape=[out_sd, out_sd],
        scratch_shapes=[pltpu.VMEM((2, hh, LANES, LANES), F32), pltpu.VMEM((hh, LANES), F32)],
        compiler_params=_cparams(("arbitrary", "arbitrary")),
        name="hgrn_scan",
    )(hg_lb.reshape(na, hh, LANES), afw, abw, proj_hm, proj_hm, proj_hm, proj_hm, proj_hm, proj_hm)


def _hgrn_out_kernel(of_ref, ob_ref, gg_ref, ng_ref, w_ref, x_ref, g1_ref, o_ref):
    parts = []
    for h in range(HG_HEADS):
        o = of_ref[h] + ob_ref[h]
        inv = lax.rsqrt(jnp.mean(o * o, axis=-1, keepdims=True) + EPS)
        gg = gg_ref[h]
        parts.append(((o * inv) * ng_ref[pl.ds(h, 1), :] * (gg * jax.nn.sigmoid(gg))).astype(BF16))
    y = _dot(jnp.concatenate(parts, axis=-1), w_ref[...])
    o_ref[...] = x_ref[...] + g1_ref[0] * y


def _hgrn_out(o_fw, o_bw, proj_hm, norm_g, w_out_bf16, x, g1, seq, tm=512):
    hh = HG_HEADS
    t, d = x.shape
    tpb = seq // tm
    return pl.pallas_call(
        _hgrn_out_kernel,
        grid=(t // tm,),
        in_specs=[pl.BlockSpec((hh, tm, LANES), lambda i: (0, i, 0)),
                  pl.BlockSpec((hh, tm, LANES), lambda i: (0, i, 0)),
                  pl.BlockSpec((hh, tm, LANES), lambda i: (4, i, 0)),
                  pl.BlockSpec((hh, LANES), lambda i: (0, 0)),
                  pl.BlockSpec((d, d), lambda i: (0, 0)),
                  pl.BlockSpec((tm, d), lambda i: (i, 0)),
                  pl.BlockSpec((1, 1, d), lambda i: (i // tpb, 0, 0))],
        out_specs=pl.BlockSpec((tm, d), lambda i: (i, 0)),
        out_shape=jax.ShapeDtypeStruct((t, d), F32),
        compiler_params=_cparams(("arbitrary",)),
        name="hgrn_out",
    )(o_fw, o_bw, proj_hm, norm_g.reshape(hh, LANES), w_out_bf16, x, g1)


def _lru_group_scan(a, x, carry, rows, reverse):
    for s in (1, 2, 4):
        if not reverse:
            keep = rows >= s
            a_sh = jnp.where(keep, pltpu.roll(a, s, 0), 1.0)
            x_sh = jnp.where(keep, pltpu.roll(x, s, 0), 0.0)
        else:
            keep = rows < SUBLANES - s
            a_sh = jnp.where(keep, pltpu.roll(a, SUBLANES - s, 0), 1.0)
            x_sh = jnp.where(keep, pltpu.roll(x, SUBLANES - s, 0), 0.0)
        x = x + a * x_sh
        a = a * a_sh
    hs = x + a * carry
    new_carry = hs[0:1] if reverse else hs[SUBLANES - 1:SUBLANES]
    return hs, new_carry


def _lru_kernel(xc_ref, xp_ref, xn_ref, cw_ref, cb_ref, wa_ref, ba_ref, wx_ref, bx_ref, lam_ref,
                o_ref, a_scr, b_scr, carry_scr, *, tm, nchunk):
    d = pl.program_id(0)
    cc = pl.program_id(2)
    chunk = jnp.where(d == 0, cc, nchunk - 1 - cc)

    @pl.when(cc == 0)
    def _():
        carry_scr[...] = jnp.zeros_like(carry_scr)

    xcur = xc_ref[...]
    prev = jnp.where(chunk == 0, 0.0, xp_ref[...])
    nxt = jnp.where(chunk == nchunk - 1, 0.0, xn_ref[...])
    ext = jnp.concatenate([prev, xcur, nxt], axis=0)
    cw = cw_ref[...]
    xc = cb_ref[...]
    for j in range(CONV_W):
        off = SUBLANES - CONV_LEFT + j
        xc = xc + ext[off:off + tm] * cw[j:j + 1]

    xcb = xc.astype(BF16)
    bw = xcb.shape[1] // LRU_BLOCKS
    ra, rx = [], []
    for n in range(LRU_BLOCKS):
        blk = xcb[:, n * bw:(n + 1) * bw]
        ra.append(_dot(blk, wa_ref[0, n]))
        rx.append(_dot(blk, wx_ref[0, n]))
    r = jax.nn.sigmoid(jnp.concatenate(ra, axis=-1) + ba_ref[0])
    ig = jax.nn.sigmoid(jnp.concatenate(rx, axis=-1) + bx_ref[0])
    lam = lam_ref[0]
    softplus_neg = jnp.maximum(-lam, 0.0) + jnp.log(1.0 + jnp.exp(-jnp.abs(lam)))
    log_a = -LRU_C * r * softplus_neg
    a = jnp.exp(log_a)
    a_scr[...] = a
    b_scr[...] = jnp.sqrt(-jnp.tanh(log_a) * (a * a + 1.0)) * ig * xc

    ngroups = tm // SUBLANES
    rows = lax.broadcasted_iota(jnp.int32, (SUBLANES, xc.shape[1]), 0)

    def run(reverse):
        def body(i, carry):
            gi = (ngroups - 1 - i) if reverse else i
            sl = pl.ds(pl.multiple_of(gi * SUBLANES, SUBLANES), SUBLANES)
            hs, carry = _lru_group_scan(a_scr[sl, :], b_scr[sl, :], carry, rows, reverse)
            o_ref[0, sl, :] = hs
            return carry
        carry_scr[...] = lax.fori_loop(0, ngroups, body, carry_scr[...])

    @pl.when(d == 0)
    def _():
        run(False)

    @pl.when(d == 1)
    def _():
        run(True)


def _lru_scan(proj, conv_w, conv_b, w_a, b_a, w_x, b_x, lam, batch, seq, tm=256):
    t = proj.shape[0]
    w = proj.shape[1] // 2
    nchunk = seq // tm
    hb = tm // SUBLANES
    nhalo = t // SUBLANES

    def pos(d, b, cc):
        return b * nchunk + jnp.where(d == 0, cc, nchunk - 1 - cc)

    return pl.pallas_call(
        functools.partial(_lru_kernel, tm=tm, nchunk=nchunk),
        grid=(2, batch, nchunk),
        in_specs=[pl.BlockSpec((tm, w), lambda d, b, cc: (pos(d, b, cc), 0)),
                  pl.BlockSpec((SUBLANES, w), lambda d, b, cc: (jnp.maximum(pos(d, b, cc) * hb - 1, 0), 0)),
                  pl.BlockSpec((SUBLANES, w),
                               lambda d, b, cc: (jnp.minimum((pos(d, b, cc) + 1) * hb, nhalo - 1), 0)),
                  pl.BlockSpec((CONV_W, w), lambda d, b, cc: (0, 0)),
                  pl.BlockSpec((1, w), lambda d, b, cc: (0, 0)),
                  pl.BlockSpec((1, LRU_BLOCKS, w // LRU_BLOCKS, w // LRU_BLOCKS), lambda d, b, cc: (d, 0, 0, 0)),
                  pl.BlockSpec((1, 1, w), lambda d, b, cc: (d, 0, 0)),
                  pl.BlockSpec((1, LRU_BLOCKS, w // LRU_BLOCKS, w // LRU_BLOCKS), lambda d, b, cc: (d, 0, 0, 0)),
                  pl.BlockSpec((1, 1, w), lambda d, b, cc: (d, 0, 0)),
                  pl.BlockSpec((1, 1, w), lambda d, b, cc: (d, 0, 0))],
        out_specs=pl.BlockSpec((1, tm, w), lambda d, b, cc: (d, pos(d, b, cc), 0)),
        out_shape=jax.ShapeDtypeStruct((2, t, w), F32),
        scratch_shapes=[pltpu.VMEM((tm, w), F32), pltpu.VMEM((tm, w), F32), pltpu.VMEM((1, w), F32)],
        compiler_params=_cparams(("arbitrary", "arbitrary", "arbitrary")),
        name="lru_scan",
    )(proj, proj, proj, conv_w, conv_b.reshape(1, w), w_a.astype(BF16), b_a.reshape(2, 1, w),
      w_x.astype(BF16), b_x.reshape(2, 1, w), lam.reshape(2, 1, w))


def _lru_out_kernel(hs_ref, yb_ref, w_ref, x_ref, g1_ref, o_ref):
    u = (hs_ref[0] + hs_ref[1]) * jax.nn.gelu(yb_ref[...])
    o_ref[...] = x_ref[...] + g1_ref[0] * _dot(u.astype(BF16), w_ref[...])


def _lru_out(hs, proj, w_out_bf16, x, g1, seq, tm=512):
    t, d = x.shape
    w = hs.shape[2]
    tpb = seq // tm
    return pl.pallas_call(
        _lru_out_kernel,
        grid=(t // tm,),
        in_specs=[pl.BlockSpec((2, tm, w), lambda i: (0, i, 0)),
                  pl.BlockSpec((tm, w), lambda i: (i, 1)),
                  pl.BlockSpec((w, d), lambda i: (0, 0)),
                  pl.BlockSpec((tm, d), lambda i: (i, 0)),
                  pl.BlockSpec((1, 1, d), lambda i: (i // tpb, 0, 0))],
        out_specs=pl.BlockSpec((tm, d), lambda i: (i, 0)),
        out_shape=jax.ShapeDtypeStruct((t, d), F32),
        compiler_params=_cparams(("arbitrary",)),
        name="lru_out",
    )(hs, proj, w_out_bf16, x, g1)


def _peer_q_kernel(x_ref, g_ref, sc_ref, sh_ref, w1_ref, w2_ref, k1_ref, k2_ref, hb_ref, sc_out_ref):
    h = _norm_mod(x_ref[...], g_ref[...], sc_ref[0], sh_ref[0])
    h1, h2 = _split2(h)
    hb_ref[...] = h.T.astype(BF16)
    q = _dot(h1, w1_ref[...]) + _dot(h1, w2_ref[...]) + _dot(h2, w1_ref[...])
    nhp = k1_ref.shape[0]
    nch = sc_out_ref.shape[1]
    for hp in range(nhp):
        qa, qb = _split2(q[:, hp * LANES:(hp + 1) * LANES])
        s = _dot_nt(k1_ref[hp], qa) + _dot_nt(k1_ref[hp], qb) + _dot_nt(k2_ref[hp], qa)
        for ch in range(nch):
            sc_out_ref[hp, ch] = s[:, ch * LANES:(ch + 1) * LANES]


def _peer_q(x, gain, sc, sh, wq1, wq2, keys1, keys2, seq, tm=256):
    t, d = x.shape
    nq = wq1.shape[1]
    nhp = keys1.shape[0]
    tpb = seq // tm
    nch = tm // LANES
    return pl.pallas_call(
        _peer_q_kernel,
        grid=(t // tm,),
        in_specs=[pl.BlockSpec((tm, d), lambda i: (i, 0)),
                  pl.BlockSpec((1, d), lambda i: (0, 0)),
                  pl.BlockSpec((1, 1, d), lambda i: (i // tpb, 0, 0)),
                  pl.BlockSpec((1, 1, d), lambda i: (i // tpb, 0, 0)),
                  pl.BlockSpec((d, nq), lambda i: (0, 0)),
                  pl.BlockSpec((d, nq), lambda i: (0, 0)),
                  pl.BlockSpec((nhp, PEER_NKEYS, LANES), lambda i: (0, 0, 0)),
                  pl.BlockSpec((nhp, PEER_NKEYS, LANES), lambda i: (0, 0, 0))],
        out_specs=[pl.BlockSpec((d, tm), lambda i: (0, i)),
                   pl.BlockSpec((nhp, nch, PEER_NKEYS, LANES), lambda i: (0, i, 0, 0))],
        out_shape=[jax.ShapeDtypeStruct((d, t), BF16),
                   jax.ShapeDtypeStruct((nhp, t // LANES, PEER_NKEYS, LANES), F32)],
        compiler_params=_cparams(("arbitrary",)),
        name="peer_q",
    )(x, gain.reshape(1, d), sc, sh, wq1, wq2, keys1, keys2)


def _top16(s, rows, exact):
    work = s
    rank = jnp.full(s.shape, PEER_TOPK, jnp.int32)
    vals = []
    for kk in range(PEER_TOPK):
        m = jnp.max(work, axis=0, keepdims=True)
        if exact:
            idx = jnp.min(jnp.where(work == m, rows, PEER_NKEYS), axis=0, keepdims=True)
            sel = rows == idx
        else:
            sel = work == m
        rank = jnp.where(sel, kk, rank)
        work = jnp.where(sel, NEG_INF, work)
        vals.append(m)
    return rank, jnp.concatenate(vals, axis=0)


def _col_count(mask):
    return jnp.sum(jnp.where(mask, 1.0, 0.0), axis=0, keepdims=True)


def _any_lane_differs(count, want):
    return jnp.max(jnp.where(count == want, 0.0, 1.0)) > 0.5


def _cand16(cand, flat, r16, mtop, exact):
    cnt = jnp.zeros((PEER_TOPK, LANES), F32)
    z = jnp.zeros((1, LANES), F32)
    for _ in range(PEER_TOPK):
        m = jnp.max(cand, axis=0, keepdims=True)
        if exact:
            idx = jnp.min(jnp.where(cand == m, flat, PEER_TOPK * PEER_TOPK), axis=0, keepdims=True)
            cand = jnp.where(flat == idx, NEG_INF, cand)
            cnt = cnt + jnp.where(r16 == (idx >> 4), 1.0, 0.0)
        else:
            cand = jnp.where(cand == m, NEG_INF, cand)
        z = z + jnp.exp(m - mtop)
    if not exact:
        gone = cand == NEG_INF
        cnt = jnp.concatenate(
            [_col_count(gone[0:PEER_TOPK])]
            + [_col_count(gone[PEER_TOPK + SUBLANES * (k1 - 1):PEER_TOPK + SUBLANES * k1]) for k1 in range(1, 8)]
            + [jnp.where(gone[PEER_TOPK + 7 * SUBLANES:], 1.0, 0.0)], axis=0)
    return cnt, z


def _bf16_pair(x):
    u = pltpu.bitcast(x.astype(BF16).astype(F32), jnp.uint32)
    return u | (u >> 16)


def _peer_topk_kernel(s_ref, e1_ref, n_ref, e2_ref, r2_ref):
    nch = s_ref.shape[1]
    rows = lax.broadcasted_iota(jnp.int32, (PEER_NKEYS, LANES), 0)
    r16 = lax.broadcasted_iota(jnp.int32, (PEER_TOPK, LANES), 0)
    r8 = lax.broadcasted_iota(jnp.int32, (SUBLANES, LANES), 0)
    flat = jnp.concatenate([r16] + [r8 + PEER_TOPK * k1 for k1 in range(1, 8)]
                           + [(r8 + 8) * PEER_TOPK], axis=0)
    want = float(PEER_TOPK)

    def body(ch, carry):
        s1 = s_ref[0, ch]
        s2 = s_ref[1, ch]
        rank1, v1 = _top16(s1, rows, False)
        rank2, v2 = _top16(s2, rows, False)
        tied = jnp.logical_or(_any_lane_differs(_col_count(rank1 < PEER_TOPK), want),
                              _any_lane_differs(_col_count(rank2 < PEER_TOPK), want))
        rank1, v1, rank2, v2 = lax.cond(
            tied, lambda: _top16(s1, rows, True) + _top16(s2, rows, True), lambda: (rank1, v1, rank2, v2))
        cand = jnp.concatenate([v1[0:1] + v2] + [v1[k1:k1 + 1] + v2[0:8] for k1 in range(1, 8)]
                               + [v1[8:16] + v2[0:1]], axis=0)
        mtop = v1[0:1] + v2[0:1]
        cnt, z = _cand16(cand, flat, r16, mtop, False)
        tied = _any_lane_differs(jnp.sum(cnt, axis=0, keepdims=True), want)
        cnt, z = lax.cond(tied, lambda: _cand16(cand, flat, r16, mtop, True), lambda: (cnt, z))
        nrow = jnp.zeros((PEER_NKEYS, LANES), F32)
        for kk in range(PEER_TOPK):
            nrow = jnp.where(rank1 == kk, cnt[kk:kk + 1], nrow)
        e1 = jnp.where(rank1 < PEER_TOPK, jnp.exp(s1 - v1[0:1]) / z, 0.0)
        e1_ref[0, ch] = _bf16_pair(e1)
        n_ref[0, ch] = _bf16_pair(nrow)
        e2 = jnp.where(rank2 < PEER_TOPK, jnp.exp(s2 - v2[0:1]), 0.0)
        e2_ref[0, ch] = pltpu.bitcast(e2.astype(BF16), jnp.uint32)
        r2_ref[0, ch] = pltpu.bitcast(rank2.astype(F32).astype(BF16), jnp.uint32)
        return carry

    lax.fori_loop(0, nch, body, 0)


def _peer_topk(scores_t, nch=4):
    nhp, ntc, nk, _ = scores_t.shape
    hh = nhp // 2
    spec = pl.BlockSpec((1, nch, nk, LANES), lambda i, h: (h, i, 0, 0))
    spec_p = pl.BlockSpec((1, nch, nk // 2, LANES), lambda i, h: (h, i, 0, 0))
    sd_u = jax.ShapeDtypeStruct((hh, ntc, nk, LANES), jnp.uint32)
    sd_b = jax.ShapeDtypeStruct((hh, ntc, nk // 2, LANES), jnp.uint32)
    return pl.pallas_call(
        _peer_topk_kernel,
        grid=(ntc // nch, hh),
        in_specs=[pl.BlockSpec((2, nch, nk, LANES), lambda i, h: (h, i, 0, 0))],
        out_specs=[spec, spec, spec_p, spec_p],
        out_shape=[sd_u, sd_u, sd_b, sd_b],
        compiler_params=_cparams(("arbitrary", "arbitrary")),
        name="peer_topk",
    )(scores_t)


def _dup_rows(row_u32, nrows):
    return pltpu.bitcast(jnp.broadcast_to(row_u32, (nrows // 2, row_u32.shape[1])), BF16)


PEER_PB = 2


def _peer_dense_kernel(hbt_ref, u_ref, v_ref, e1_ref, n_ref, e2_ref, r2_ref, x_ref, g2_ref,
                       o_ref, acc_ref, act0, act1, ab0, ab1):
    j = pl.program_id(1)

    @pl.when(j == 0)
    def _():
        acc_ref[...] = jnp.zeros_like(acc_ref)

    nch = e2_ref.shape[1]
    nb = u_ref.shape[0]
    acts = (act0, act1)
    abs_ = (ab0, ab1)
    zero = jnp.zeros((), BF16)

    def act_mm(p):
        acts[p % 2][...] = _dot(u_ref[p], hbt_ref[...])

    def out_mm(p):
        acc_ref[...] += _dot_tn(abs_[p % 2][...], v_ref[p])

    def gate(p):
        for ch in range(nch):
            for a2 in range(PEER_PB):
                a = p * PEER_PB + a2
                w = jnp.zeros((PEER_NKEYS, LANES), BF16)
                for h in range(PEER_HEADS):
                    keep = pltpu.bitcast(r2_ref[h, ch], BF16) < _dup_rows(n_ref[h, ch, a:a + 1, :], PEER_NKEYS)
                    w = w + (jnp.where(keep, pltpu.bitcast(e2_ref[h, ch], BF16), zero)
                             * _dup_rows(e1_ref[h, ch, a:a + 1, :], PEER_NKEYS))
                rows = slice(a2 * PEER_NKEYS, (a2 + 1) * PEER_NKEYS)
                cols = slice(ch * LANES, (ch + 1) * LANES)
                abs_[p % 2][rows, cols] = jax.nn.gelu(acts[p % 2][rows, cols].astype(BF16)) * w

    act_mm(0)
    for p in range(nb):
        if p + 1 < nb:
            act_mm(p + 1)
        if p >= 1:
            out_mm(p - 1)
        gate(p)
    out_mm(nb - 1)

    @pl.when(j == pl.num_programs(1) - 1)
    def _():
        o_ref[...] = x_ref[...] + g2_ref[0] * acc_ref[...]


def _peer_dense(hbt, u_blk, v_blk, e1, nrow, e2, rank2, x, g2, seq, tm=1024, na=8):
    t, d = x.shape
    nblk, rows, _ = u_blk.shape
    nb = na // PEER_PB
    nch = tm // LANES
    tpb = seq // tm
    hh = PEER_HEADS
    small = pl.BlockSpec((hh, nch, na, LANES), lambda i, j: (0, i, j, 0))
    big = pl.BlockSpec((hh, nch, PEER_NKEYS // 2, LANES), lambda i, j: (0, i, 0, 0))
    return pl.pallas_call(
        _peer_dense_kernel,
        grid=(t // tm, nblk // nb),
        in_specs=[pl.BlockSpec((d, tm), lambda i, j: (0, i)),
                  pl.BlockSpec((nb, rows, d), lambda i, j: (j, 0, 0)),
                  pl.BlockSpec((nb, rows, d), lambda i, j: (j, 0, 0)),
                  small, small, big, big,
                  pl.BlockSpec((tm, d), lambda i, j: (i, 0)),
                  pl.BlockSpec((1, 1, d), lambda i, j: (i // tpb, 0, 0))],
        out_specs=pl.BlockSpec((tm, d), lambda i, j: (i, 0)),
        out_shape=jax.ShapeDtypeStruct((t, d), F32),
        scratch_shapes=[pltpu.VMEM((tm, d), F32), pltpu.VMEM((rows, tm), F32), pltpu.VMEM((rows, tm), F32),
                        pltpu.VMEM((rows, tm), BF16), pltpu.VMEM((rows, tm), BF16)],
        compiler_params=_cparams(("arbitrary", "arbitrary")),
        name="peer_dense",
    )(hbt, u_blk, v_blk, e1, nrow, e2, rank2, x, g2)


def _final_norm_kernel(x_ref, g_ref, o_ref):
    x = x_ref[...]
    inv = lax.rsqrt(jnp.mean(x * x, axis=-1, keepdims=True) + EPS)
    o_ref[...] = (x * inv) * g_ref[...]


def _final_norm(x, gain, tm=1024):
    t, d = x.shape
    return pl.pallas_call(
        _final_norm_kernel,
        grid=(t // tm,),
        in_specs=[pl.BlockSpec((tm, d), lambda i: (i, 0)), pl.BlockSpec((1, d), lambda i: (0, 0))],
        out_specs=pl.BlockSpec((tm, d), lambda i: (i, 0)),
        out_shape=jax.ShapeDtypeStruct((t, d), F32),
        compiler_params=_cparams(("arbitrary",)),
        name="final_norm",
    )(x, gain.reshape(1, d))


def _peer_layer(x, gain, sc, sh, g2, w_q, keys, u_tab, v_tab, seq):
    wq1, wq2 = _split2(w_q)
    kf = keys.reshape(-1, PEER_NKEYS, keys.shape[-1])
    k1, k2 = _split2(kf)
    hb, scores_t = _peer_q(x, gain, sc, sh, wq1, wq2, k1, k2, seq)
    e1, nrow, e2, rank2 = _peer_topk(scores_t)
    rows = PEER_PB * PEER_NKEYS
    ne, d = u_tab.shape
    u_blk = u_tab.astype(BF16).reshape(ne // rows, rows, d)
    v_blk = v_tab.astype(BF16).reshape(ne // rows, rows, d)
    return _peer_dense(hb, u_blk, v_blk, e1, nrow, e2, rank2, x, g2, seq)


def kernel(x, c, w_ada, b_ada, norm_g, hg_w_in, hg_lb, hg_norm_g, hg_w_out, lru_w_in, lru_conv_w,
           lru_conv_b, lru_w_a, lru_b_a, lru_w_x, lru_b_x, lru_lam, lru_w_out, peer_w_q, peer_keys,
           peer_u, peer_v, final_g):
    batch, seq, d = x.shape
    depth = w_ada.shape[0]
    n_mixers = 2
    mod = _ada_mod(c, w_ada, b_ada)
    xt = x.reshape(batch * seq, d)
    for i in range(depth):
        parts = [mod[i, :, k * d:(k + 1) * d].reshape(batch, 1, d) for k in range(6)]
        sh1, sc1, g1, sh2, sc2, g2 = parts
        j = i // n_mixers
        if i % n_mixers == 0:
            proj = _norm_mod_matmul(xt, norm_g[i, 0], sc1, sh1, hg_w_in[j].astype(BF16), seq,
                                    head_major=True)
            o_fw, o_bw = _hgrn_scan(proj, hg_lb, j, batch, seq)
            xt = _hgrn_out(o_fw, o_bw, proj, hg_norm_g[j], hg_w_out[j].astype(BF16), xt, g1, seq)
        else:
            proj = _norm_mod_matmul(xt, norm_g[i, 0], sc1, sh1, lru_w_in[j].astype(BF16), seq,
                                    head_major=False)
            hs = _lru_scan(proj, lru_conv_w[j], lru_conv_b[j], lru_w_a[j], lru_b_a[j], lru_w_x[j],
                           lru_b_x[j], lru_lam[j], batch, seq)
            xt = _lru_out(hs, proj, lru_w_out[j].astype(BF16), xt, g1, seq)
        xt = _peer_layer(xt, norm_g[i, 1], sc2, sh2, g2, peer_w_q[i], peer_keys[i], peer_u[i],
                         peer_v[i], seq)
    return _final_norm(xt, final_g).reshape(batch, seq, d)
```

```python
import functools

import numpy as np
import jax
import jax.numpy as jnp
from jax import lax
from jax.experimental import pallas as pl
from jax.experimental.pallas import tpu as pltpu

F32 = jnp.float32
BF16 = jnp.bfloat16

SUBLANES = 8
LANES = 128
VMEM_LIMIT = 56 * 1024 * 1024

EPS = 1e-6
HG_HEADS = 8
HG_CHUNK = 128
LRU_BLOCKS = 4
LRU_C = 8.0
CONV_W = 4
CONV_LEFT = 2
PEER_HEADS = 8
PEER_NKEYS = 128
PEER_TOPK = 16
HG_LEVELS = 7
NEG_INF = float("-inf")


def _cparams(sem):
    return pltpu.CompilerParams(dimension_semantics=sem, vmem_limit_bytes=VMEM_LIMIT)


def _split3(x):
    x1 = x.astype(BF16)
    r1 = x - x1.astype(F32)
    x2 = r1.astype(BF16)
    x3 = (r1 - x2.astype(F32)).astype(BF16)
    return x1, x2, x3


def _split2(x):
    x1 = x.astype(BF16)
    x2 = (x - x1.astype(F32)).astype(BF16)
    return x1, x2


def _dot(a, b):
    return jnp.dot(a, b, preferred_element_type=F32)


def _dot_nt(a, b):
    return lax.dot_general(a, b, (((1,), (1,)), ((), ())), preferred_element_type=F32)


def _dot_tn(a, b):
    return lax.dot_general(a, b, (((0,), (0,)), ((), ())), preferred_element_type=F32)


def _norm_mod(x, gain, sc, sh):
    inv = lax.rsqrt(jnp.mean(x * x, axis=-1, keepdims=True) + EPS)
    return (x * inv) * gain * (1.0 + sc) + sh


def _ada_kernel(c_ref, w_ref, b_ref, o_ref):
    c = c_ref[...]
    cond = c * jax.nn.sigmoid(c)
    c1, c2, c3 = _split3(cond)
    w1, w2, w3 = _split3(w_ref[0])
    acc = _dot(c1, w1) + _dot(c1, w2) + _dot(c2, w1)
    acc = acc + _dot(c1, w3) + _dot(c2, w2) + _dot(c3, w1)
    o_ref[0] = acc + b_ref[0]


def _ada_mod(c, w_ada, b_ada):
    depth, d, n = w_ada.shape
    b = c.shape[0]
    cp = jnp.zeros((SUBLANES, d), F32).at[:b].set(c)
    tn = 1536
    out = pl.pallas_call(
        _ada_kernel,
        grid=(depth, n // tn),
        in_specs=[pl.BlockSpec((SUBLANES, d), lambda i, j: (0, 0)),
                  pl.BlockSpec((1, d, tn), lambda i, j: (i, 0, j)),
                  pl.BlockSpec((1, 1, tn), lambda i, j: (i, 0, j))],
        out_specs=pl.BlockSpec((1, SUBLANES, tn), lambda i, j: (i, 0, j)),
        out_shape=jax.ShapeDtypeStruct((depth, SUBLANES, n), F32),
        compiler_params=_cparams(("arbitrary", "arbitrary")),
        name="ada_mod",
    )(cp, w_ada, b_ada.reshape(depth, 1, n))
    return out[:, :b]


def _nmm_kernel(x_ref, g_ref, sc_ref, sh_ref, w_ref, o_ref, h_scr, *, head_major):
    @pl.when(pl.program_id(1) == 0)
    def _():
        h_scr[...] = _norm_mod(x_ref[...], g_ref[...], sc_ref[0], sh_ref[0]).astype(BF16)

    res = _dot(h_scr[...], w_ref[...])
    if head_major:
        for k in range(o_ref.shape[0]):
            o_ref[k] = res[:, k * LANES:(k + 1) * LANES]
    else:
        o_ref[...] = res


def _norm_mod_matmul(x, gain, sc, sh, w_bf16, seq, *, head_major, tm=1024, tn=1024):
    t, d = x.shape
    n = w_bf16.shape[1]
    tpb = seq // tm
    if head_major:
        out_shape = jax.ShapeDtypeStruct((n // LANES, t, LANES), F32)
        out_spec = pl.BlockSpec((tn // LANES, tm, LANES), lambda i, j: (j, i, 0))
    else:
        out_shape = jax.ShapeDtypeStruct((t, n), F32)
        out_spec = pl.BlockSpec((tm, tn), lambda i, j: (i, j))
    return pl.pallas_call(
        functools.partial(_nmm_kernel, head_major=head_major),
        grid=(t // tm, n // tn),
        in_specs=[pl.BlockSpec((tm, d), lambda i, j: (i, 0)),
                  pl.BlockSpec((1, d), lambda i, j: (0, 0)),
                  pl.BlockSpec((1, 1, d), lambda i, j: (i // tpb, 0, 0)),
                  pl.BlockSpec((1, 1, d), lambda i, j: (i // tpb, 0, 0)),
                  pl.BlockSpec((d, tn), lambda i, j: (0, j))],
        out_specs=out_spec,
        out_shape=out_shape,
        scratch_shapes=[pltpu.VMEM((tm, d), BF16)],
        compiler_params=_cparams(("arbitrary", "arbitrary")),
        name="norm_mod_matmul",
    )(x, gain.reshape(1, d), sc, sh, w_bf16)


def _hgrn_sum_mats(reverse):
    c = HG_CHUNK
    mats = []
    for lvl in range(1, HG_LEVELS + 1):
        m = 1 << lvl
        half = m >> 1
        a = np.zeros((c, c), np.float32)
        for t in range(c):
            mid = (t // m) * m + half
            if not reverse:
                if t >= mid:
                    a[t, mid:t + 1] = 1.0
                else:
                    a[t, t + 1:mid] = 1.0
            else:
                if t < mid:
                    a[t, t:mid] = 1.0
                else:
                    a[t, mid:t] = 1.0
        mats.append(a)
    ones = np.ones((c, c), np.float32)
    if not reverse:
        mats += [np.tril(ones), np.triu(ones, 1)]
    else:
        mats += [np.triu(ones), np.tril(ones, -1)]
    return np.concatenate(mats, axis=0)


def _hgrn_chain(q, flog, v, lb, a_ref, st_ref, d, h, reverse):
    c = HG_CHUNK
    f = lb + (1.0 - lb) * jax.nn.sigmoid(flog)
    g = jnp.log(f)
    k = 1.0 - f
    g1, g2 = _split2(g)
    e2 = _dot(a_ref[...], jnp.concatenate([g1, g2], axis=1))
    e = e2[:, :LANES] + e2[:, LANES:]
    xdec = jnp.exp(e)
    rows = lax.broadcasted_iota(jnp.int32, (c, c), 0)
    cols = lax.broadcasted_iota(jnp.int32, (c, c), 1)
    attn = jnp.where(rows == cols, jnp.sum(q * k, axis=-1, keepdims=True), 0.0)
    for lvl in range(1, HG_LEVELS + 1):
        xl = xdec[(lvl - 1) * c:lvl * c]
        upper = ((rows >> (lvl - 1)) & 1) == 1
        qmask = jnp.logical_not(upper) if reverse else upper
        ql = jnp.where(qmask, q * xl, 0.0).astype(BF16)
        kl = jnp.where(qmask, 0.0, k * xl).astype(BF16)
        p = _dot_nt(ql, kl)
        attn = attn + jnp.where((rows >> lvl) == (cols >> lvl), p, 0.0)
    eq = e[HG_LEVELS * c:(HG_LEVELS + 1) * c]
    qd = (q * xdec[HG_LEVELS * c:(HG_LEVELS + 1) * c]).astype(BF16)
    kd = (k * xdec[(HG_LEVELS + 1) * c:(HG_LEVELS + 2) * c]).astype(BF16)
    tot = eq[0:1] if reverse else eq[c - 1:c]
    st = st_ref[d, h]
    vb = v.astype(BF16)
    o = _dot(attn.astype(BF16), vb) + _dot_nt(qd, st.astype(BF16))
    st_ref[d, h] = st * jnp.exp(tot) + _dot_tn(vb, kd)
    return o


def _hgrn_scan_kernel(lb_ref, afw_ref, abw_ref, qf_ref, ff_ref, vf_ref, qb_ref, fb_ref, vb_ref,
                      of_ref, ob_ref, st_ref, lbs_ref, *, layer):
    @pl.when(pl.program_id(1) == 0)
    def _():
        st_ref[...] = jnp.zeros_like(st_ref)
        lb = lb_ref[...]
        ex = jnp.exp(lb - jnp.max(lb, axis=0, keepdims=True))
        p = ex / jnp.sum(ex, axis=0, keepdims=True)
        acc = p[0]
        for jj in range(1, layer + 1):
            acc = acc + p[jj]
        lbs_ref[...] = acc - p[0]

    def body(h, carry):
        lb = lbs_ref[pl.ds(h, 1), :]
        of_ref[h] = _hgrn_chain(qf_ref[h], ff_ref[h], vf_ref[h], lb, afw_ref, st_ref, 0, h, False)
        ob_ref[h] = _hgrn_chain(qb_ref[h], fb_ref[h], vb_ref[h], lb, abw_ref, st_ref, 1, h, True)
        return carry

    lax.fori_loop(0, HG_HEADS, body, 0, unroll=8)


def _hgrn_scan(proj_hm, hg_lb, layer, batch, seq):
    hh = HG_HEADS
    t = proj_hm.shape[1]
    nc = seq // HG_CHUNK
    c = HG_CHUNK
    na = hg_lb.shape[0]
    afw = jnp.asarray(_hgrn_sum_mats(False), BF16)
    abw = jnp.asarray(_hgrn_sum_mats(True), BF16)
    nrow = afw.shape[0]

    def fwd(part):
        return pl.BlockSpec((hh, c, LANES), lambda b, cc: (part, b * nc + cc, 0))

    def bwd(part):
        return pl.BlockSpec((hh, c, LANES), lambda b, cc: (part, b * nc + nc - 1 - cc, 0))

    out_sd = jax.ShapeDtypeStruct((hh, t, LANES), F32)
    return pl.pallas_call(
        functools.partial(_hgrn_scan_kernel, layer=layer),
        grid=(batch, nc),
        in_specs=[pl.BlockSpec((na, hh, LANES), lambda b, cc: (0, 0, 0)),
                  pl.BlockSpec((nrow, c), lambda b, cc: (0, 0)),
                  pl.BlockSpec((nrow, c), lambda b, cc: (0, 0)),
                  fwd(0), fwd(1), fwd(3), bwd(0), bwd(2), bwd(3)],
        out_specs=[pl.BlockSpec((hh, c, LANES), lambda b, cc: (0, b * nc + cc, 0)),
                   pl.BlockSpec((hh, c, LANES), lambda b, cc: (0, b * nc + nc - 1 - cc, 0))],
        out_shape=[out_sd, out_sd],
        scratch_shapes=[pltpu.VMEM((2, hh, LANES, LANES), F32), pltpu.VMEM((hh, LANES), F32)],
        compiler_params=_cparams(("arbitrary", "arbitrary")),
        name="hgrn_scan",
    )(hg_lb.reshape(na, hh, LANES), afw, abw, proj_hm, proj_hm, proj_hm, proj_hm, proj_hm, proj_hm)


def _hgrn_out_kernel(of_ref, ob_ref, gg_ref, ng_ref, w_ref, x_ref, g1_ref, o_ref):
    parts = []
    for h in range(HG_HEADS):
        o = of_ref[h] + ob_ref[h]
        inv = lax.rsqrt(jnp.mean(o * o, axis=-1, keepdims=True) + EPS)
        gg = gg_ref[h]
        parts.append(((o * inv) * ng_ref[pl.ds(h, 1), :] * (gg * jax.nn.sigmoid(gg))).astype(BF16))
    y = _dot(jnp.concatenate(parts, axis=-1), w_ref[...])
    o_ref[...] = x_ref[...] + g1_ref[0] * y


def _hgrn_out(o_fw, o_bw, proj_hm, norm_g, w_out_bf16, x, g1, seq, tm=512):
    hh = HG_HEADS
    t, d = x.shape
    tpb = seq // tm
    return pl.pallas_call(
        _hgrn_out_kernel,
        grid=(t // tm,),
        in_specs=[pl.BlockSpec((hh, tm, LANES), lambda i: (0, i, 0)),
                  pl.BlockSpec((hh, tm, LANES), lambda i: (0, i, 0)),
                  pl.BlockSpec((hh, tm, LANES), lambda i: (4, i, 0)),
                  pl.BlockSpec((hh, LANES), lambda i: (0, 0)),
                  pl.BlockSpec((d, d), lambda i: (0, 0)),
                  pl.BlockSpec((tm, d), lambda i: (i, 0)),
                  pl.BlockSpec((1, 1, d), lambda i: (i // tpb, 0, 0))],
        out_specs=pl.BlockSpec((tm, d), lambda i: (i, 0)),
        out_shape=jax.ShapeDtypeStruct((t, d), F32),
        compiler_params=_cparams(("arbitrary",)),
        name="hgrn_out",
    )(o_fw, o_bw, proj_hm, norm_g.reshape(hh, LANES), w_out_bf16, x, g1)


def _lru_group_scan(a, x, carry, rows, reverse):
    for s in (1, 2, 4):
        if not reverse:
            keep = rows >= s
            a_sh = jnp.where(keep, pltpu.roll(a, s, 0), 1.0)
            x_sh = jnp.where(keep, pltpu.roll(x, s, 0), 0.0)
        else:
            keep = rows < SUBLANES - s
            a_sh = jnp.where(keep, pltpu.roll(a, SUBLANES - s, 0), 1.0)
            x_sh = jnp.where(keep, pltpu.roll(x, SUBLANES - s, 0), 0.0)
        x = x + a * x_sh
        a = a * a_sh
    hs = x + a * carry
    new_carry = hs[0:1] if reverse else hs[SUBLANES - 1:SUBLANES]
    return hs, new_carry


def _lru_kernel(xc_ref, xp_ref, xn_ref, cw_ref, cb_ref, wa_ref, ba_ref, wx_ref, bx_ref, lam_ref,
                o_ref, a_scr, b_scr, carry_scr, *, tm, nchunk):
    d = pl.program_id(0)
    cc = pl.program_id(2)
    chunk = jnp.where(d == 0, cc, nchunk - 1 - cc)

    @pl.when(cc == 0)
    def _():
        carry_scr[...] = jnp.zeros_like(carry_scr)

    xcur = xc_ref[...]
    prev = jnp.where(chunk == 0, 0.0, xp_ref[...])
    nxt = jnp.where(chunk == nchunk - 1, 0.0, xn_ref[...])
    ext = jnp.concatenate([prev, xcur, nxt], axis=0)
    cw = cw_ref[...]
    xc = cb_ref[...]
    for j in range(CONV_W):
        off = SUBLANES - CONV_LEFT + j
        xc = xc + ext[off:off + tm] * cw[j:j + 1]

    xcb = xc.astype(BF16)
    bw = xcb.shape[1] // LRU_BLOCKS
    ra, rx = [], []
    for n in range(LRU_BLOCKS):
        blk = xcb[:, n * bw:(n + 1) * bw]
        ra.append(_dot(blk, wa_ref[0, n]))
        rx.append(_dot(blk, wx_ref[0, n]))
    r = jax.nn.sigmoid(jnp.concatenate(ra, axis=-1) + ba_ref[0])
    ig = jax.nn.sigmoid(jnp.concatenate(rx, axis=-1) + bx_ref[0])
    lam = lam_ref[0]
    softplus_neg = jnp.maximum(-lam, 0.0) + jnp.log(1.0 + jnp.exp(-jnp.abs(lam)))
    log_a = -LRU_C * r * softplus_neg
    a = jnp.exp(log_a)
    a_scr[...] = a
    b_scr[...] = jnp.sqrt(-jnp.tanh(log_a) * (a * a + 1.0)) * ig * xc

    ngroups = tm // SUBLANES
    rows = lax.broadcasted_iota(jnp.int32, (SUBLANES, xc.shape[1]), 0)

    def run(reverse):
        def body(i, carry):
            gi = (ngroups - 1 - i) if reverse else i
            sl = pl.ds(pl.multiple_of(gi * SUBLANES, SUBLANES), SUBLANES)
            hs, carry = _lru_group_scan(a_scr[sl, :], b_scr[sl, :], carry, rows, reverse)
            o_ref[0, sl, :] = hs
            return carry
        carry_scr[...] = lax.fori_loop(0, ngroups, body, carry_scr[...])

    @pl.when(d == 0)
    def _():
        run(False)

    @pl.when(d == 1)
    def _():
        run(True)


def _lru_scan(proj, conv_w, conv_b, w_a, b_a, w_x, b_x, lam, batch, seq, tm=256):
    t = proj.shape[0]
    w = proj.shape[1] // 2
    nchunk = seq // tm
    hb = tm // SUBLANES
    nhalo = t // SUBLANES

    def pos(d, b, cc):
        return b * nchunk + jnp.where(d == 0, cc, nchunk - 1 - cc)

    return pl.pallas_call(
        functools.partial(_lru_kernel, tm=tm, nchunk=nchunk),
        grid=(2, batch, nchunk),
        in_specs=[pl.BlockSpec((tm, w), lambda d, b, cc: (pos(d, b, cc), 0)),
                  pl.BlockSpec((SUBLANES, w), lambda d, b, cc: (jnp.maximum(pos(d, b, cc) * hb - 1, 0), 0)),
                  pl.BlockSpec((SUBLANES, w),
                               lambda d, b, cc: (jnp.minimum((pos(d, b, cc) + 1) * hb, nhalo - 1), 0)),
                  pl.BlockSpec((CONV_W, w), lambda d, b, cc: (0, 0)),
                  pl.BlockSpec((1, w), lambda d, b, cc: (0, 0)),
                  pl.BlockSpec((1, LRU_BLOCKS, w // LRU_BLOCKS, w // LRU_BLOCKS), lambda d, b, cc: (d, 0, 0, 0)),
                  pl.BlockSpec((1, 1, w), lambda d, b, cc: (d, 0, 0)),
                  pl.BlockSpec((1, LRU_BLOCKS, w // LRU_BLOCKS, w // LRU_BLOCKS), lambda d, b, cc: (d, 0, 0, 0)),
                  pl.BlockSpec((1, 1, w), lambda d, b, cc: (d, 0, 0)),
                  pl.BlockSpec((1, 1, w), lambda d, b, cc: (d, 0, 0))],
        out_specs=pl.BlockSpec((1, tm, w), lambda d, b, cc: (d, pos(d, b, cc), 0)),
        out_shape=jax.ShapeDtypeStruct((2, t, w), F32),
        scratch_shapes=[pltpu.VMEM((tm, w), F32), pltpu.VMEM((tm, w), F32), pltpu.VMEM((1, w), F32)],
        compiler_params=_cparams(("arbitrary", "arbitrary", "arbitrary")),
        name="lru_scan",
    )(proj, proj, proj, conv_w, conv_b.reshape(1, w), w_a.astype(BF16), b_a.reshape(2, 1, w),
      w_x.astype(BF16), b_x.reshape(2, 1, w), lam.reshape(2, 1, w))


def _lru_out_kernel(hs_ref, yb_ref, w_ref, x_ref, g1_ref, o_ref):
    u = (hs_ref[0] + hs_ref[1]) * jax.nn.gelu(yb_ref[...])
    o_ref[...] = x_ref[...] + g1_ref[0] * _dot(u.astype(BF16), w_ref[...])


def _lru_out(hs, proj, w_out_bf16, x, g1, seq, tm=512):
    t, d = x.shape
    w = hs.shape[2]
    tpb = seq // tm
    return pl.pallas_call(
        _lru_out_kernel,
        grid=(t // tm,),
        in_specs=[pl.BlockSpec((2, tm, w), lambda i: (0, i, 0)),
                  pl.BlockSpec((tm, w), lambda i: (i, 1)),
                  pl.BlockSpec((w, d), lambda i: (0, 0)),
                  pl.BlockSpec((tm, d), lambda i: (i, 0)),
                  pl.BlockSpec((1, 1, d), lambda i: (i // tpb, 0, 0))],
        out_specs=pl.BlockSpec((tm, d), lambda i: (i, 0)),
        out_shape=jax.ShapeDtypeStruct((t, d), F32),
        compiler_params=_cparams(("arbitrary",)),
        name="lru_out",
    )(hs, proj, w_out_bf16, x, g1)


def _peer_q_kernel(x_ref, g_ref, sc_ref, sh_ref, w1_ref, w2_ref, k1_ref, k2_ref, hb_ref, sc_out_ref):
    h = _norm_mod(x_ref[...], g_ref[...], sc_ref[0], sh_ref[0])
    h1, h2 = _split2(h)
    hb_ref[...] = h.T.astype(BF16)
    q = _dot(h1, w1_ref[...]) + _dot(h1, w2_ref[...]) + _dot(h2, w1_ref[...])
    nhp = k1_ref.shape[0]
    nch = sc_out_ref.shape[1]
    for hp in range(nhp):
        qa, qb = _split2(q[:, hp * LANES:(hp + 1) * LANES])
        s = _dot_nt(k1_ref[hp], qa) + _dot_nt(k1_ref[hp], qb) + _dot_nt(k2_ref[hp], qa)
        for ch in range(nch):
            sc_out_ref[hp, ch] = s[:, ch * LANES:(ch + 1) * LANES]


def _peer_q(x, gain, sc, sh, wq1, wq2, keys1, keys2, seq, tm=256):
    t, d = x.shape
    nq = wq1.shape[1]
    nhp = keys1.shape[0]
    tpb = seq // tm
    nch = tm // LANES
    return pl.pallas_call(
        _peer_q_kernel,
        grid=(t // tm,),
        in_specs=[pl.BlockSpec((tm, d), lambda i: (i, 0)),
                  pl.BlockSpec((1, d), lambda i: (0, 0)),
                  pl.BlockSpec((1, 1, d), lambda i: (i // tpb, 0, 0)),
                  pl.BlockSpec((1, 1, d), lambda i: (i // tpb, 0, 0)),
                  pl.BlockSpec((d, nq), lambda i: (0, 0)),
                  pl.BlockSpec((d, nq), lambda i: (0, 0)),
                  pl.BlockSpec((nhp, PEER_NKEYS, LANES), lambda i: (0, 0, 0)),
                  pl.BlockSpec((nhp, PEER_NKEYS, LANES), lambda i: (0, 0, 0))],
        out_specs=[pl.BlockSpec((d, tm), lambda i: (0, i)),
                   pl.BlockSpec((nhp, nch, PEER_NKEYS, LANES), lambda i: (0, i, 0, 0))],
        out_shape=[jax.ShapeDtypeStruct((d, t), BF16),
                   jax.ShapeDtypeStruct((nhp, t // LANES, PEER_NKEYS, LANES), F32)],
        compiler_params=_cparams(("arbitrary",)),
        name="peer_q",
    )(x, gain.reshape(1, d), sc, sh, wq1, wq2, keys1, keys2)


def _top16(s, rows, exact):
    work = s
    rank = jnp.full(s.shape, PEER_TOPK, jnp.int32)
    vals = []
    for kk in range(PEER_TOPK):
        m = jnp.max(work, axis=0, keepdims=True)
        if exact:
            idx = jnp.min(jnp.where(work == m, rows, PEER_NKEYS), axis=0, keepdims=True)
            sel = rows == idx
        else:
            sel = work == m
        rank = jnp.where(sel, kk, rank)
        work = jnp.where(sel, NEG_INF, work)
        vals.append(m)
    return rank, jnp.concatenate(vals, axis=0)


def _col_count(mask):
    return jnp.sum(jnp.where(mask, 1.0, 0.0), axis=0, keepdims=True)


def _any_lane_differs(count, want):
    return jnp.max(jnp.where(count == want, 0.0, 1.0)) > 0.5


def _cand16(cand, flat, r16, mtop, exact):
    cnt = jnp.zeros((PEER_TOPK, cand.shape[1]), F32)
    z = jnp.zeros((1, cand.shape[1]), F32)
    for _ in range(PEER_TOPK):
        m = jnp.max(cand, axis=0, keepdims=True)
        if exact:
            idx = jnp.min(jnp.where(cand == m, flat, PEER_TOPK * PEER_TOPK), axis=0, keepdims=True)
            cand = jnp.where(flat == idx, NEG_INF, cand)
            cnt = cnt + jnp.where(r16 == (idx >> 4), 1.0, 0.0)
        else:
            cand = jnp.where(cand == m, NEG_INF, cand)
        z = z + jnp.exp(m - mtop)
    if not exact:
        gone = cand == NEG_INF
        cnt = jnp.concatenate(
            [_col_count(gone[0:PEER_TOPK])]
            + [_col_count(gone[PEER_TOPK + SUBLANES * (k1 - 1):PEER_TOPK + SUBLANES * k1]) for k1 in range(1, 8)]
            + [jnp.where(gone[PEER_TOPK + 7 * SUBLANES:], 1.0, 0.0)], axis=0)
    return cnt, z


def _bf16_pair(x):
    u = pltpu.bitcast(x.astype(BF16).astype(F32), jnp.uint32)
    return u | (u >> 16)


PEER_TOPK_GROUP = 4


def _peer_topk_kernel(s_ref, e1_ref, n_ref, e2_ref, r2_ref):
    nch = s_ref.shape[1]
    grp = PEER_TOPK_GROUP
    width = grp * LANES
    rows = lax.broadcasted_iota(jnp.int32, (PEER_NKEYS, width), 0)
    r16 = lax.broadcasted_iota(jnp.int32, (PEER_TOPK, width), 0)
    r8 = lax.broadcasted_iota(jnp.int32, (SUBLANES, width), 0)
    flat = jnp.concatenate([r16] + [r8 + PEER_TOPK * k1 for k1 in range(1, 8)]
                           + [(r8 + 8) * PEER_TOPK], axis=0)
    want = float(PEER_TOPK)

    def body(i, carry):
        chunks = [i * grp + k for k in range(grp)]
        s1 = jnp.concatenate([s_ref[0, ch] for ch in chunks], axis=1)
        s2 = jnp.concatenate([s_ref[1, ch] for ch in chunks], axis=1)
        rank1, v1 = _top16(s1, rows, False)
        rank2, v2 = _top16(s2, rows, False)
        tied = jnp.logical_or(_any_lane_differs(_col_count(rank1 < PEER_TOPK), want),
                              _any_lane_differs(_col_count(rank2 < PEER_TOPK), want))
        rank1, v1, rank2, v2 = lax.cond(
            tied, lambda: _top16(s1, rows, True) + _top16(s2, rows, True), lambda: (rank1, v1, rank2, v2))
        cand = jnp.concatenate([v1[0:1] + v2] + [v1[k1:k1 + 1] + v2[0:8] for k1 in range(1, 8)]
                               + [v1[8:16] + v2[0:1]], axis=0)
        mtop = v1[0:1] + v2[0:1]
        cnt, z = _cand16(cand, flat, r16, mtop, False)
        tied = _any_lane_differs(jnp.sum(cnt, axis=0, keepdims=True), want)
        cnt, z = lax.cond(tied, lambda: _cand16(cand, flat, r16, mtop, True), lambda: (cnt, z))
        nrow = jnp.zeros((PEER_NKEYS, width), F32)
        for kk in range(PEER_TOPK):
            nrow = jnp.where(rank1 == kk, cnt[kk:kk + 1], nrow)
        e1 = _bf16_pair(jnp.where(rank1 < PEER_TOPK, jnp.exp(s1 - v1[0:1]) / z, 0.0))
        nrow = _bf16_pair(nrow)
        e2 = jnp.where(rank2 < PEER_TOPK, jnp.exp(s2 - v2[0:1]), 0.0)
        e2 = pltpu.bitcast(e2.astype(BF16), jnp.uint32)
        r2 = pltpu.bitcast(rank2.astype(F32).astype(BF16), jnp.uint32)
        for k, ch in enumerate(chunks):
            cols = slice(k * LANES, (k + 1) * LANES)
            e1_ref[0, ch] = e1[:, cols]
            n_ref[0, ch] = nrow[:, cols]
            e2_ref[0, ch] = e2[:, cols]
            r2_ref[0, ch] = r2[:, cols]
        return carry

    lax.fori_loop(0, nch // grp, body, 0)


def _peer_topk(scores_t, nch=4):
    nhp, ntc, nk, _ = scores_t.shape
    hh = nhp // 2
    spec = pl.BlockSpec((1, nch, nk, LANES), lambda i, h: (h, i, 0, 0))
    spec_p = pl.BlockSpec((1, nch, nk // 2, LANES), lambda i, h: (h, i, 0, 0))
    sd_u = jax.ShapeDtypeStruct((hh, ntc, nk, LANES), jnp.uint32)
    sd_b = jax.ShapeDtypeStruct((hh, ntc, nk // 2, LANES), jnp.uint32)
    return pl.pallas_call(
        _peer_topk_kernel,
        grid=(ntc // nch, hh),
        in_specs=[pl.BlockSpec((2, nch, nk, LANES), lambda i, h: (h, i, 0, 0))],
        out_specs=[spec, spec, spec_p, spec_p],
        out_shape=[sd_u, sd_u, sd_b, sd_b],
        compiler_params=_cparams(("arbitrary", "arbitrary")),
        name="peer_topk",
    )(scores_t)


def _dup_rows(row_u32, nrows):
    return pltpu.bitcast(jnp.broadcast_to(row_u32, (nrows // 2, row_u32.shape[1])), BF16)


PEER_PB = 2


def _peer_dense_kernel(hbt_ref, u_ref, v_ref, e1_ref, n_ref, e2_ref, r2_ref, x_ref, g2_ref,
                       o_ref, acc_ref, act0, act1, ab0, ab1):
    j = pl.program_id(1)

    @pl.when(j == 0)
    def _():
        acc_ref[...] = jnp.zeros_like(acc_ref)

    nch = e2_ref.shape[1]
    nb = u_ref.shape[0]
    acts = (act0, act1)
    abs_ = (ab0, ab1)
    zero = jnp.zeros((), BF16)

    def act_mm(p):
        acts[p % 2][...] = _dot(u_ref[p], hbt_ref[...])

    def out_mm(p):
        acc_ref[...] += _dot_tn(abs_[p % 2][...], v_ref[p])

    def gate(p):
        for ch in range(nch):
            for a2 in range(PEER_PB):
                a = p * PEER_PB + a2
                w = jnp.zeros((PEER_NKEYS, LANES), BF16)
                for h in range(PEER_HEADS):
                    keep = pltpu.bitcast(r2_ref[h, ch], BF16) < _dup_rows(n_ref[h, ch, a:a + 1, :], PEER_NKEYS)
                    w = w + (jnp.where(keep, pltpu.bitcast(e2_ref[h, ch], BF16), zero)
                             * _dup_rows(e1_ref[h, ch, a:a + 1, :], PEER_NKEYS))
                rows = slice(a2 * PEER_NKEYS, (a2 + 1) * PEER_NKEYS)
                cols = slice(ch * LANES, (ch + 1) * LANES)
                abs_[p % 2][rows, cols] = jax.nn.gelu(acts[p % 2][rows, cols].astype(BF16)) * w

    act_mm(0)
    for p in range(nb):
        if p + 1 < nb:
            act_mm(p + 1)
        if p >= 1:
            out_mm(p - 1)
        gate(p)
    out_mm(nb - 1)

    @pl.when(j == pl.num_programs(1) - 1)
    def _():
        o_ref[...] = x_ref[...] + g2_ref[0] * acc_ref[...]


def _peer_dense(hbt, u_blk, v_blk, e1, nrow, e2, rank2, x, g2, seq, tm=512, na=16):
    t, d = x.shape
    nblk, rows, _ = u_blk.shape
    nb = na // PEER_PB
    nch = tm // LANES
    tpb = seq // tm
    hh = PEER_HEADS
    small = pl.BlockSpec((hh, nch, na, LANES), lambda i, j: (0, i, j, 0))
    big = pl.BlockSpec((hh, nch, PEER_NKEYS // 2, LANES), lambda i, j: (0, i, 0, 0))
    return pl.pallas_call(
        _peer_dense_kernel,
        grid=(t // tm, nblk // nb),
        in_specs=[pl.BlockSpec((d, tm), lambda i, j: (0, i)),
                  pl.BlockSpec((nb, rows, d), lambda i, j: (j, 0, 0)),
                  pl.BlockSpec((nb, rows, d), lambda i, j: (j, 0, 0)),
                  small, small, big, big,
                  pl.BlockSpec((tm, d), lambda i, j: (i, 0)),
                  pl.BlockSpec((1, 1, d), lambda i, j: (i // tpb, 0, 0))],
        out_specs=pl.BlockSpec((tm, d), lambda i, j: (i, 0)),
        out_shape=jax.ShapeDtypeStruct((t, d), F32),
        scratch_shapes=[pltpu.VMEM((tm, d), F32), pltpu.VMEM((rows, tm), F32), pltpu.VMEM((rows, tm), F32),
                        pltpu.VMEM((rows, tm), BF16), pltpu.VMEM((rows, tm), BF16)],
        compiler_params=_cparams(("arbitrary", "arbitrary")),
        name="peer_dense",
    )(hbt, u_blk, v_blk, e1, nrow, e2, rank2, x, g2)


def _final_norm_kernel(x_ref, g_ref, o_ref):
    x = x_ref[...]
    inv = lax.rsqrt(jnp.mean(x * x, axis=-1, keepdims=True) + EPS)
    o_ref[...] = (x * inv) * g_ref[...]


def _final_norm(x, gain, tm=1024):
    t, d = x.shape
    return pl.pallas_call(
        _final_norm_kernel,
        grid=(t // tm,),
        in_specs=[pl.BlockSpec((tm, d), lambda i: (i, 0)), pl.BlockSpec((1, d), lambda i: (0, 0))],
        out_specs=pl.BlockSpec((tm, d), lambda i: (i, 0)),
        out_shape=jax.ShapeDtypeStruct((t, d), F32),
        compiler_params=_cparams(("arbitrary",)),
        name="final_norm",
    )(x, gain.reshape(1, d))


def _peer_layer(x, gain, sc, sh, g2, w_q, keys, u_tab, v_tab, seq):
    wq1, wq2 = _split2(w_q)
    kf = keys.reshape(-1, PEER_NKEYS, keys.shape[-1])
    k1, k2 = _split2(kf)
    hb, scores_t = _peer_q(x, gain, sc, sh, wq1, wq2, k1, k2, seq)
    e1, nrow, e2, rank2 = _peer_topk(scores_t)
    rows = PEER_PB * PEER_NKEYS
    ne, d = u_tab.shape
    u_blk = u_tab.astype(BF16).reshape(ne // rows, rows, d)
    v_blk = v_tab.astype(BF16).reshape(ne // rows, rows, d)
    return _peer_dense(hb, u_blk, v_blk, e1, nrow, e2, rank2, x, g2, seq)


def kernel(x, c, w_ada, b_ada, norm_g, hg_w_in, hg_lb, hg_norm_g, hg_w_out, lru_w_in, lru_conv_w,
           lru_conv_b, lru_w_a, lru_b_a, lru_w_x, lru_b_x, lru_lam, lru_w_out, peer_w_q, peer_keys,
           peer_u, peer_v, final_g):
    batch, seq, d = x.shape
    depth = w_ada.shape[0]
    n_mixers = 2
    mod = _ada_mod(c, w_ada, b_ada)
    xt = x.reshape(batch * seq, d)
    for i in range(depth):
        parts = [mod[i, :, k * d:(k + 1) * d].reshape(batch, 1, d) for k in range(6)]
        sh1, sc1, g1, sh2, sc2, g2 = parts
        j = i // n_mixers
        if i % n_mixers == 0:
            proj = _norm_mod_matmul(xt, norm_g[i, 0], sc1, sh1, hg_w_in[j].astype(BF16), seq,
                                    head_major=True)
            o_fw, o_bw = _hgrn_scan(proj, hg_lb, j, batch, seq)
            xt = _hgrn_out(o_fw, o_bw, proj, hg_norm_g[j], hg_w_out[j].astype(BF16), xt, g1, seq)
        else:
            proj = _norm_mod_matmul(xt, norm_g[i, 0], sc1, sh1, lru_w_in[j].astype(BF16), seq,
                                    head_major=False)
            hs = _lru_scan(proj, lru_conv_w[j], lru_conv_b[j], lru_w_a[j], lru_b_a[j], lru_w_x[j],
                           lru_b_x[j], lru_lam[j], batch, seq)
            xt = _lru_out(hs, proj, lru_w_out[j].astype(BF16), xt, g1, seq)
        xt = _peer_layer(xt, norm_g[i, 1], sc2, sh2, g2, peer_w_q[i], peer_keys[i], peer_u[i],
                         peer_v[i], seq)
    return _final_norm(xt, final_g).reshape(batch, seq, d)
```

```python
import functools

import numpy as np
import jax
import jax.numpy as jnp
from jax import lax
from jax.experimental import pallas as pl
from jax.experimental.pallas import tpu as pltpu

F32 = jnp.float32
BF16 = jnp.bfloat16

SUBLANES = 8
LANES = 128
VMEM_LIMIT = 56 * 1024 * 1024

EPS = 1e-6
HG_HEADS = 8
HG_CHUNK = 128
LRU_BLOCKS = 4
LRU_C = 8.0
CONV_W = 4
CONV_LEFT = 2
PEER_HEADS = 8
PEER_NKEYS = 128
PEER_TOPK = 16
HG_LEVELS = 7
NEG_INF = float("-inf")


def _cparams(sem):
    return pltpu.CompilerParams(dimension_semantics=sem, vmem_limit_bytes=VMEM_LIMIT)


def _split3(x):
    x1 = x.astype(BF16)
    r1 = x - x1.astype(F32)
    x2 = r1.astype(BF16)
    x3 = (r1 - x2.astype(F32)).astype(BF16)
    return x1, x2, x3


def _split2(x):
    x1 = x.astype(BF16)
    x2 = (x - x1.astype(F32)).astype(BF16)
    return x1, x2


def _dot(a, b):
    return jnp.dot(a, b, preferred_element_type=F32)


def _dot_nt(a, b):
    return lax.dot_general(a, b, (((1,), (1,)), ((), ())), preferred_element_type=F32)


def _dot_tn(a, b):
    return lax.dot_general(a, b, (((0,), (0,)), ((), ())), preferred_element_type=F32)


def _norm_mod(x, gain, sc, sh):
    inv = lax.rsqrt(jnp.mean(x * x, axis=-1, keepdims=True) + EPS)
    return (x * inv) * gain * (1.0 + sc) + sh


def _ada_kernel(c_ref, w_ref, b_ref, o_ref):
    c = c_ref[...]
    cond = c * jax.nn.sigmoid(c)
    c1, c2, c3 = _split3(cond)
    w1, w2, w3 = _split3(w_ref[0])
    acc = _dot(c1, w1) + _dot(c1, w2) + _dot(c2, w1)
    acc = acc + _dot(c1, w3) + _dot(c2, w2) + _dot(c3, w1)
    o_ref[0] = acc + b_ref[0]


def _ada_mod(c, w_ada, b_ada):
    depth, d, n = w_ada.shape
    b = c.shape[0]
    cp = jnp.zeros((SUBLANES, d), F32).at[:b].set(c)
    tn = 1536
    out = pl.pallas_call(
        _ada_kernel,
        grid=(depth, n // tn),
        in_specs=[pl.BlockSpec((SUBLANES, d), lambda i, j: (0, 0)),
                  pl.BlockSpec((1, d, tn), lambda i, j: (i, 0, j)),
                  pl.BlockSpec((1, 1, tn), lambda i, j: (i, 0, j))],
        out_specs=pl.BlockSpec((1, SUBLANES, tn), lambda i, j: (i, 0, j)),
        out_shape=jax.ShapeDtypeStruct((depth, SUBLANES, n), F32),
        compiler_params=_cparams(("arbitrary", "arbitrary")),
        name="ada_mod",
    )(cp, w_ada, b_ada.reshape(depth, 1, n))
    return out[:, :b]


def _nmm_kernel(x_ref, g_ref, sc_ref, sh_ref, w_ref, o_ref, h_scr, *, head_major):
    @pl.when(pl.program_id(1) == 0)
    def _():
        h_scr[...] = _norm_mod(x_ref[...], g_ref[...], sc_ref[0], sh_ref[0]).astype(BF16)

    res = _dot(h_scr[...], w_ref[...])
    if head_major:
        for k in range(o_ref.shape[0]):
            o_ref[k] = res[:, k * LANES:(k + 1) * LANES]
    else:
        o_ref[...] = res


def _norm_mod_matmul(x, gain, sc, sh, w_bf16, seq, *, head_major, tm=1024, tn=1024):
    t, d = x.shape
    n = w_bf16.shape[1]
    tpb = seq // tm
    if head_major:
        out_shape = jax.ShapeDtypeStruct((n // LANES, t, LANES), F32)
        out_spec = pl.BlockSpec((tn // LANES, tm, LANES), lambda i, j: (j, i, 0))
    else:
        out_shape = jax.ShapeDtypeStruct((t, n), F32)
        out_spec = pl.BlockSpec((tm, tn), lambda i, j: (i, j))
    return pl.pallas_call(
        functools.partial(_nmm_kernel, head_major=head_major),
        grid=(t // tm, n // tn),
        in_specs=[pl.BlockSpec((tm, d), lambda i, j: (i, 0)),
                  pl.BlockSpec((1, d), lambda i, j: (0, 0)),
                  pl.BlockSpec((1, 1, d), lambda i, j: (i // tpb, 0, 0)),
                  pl.BlockSpec((1, 1, d), lambda i, j: (i // tpb, 0, 0)),
                  pl.BlockSpec((d, tn), lambda i, j: (0, j))],
        out_specs=out_spec,
        out_shape=out_shape,
        scratch_shapes=[pltpu.VMEM((tm, d), BF16)],
        compiler_params=_cparams(("arbitrary", "arbitrary")),
        name="norm_mod_matmul",
    )(x, gain.reshape(1, d), sc, sh, w_bf16)


def _hgrn_sum_mats(reverse):
    c = HG_CHUNK
    mats = []
    for lvl in range(1, HG_LEVELS + 1):
        m = 1 << lvl
        half = m >> 1
        a = np.zeros((c, c), np.float32)
        for t in range(c):
            mid = (t // m) * m + half
            if not reverse:
                if t >= mid:
                    a[t, mid:t + 1] = 1.0
                else:
                    a[t, t + 1:mid] = 1.0
            else:
                if t < mid:
                    a[t, t:mid] = 1.0
                else:
                    a[t, mid:t] = 1.0
        mats.append(a)
    ones = np.ones((c, c), np.float32)
    if not reverse:
        mats += [np.tril(ones), np.triu(ones, 1)]
    else:
        mats += [np.triu(ones), np.tril(ones, -1)]
    return np.concatenate(mats, axis=0)


def _hgrn_chain(q, flog, v, lb, a_ref, st_ref, d, h, reverse):
    c = HG_CHUNK
    f = lb + (1.0 - lb) * jax.nn.sigmoid(flog)
    g = jnp.log(f)
    k = 1.0 - f
    g1, g2 = _split2(g)
    e2 = _dot(a_ref[...], jnp.concatenate([g1, g2], axis=1))
    e = e2[:, :LANES] + e2[:, LANES:]
    xdec = jnp.exp(e)
    rows = lax.broadcasted_iota(jnp.int32, (c, c), 0)
    cols = lax.broadcasted_iota(jnp.int32, (c, c), 1)
    attn = jnp.where(rows == cols, jnp.sum(q * k, axis=-1, keepdims=True), 0.0)
    for lvl in range(1, HG_LEVELS + 1):
        xl = xdec[(lvl - 1) * c:lvl * c]
        upper = ((rows >> (lvl - 1)) & 1) == 1
        qmask = jnp.logical_not(upper) if reverse else upper
        ql = jnp.where(qmask, q * xl, 0.0).astype(BF16)
        kl = jnp.where(qmask, 0.0, k * xl).astype(BF16)
        p = _dot_nt(ql, kl)
        attn = attn + jnp.where((rows >> lvl) == (cols >> lvl), p, 0.0)
    eq = e[HG_LEVELS * c:(HG_LEVELS + 1) * c]
    qd = (q * xdec[HG_LEVELS * c:(HG_LEVELS + 1) * c]).astype(BF16)
    kd = (k * xdec[(HG_LEVELS + 1) * c:(HG_LEVELS + 2) * c]).astype(BF16)
    tot = eq[0:1] if reverse else eq[c - 1:c]
    st = st_ref[d, h]
    vb = v.astype(BF16)
    o = _dot(attn.astype(BF16), vb) + _dot_nt(qd, st.astype(BF16))
    st_ref[d, h] = st * jnp.exp(tot) + _dot_tn(vb, kd)
    return o


def _hgrn_scan_kernel(lb_ref, afw_ref, abw_ref, qf_ref, ff_ref, vf_ref, qb_ref, fb_ref, vb_ref,
                      of_ref, ob_ref, st_ref, lbs_ref, *, layer):
    @pl.when(pl.program_id(1) == 0)
    def _():
        st_ref[...] = jnp.zeros_like(st_ref)
        lb = lb_ref[...]
        ex = jnp.exp(lb - jnp.max(lb, axis=0, keepdims=True))
        p = ex / jnp.sum(ex, axis=0, keepdims=True)
        acc = p[0]
        for jj in range(1, layer + 1):
            acc = acc + p[jj]
        lbs_ref[...] = acc - p[0]

    def body(h, carry):
        lb = lbs_ref[pl.ds(h, 1), :]
        of_ref[h] = _hgrn_chain(qf_ref[h], ff_ref[h], vf_ref[h], lb, afw_ref, st_ref, 0, h, False)
        ob_ref[h] = _hgrn_chain(qb_ref[h], fb_ref[h], vb_ref[h], lb, abw_ref, st_ref, 1, h, True)
        return carry

    lax.fori_loop(0, HG_HEADS, body, 0, unroll=8)


def _hgrn_scan(proj_hm, hg_lb, layer, batch, seq):
    hh = HG_HEADS
    t = proj_hm.shape[1]
    nc = seq // HG_CHUNK
    c = HG_CHUNK
    na = hg_lb.shape[0]
    afw = jnp.asarray(_hgrn_sum_mats(False), BF16)
    abw = jnp.asarray(_hgrn_sum_mats(True), BF16)
    nrow = afw.shape[0]

    def fwd(part):
        return pl.BlockSpec((hh, c, LANES), lambda b, cc: (part, b * nc + cc, 0))

    def bwd(part):
        return pl.BlockSpec((hh, c, LANES), lambda b, cc: (part, b * nc + nc - 1 - cc, 0))

    out_sd = jax.ShapeDtypeStruct((hh, t, LANES), F32)
    return pl.pallas_call(
        functools.partial(_hgrn_scan_kernel, layer=layer),
        grid=(batch, nc),
        in_specs=[pl.BlockSpec((na, hh, LANES), lambda b, cc: (0, 0, 0)),
                  pl.BlockSpec((nrow, c), lambda b, cc: (0, 0)),
                  pl.BlockSpec((nrow, c), lambda b, cc: (0, 0)),
                  fwd(0), fwd(1), fwd(3), bwd(0), bwd(2), bwd(3)],
        out_specs=[pl.BlockSpec((hh, c, LANES), lambda b, cc: (0, b * nc + cc, 0)),
                   pl.BlockSpec((hh, c, LANES), lambda b, cc: (0, b * nc + nc - 1 - cc, 0))],
        out_shape=[out_sd, out_sd],
        scratch_shapes=[pltpu.VMEM((2, hh, LANES, LANES), F32), pltpu.VMEM((hh, LANES), F32)],
        compiler_params=_cparams(("arbitrary", "arbitrary")),
        name="hgrn_scan",
    )(hg_lb.reshape(na, hh, LANES), afw, abw, proj_hm, proj_hm, proj_hm, proj_hm, proj_hm, proj_hm)


def _hgrn_out_kernel(of_ref, ob_ref, gg_ref, ng_ref, w_ref, x_ref, g1_ref, o_ref):
    parts = []
    for h in range(HG_HEADS):
        o = of_ref[h] + ob_ref[h]
        inv = lax.rsqrt(jnp.mean(o * o, axis=-1, keepdims=True) + EPS)
        gg = gg_ref[h]
        parts.append(((o * inv) * ng_ref[pl.ds(h, 1), :] * (gg * jax.nn.sigmoid(gg))).astype(BF16))
    y = _dot(jnp.concatenate(parts, axis=-1), w_ref[...])
    o_ref[...] = x_ref[...] + g1_ref[0] * y


def _hgrn_out(o_fw, o_bw, proj_hm, norm_g, w_out_bf16, x, g1, seq, tm=512):
    hh = HG_HEADS
    t, d = x.shape
    tpb = seq // tm
    return pl.pallas_call(
        _hgrn_out_kernel,
        grid=(t // tm,),
        in_specs=[pl.BlockSpec((hh, tm, LANES), lambda i: (0, i, 0)),
                  pl.BlockSpec((hh, tm, LANES), lambda i: (0, i, 0)),
                  pl.BlockSpec((hh, tm, LANES), lambda i: (4, i, 0)),
                  pl.BlockSpec((hh, LANES), lambda i: (0, 0)),
                  pl.BlockSpec((d, d), lambda i: (0, 0)),
                  pl.BlockSpec((tm, d), lambda i: (i, 0)),
                  pl.BlockSpec((1, 1, d), lambda i: (i // tpb, 0, 0))],
        out_specs=pl.BlockSpec((tm, d), lambda i: (i, 0)),
        out_shape=jax.ShapeDtypeStruct((t, d), F32),
        compiler_params=_cparams(("arbitrary",)),
        name="hgrn_out",
    )(o_fw, o_bw, proj_hm, norm_g.reshape(hh, LANES), w_out_bf16, x, g1)


def _lru_group_scan(a, x, carry, rows, reverse):
    for s in (1, 2, 4):
        if not reverse:
            keep = rows >= s
            a_sh = jnp.where(keep, pltpu.roll(a, s, 0), 1.0)
            x_sh = jnp.where(keep, pltpu.roll(x, s, 0), 0.0)
        else:
            keep = rows < SUBLANES - s
            a_sh = jnp.where(keep, pltpu.roll(a, SUBLANES - s, 0), 1.0)
            x_sh = jnp.where(keep, pltpu.roll(x, SUBLANES - s, 0), 0.0)
        x = x + a * x_sh
        a = a * a_sh
    hs = x + a * carry
    new_carry = hs[0:1] if reverse else hs[SUBLANES - 1:SUBLANES]
    return hs, new_carry


def _lru_kernel(xc_ref, xp_ref, xn_ref, cw_ref, cb_ref, wa_ref, ba_ref, wx_ref, bx_ref, lam_ref,
                o_ref, a_scr, b_scr, carry_scr, *, tm, nchunk):
    d = pl.program_id(0)
    cc = pl.program_id(2)
    chunk = jnp.where(d == 0, cc, nchunk - 1 - cc)

    @pl.when(cc == 0)
    def _():
        carry_scr[...] = jnp.zeros_like(carry_scr)

    xcur = xc_ref[...]
    prev = jnp.where(chunk == 0, 0.0, xp_ref[...])
    nxt = jnp.where(chunk == nchunk - 1, 0.0, xn_ref[...])
    ext = jnp.concatenate([prev, xcur, nxt], axis=0)
    cw = cw_ref[...]
    xc = cb_ref[...]
    for j in range(CONV_W):
        off = SUBLANES - CONV_LEFT + j
        xc = xc + ext[off:off + tm] * cw[j:j + 1]

    xcb = xc.astype(BF16)
    bw = xcb.shape[1] // LRU_BLOCKS
    ra, rx = [], []
    for n in range(LRU_BLOCKS):
        blk = xcb[:, n * bw:(n + 1) * bw]
        ra.append(_dot(blk, wa_ref[0, n]))
        rx.append(_dot(blk, wx_ref[0, n]))
    r = jax.nn.sigmoid(jnp.concatenate(ra, axis=-1) + ba_ref[0])
    ig = jax.nn.sigmoid(jnp.concatenate(rx, axis=-1) + bx_ref[0])
    lam = lam_ref[0]
    softplus_neg = jnp.maximum(-lam, 0.0) + jnp.log(1.0 + jnp.exp(-jnp.abs(lam)))
    log_a = -LRU_C * r * softplus_neg
    a = jnp.exp(log_a)
    a_scr[...] = a
    b_scr[...] = jnp.sqrt(-jnp.tanh(log_a) * (a * a + 1.0)) * ig * xc

    ngroups = tm // SUBLANES
    rows = lax.broadcasted_iota(jnp.int32, (SUBLANES, xc.shape[1]), 0)

    def run(reverse):
        def body(i, carry):
            gi = (ngroups - 1 - i) if reverse else i
            sl = pl.ds(pl.multiple_of(gi * SUBLANES, SUBLANES), SUBLANES)
            hs, carry = _lru_group_scan(a_scr[sl, :], b_scr[sl, :], carry, rows, reverse)
            o_ref[0, sl, :] = hs
            return carry
        carry_scr[...] = lax.fori_loop(0, ngroups, body, carry_scr[...])

    @pl.when(d == 0)
    def _():
        run(False)

    @pl.when(d == 1)
    def _():
        run(True)


def _lru_scan(proj, conv_w, conv_b, w_a, b_a, w_x, b_x, lam, batch, seq, tm=256):
    t = proj.shape[0]
    w = proj.shape[1] // 2
    nchunk = seq // tm
    hb = tm // SUBLANES
    nhalo = t // SUBLANES

    def pos(d, b, cc):
        return b * nchunk + jnp.where(d == 0, cc, nchunk - 1 - cc)

    return pl.pallas_call(
        functools.partial(_lru_kernel, tm=tm, nchunk=nchunk),
        grid=(2, batch, nchunk),
        in_specs=[pl.BlockSpec((tm, w), lambda d, b, cc: (pos(d, b, cc), 0)),
                  pl.BlockSpec((SUBLANES, w), lambda d, b, cc: (jnp.maximum(pos(d, b, cc) * hb - 1, 0), 0)),
                  pl.BlockSpec((SUBLANES, w),
                               lambda d, b, cc: (jnp.minimum((pos(d, b, cc) + 1) * hb, nhalo - 1), 0)),
                  pl.BlockSpec((CONV_W, w), lambda d, b, cc: (0, 0)),
                  pl.BlockSpec((1, w), lambda d, b, cc: (0, 0)),
                  pl.BlockSpec((1, LRU_BLOCKS, w // LRU_BLOCKS, w // LRU_BLOCKS), lambda d, b, cc: (d, 0, 0, 0)),
                  pl.BlockSpec((1, 1, w), lambda d, b, cc: (d, 0, 0)),
                  pl.BlockSpec((1, LRU_BLOCKS, w // LRU_BLOCKS, w // LRU_BLOCKS), lambda d, b, cc: (d, 0, 0, 0)),
                  pl.BlockSpec((1, 1, w), lambda d, b, cc: (d, 0, 0)),
                  pl.BlockSpec((1, 1, w), lambda d, b, cc: (d, 0, 0))],
        out_specs=pl.BlockSpec((1, tm, w), lambda d, b, cc: (d, pos(d, b, cc), 0)),
        out_shape=jax.ShapeDtypeStruct((2, t, w), F32),
        scratch_shapes=[pltpu.VMEM((tm, w), F32), pltpu.VMEM((tm, w), F32), pltpu.VMEM((1, w), F32)],
        compiler_params=_cparams(("arbitrary", "arbitrary", "arbitrary")),
        name="lru_scan",
    )(proj, proj, proj, conv_w, conv_b.reshape(1, w), w_a.astype(BF16), b_a.reshape(2, 1, w),
      w_x.astype(BF16), b_x.reshape(2, 1, w), lam.reshape(2, 1, w))


def _lru_out_kernel(hs_ref, yb_ref, w_ref, x_ref, g1_ref, o_ref):
    u = (hs_ref[0] + hs_ref[1]) * jax.nn.gelu(yb_ref[...])
    o_ref[...] = x_ref[...] + g1_ref[0] * _dot(u.astype(BF16), w_ref[...])


def _lru_out(hs, proj, w_out_bf16, x, g1, seq, tm=512):
    t, d = x.shape
    w = hs.shape[2]
    tpb = seq // tm
    return pl.pallas_call(
        _lru_out_kernel,
        grid=(t // tm,),
        in_specs=[pl.BlockSpec((2, tm, w), lambda i: (0, i, 0)),
                  pl.BlockSpec((tm, w), lambda i: (i, 1)),
                  pl.BlockSpec((w, d), lambda i: (0, 0)),
                  pl.BlockSpec((tm, d), lambda i: (i, 0)),
                  pl.BlockSpec((1, 1, d), lambda i: (i // tpb, 0, 0))],
        out_specs=pl.BlockSpec((tm, d), lambda i: (i, 0)),
        out_shape=jax.ShapeDtypeStruct((t, d), F32),
        compiler_params=_cparams(("arbitrary",)),
        name="lru_out",
    )(hs, proj, w_out_bf16, x, g1)


def _peer_q_kernel(x_ref, g_ref, sc_ref, sh_ref, w1_ref, w2_ref, k1_ref, k2_ref, hb_ref, sc_out_ref):
    h = _norm_mod(x_ref[...], g_ref[...], sc_ref[0], sh_ref[0])
    h1, h2 = _split2(h)
    hb_ref[...] = h.T.astype(BF16)
    q = _dot(h1, w1_ref[...]) + _dot(h1, w2_ref[...]) + _dot(h2, w1_ref[...])
    nhp = k1_ref.shape[0]
    nch = sc_out_ref.shape[1]
    for hp in range(nhp):
        qa, qb = _split2(q[:, hp * LANES:(hp + 1) * LANES])
        s = _dot_nt(k1_ref[hp], qa) + _dot_nt(k1_ref[hp], qb) + _dot_nt(k2_ref[hp], qa)
        for ch in range(nch):
            sc_out_ref[hp, ch] = s[:, ch * LANES:(ch + 1) * LANES]


def _peer_q(x, gain, sc, sh, wq1, wq2, keys1, keys2, seq, tm=256):
    t, d = x.shape
    nq = wq1.shape[1]
    nhp = keys1.shape[0]
    tpb = seq // tm
    nch = tm // LANES
    return pl.pallas_call(
        _peer_q_kernel,
        grid=(t // tm,),
        in_specs=[pl.BlockSpec((tm, d), lambda i: (i, 0)),
                  pl.BlockSpec((1, d), lambda i: (0, 0)),
                  pl.BlockSpec((1, 1, d), lambda i: (i // tpb, 0, 0)),
                  pl.BlockSpec((1, 1, d), lambda i: (i // tpb, 0, 0)),
                  pl.BlockSpec((d, nq), lambda i: (0, 0)),
                  pl.BlockSpec((d, nq), lambda i: (0, 0)),
                  pl.BlockSpec((nhp, PEER_NKEYS, LANES), lambda i: (0, 0, 0)),
                  pl.BlockSpec((nhp, PEER_NKEYS, LANES), lambda i: (0, 0, 0))],
        out_specs=[pl.BlockSpec((d, tm), lambda i: (0, i)),
                   pl.BlockSpec((nhp, nch, PEER_NKEYS, LANES), lambda i: (0, i, 0, 0))],
        out_shape=[jax.ShapeDtypeStruct((d, t), BF16),
                   jax.ShapeDtypeStruct((nhp, t // LANES, PEER_NKEYS, LANES), F32)],
        compiler_params=_cparams(("arbitrary",)),
        name="peer_q",
    )(x, gain.reshape(1, d), sc, sh, wq1, wq2, keys1, keys2)


def _top16(s, rows, exact):
    work = s
    rank = jnp.full(s.shape, PEER_TOPK, jnp.int32)
    vals = []
    for kk in range(PEER_TOPK):
        m = jnp.max(work, axis=0, keepdims=True)
        if exact:
            idx = jnp.min(jnp.where(work == m, rows, PEER_NKEYS), axis=0, keepdims=True)
            sel = rows == idx
        else:
            sel = work == m
        rank = jnp.where(sel, kk, rank)
        work = jnp.where(sel, NEG_INF, work)
        vals.append(m)
    return rank, jnp.concatenate(vals, axis=0)


def _col_count(mask):
    return jnp.sum(jnp.where(mask, 1.0, 0.0), axis=0, keepdims=True)


def _any_lane_differs(count, want):
    return jnp.max(jnp.where(count == want, 0.0, 1.0)) > 0.5


def _cand16(cand, flat, r16, mtop, exact):
    cnt = jnp.zeros((PEER_TOPK, cand.shape[1]), F32)
    z = jnp.zeros((1, cand.shape[1]), F32)
    for _ in range(PEER_TOPK):
        m = jnp.max(cand, axis=0, keepdims=True)
        if exact:
            idx = jnp.min(jnp.where(cand == m, flat, PEER_TOPK * PEER_TOPK), axis=0, keepdims=True)
            cand = jnp.where(flat == idx, NEG_INF, cand)
            cnt = cnt + jnp.where(r16 == (idx >> 4), 1.0, 0.0)
        else:
            cand = jnp.where(cand == m, NEG_INF, cand)
        z = z + jnp.exp(m - mtop)
    if not exact:
        gone = cand == NEG_INF
        cnt = jnp.concatenate(
            [_col_count(gone[0:PEER_TOPK])]
            + [_col_count(gone[PEER_TOPK + SUBLANES * (k1 - 1):PEER_TOPK + SUBLANES * k1]) for k1 in range(1, 8)]
            + [jnp.where(gone[PEER_TOPK + 7 * SUBLANES:], 1.0, 0.0)], axis=0)
    return cnt, z


def _bf16_pair(x):
    u = pltpu.bitcast(x.astype(BF16).astype(F32), jnp.uint32)
    return u | (u >> 16)


PEER_TOPK_GROUP = 4


def _peer_topk_kernel(s_ref, e1_ref, n_ref, e2_ref, r2_ref):
    nch = s_ref.shape[1]
    grp = PEER_TOPK_GROUP
    width = grp * LANES
    rows = lax.broadcasted_iota(jnp.int32, (PEER_NKEYS, width), 0)
    r16 = lax.broadcasted_iota(jnp.int32, (PEER_TOPK, width), 0)
    r8 = lax.broadcasted_iota(jnp.int32, (SUBLANES, width), 0)
    flat = jnp.concatenate([r16] + [r8 + PEER_TOPK * k1 for k1 in range(1, 8)]
                           + [(r8 + 8) * PEER_TOPK], axis=0)
    want = float(PEER_TOPK)

    def body(i, carry):
        chunks = [i * grp + k for k in range(grp)]
        s1 = jnp.concatenate([s_ref[0, ch] for ch in chunks], axis=1)
        s2 = jnp.concatenate([s_ref[1, ch] for ch in chunks], axis=1)
        rank1, v1 = _top16(s1, rows, False)
        rank2, v2 = _top16(s2, rows, False)
        tied = jnp.logical_or(_any_lane_differs(_col_count(rank1 < PEER_TOPK), want),
                              _any_lane_differs(_col_count(rank2 < PEER_TOPK), want))
        rank1, v1, rank2, v2 = lax.cond(
            tied, lambda: _top16(s1, rows, True) + _top16(s2, rows, True), lambda: (rank1, v1, rank2, v2))
        cand = jnp.concatenate([v1[0:1] + v2] + [v1[k1:k1 + 1] + v2[0:8] for k1 in range(1, 8)]
                               + [v1[8:16] + v2[0:1]], axis=0)
        mtop = v1[0:1] + v2[0:1]
        cnt, z = _cand16(cand, flat, r16, mtop, False)
        tied = _any_lane_differs(jnp.sum(cnt, axis=0, keepdims=True), want)
        cnt, z = lax.cond(tied, lambda: _cand16(cand, flat, r16, mtop, True), lambda: (cnt, z))
        nrow = jnp.zeros((PEER_NKEYS, width), F32)
        for kk in range(PEER_TOPK):
            nrow = jnp.where(rank1 == kk, cnt[kk:kk + 1], nrow)
        e1 = _bf16_pair(jnp.where(rank1 < PEER_TOPK, jnp.exp(s1 - v1[0:1]) / z, 0.0))
        nrow = _bf16_pair(nrow)
        e2 = jnp.where(rank2 < PEER_TOPK, jnp.exp(s2 - v2[0:1]), 0.0)
        e2 = pltpu.bitcast(e2.astype(BF16), jnp.uint32)
        r2 = pltpu.bitcast(rank2.astype(F32).astype(BF16), jnp.uint32)
        for k, ch in enumerate(chunks):
            cols = slice(k * LANES, (k + 1) * LANES)
            e1_ref[0, ch] = e1[:, cols]
            n_ref[0, ch] = nrow[:, cols]
            e2_ref[0, ch] = e2[:, cols]
            r2_ref[0, ch] = r2[:, cols]
        return carry

    lax.fori_loop(0, nch // grp, body, 0)


def _peer_topk(scores_t, nch=4):
    nhp, ntc, nk, _ = scores_t.shape
    hh = nhp // 2
    spec = pl.BlockSpec((1, nch, nk, LANES), lambda i, h: (h, i, 0, 0))
    spec_p = pl.BlockSpec((1, nch, nk // 2, LANES), lambda i, h: (h, i, 0, 0))
    sd_u = jax.ShapeDtypeStruct((hh, ntc, nk, LANES), jnp.uint32)
    sd_b = jax.ShapeDtypeStruct((hh, ntc, nk // 2, LANES), jnp.uint32)
    return pl.pallas_call(
        _peer_topk_kernel,
        grid=(ntc // nch, hh),
        in_specs=[pl.BlockSpec((2, nch, nk, LANES), lambda i, h: (h, i, 0, 0))],
        out_specs=[spec, spec, spec_p, spec_p],
        out_shape=[sd_u, sd_u, sd_b, sd_b],
        compiler_params=_cparams(("arbitrary", "arbitrary")),
        name="peer_topk",
    )(scores_t)


def _dup_rows(row_u32, nrows):
    return pltpu.bitcast(jnp.broadcast_to(row_u32, (nrows // 2, row_u32.shape[1])), BF16)


PEER_PB = 4


def _peer_dense_kernel(hbt_ref, u_ref, v_ref, e1_ref, n_ref, e2_ref, r2_ref, x_ref, g2_ref,
                       o_ref, acc_ref, act0, act1, ab0, ab1):
    j = pl.program_id(1)

    @pl.when(j == 0)
    def _():
        acc_ref[...] = jnp.zeros_like(acc_ref)

    nch = e2_ref.shape[1]
    nb = u_ref.shape[0]
    acts = (act0, act1)
    abs_ = (ab0, ab1)
    zero = jnp.zeros((), BF16)

    def act_mm(p):
        acts[p % 2][...] = _dot(u_ref[p], hbt_ref[...])

    def out_mm(p):
        acc_ref[...] += _dot_tn(abs_[p % 2][...], v_ref[p])

    def gate(p):
        for ch in range(nch):
            for a2 in range(PEER_PB):
                a = p * PEER_PB + a2
                w = jnp.zeros((PEER_NKEYS, LANES), BF16)
                for h in range(PEER_HEADS):
                    keep = pltpu.bitcast(r2_ref[h, ch], BF16) < _dup_rows(n_ref[h, ch, a:a + 1, :], PEER_NKEYS)
                    w = w + (jnp.where(keep, pltpu.bitcast(e2_ref[h, ch], BF16), zero)
                             * _dup_rows(e1_ref[h, ch, a:a + 1, :], PEER_NKEYS))
                rows = slice(a2 * PEER_NKEYS, (a2 + 1) * PEER_NKEYS)
                cols = slice(ch * LANES, (ch + 1) * LANES)
                abs_[p % 2][rows, cols] = jax.nn.gelu(acts[p % 2][rows, cols].astype(BF16)) * w

    act_mm(0)
    for p in range(nb):
        if p + 1 < nb:
            act_mm(p + 1)
        if p >= 1:
            out_mm(p - 1)
        gate(p)
    out_mm(nb - 1)

    @pl.when(j == pl.num_programs(1) - 1)
    def _():
        o_ref[...] = x_ref[...] + g2_ref[0] * acc_ref[...]


def _peer_dense(hbt, u_blk, v_blk, e1, nrow, e2, rank2, x, g2, seq, tm=512, na=16):
    t, d = x.shape
    nblk, rows, _ = u_blk.shape
    nb = na // PEER_PB
    nch = tm // LANES
    tpb = seq // tm
    hh = PEER_HEADS
    small = pl.BlockSpec((hh, nch, na, LANES), lambda i, j: (0, i, j, 0))
    big = pl.BlockSpec((hh, nch, PEER_NKEYS // 2, LANES), lambda i, j: (0, i, 0, 0))
    return pl.pallas_call(
        _peer_dense_kernel,
        grid=(t // tm, nblk // nb),
        in_specs=[pl.BlockSpec((d, tm), lambda i, j: (0, i)),
                  pl.BlockSpec((nb, rows, d), lambda i, j: (j, 0, 0)),
                  pl.BlockSpec((nb, rows, d), lambda i, j: (j, 0, 0)),
                  small, small, big, big,
                  pl.BlockSpec((tm, d), lambda i, j: (i, 0)),
                  pl.BlockSpec((1, 1, d), lambda i, j: (i // tpb, 0, 0))],
        out_specs=pl.BlockSpec((tm, d), lambda i, j: (i, 0)),
        out_shape=jax.ShapeDtypeStruct((t, d), F32),
        scratch_shapes=[pltpu.VMEM((tm, d), F32), pltpu.VMEM((rows, tm), F32), pltpu.VMEM((rows, tm), F32),
                        pltpu.VMEM((rows, tm), BF16), pltpu.VMEM((rows, tm), BF16)],
        compiler_params=_cparams(("arbitrary", "arbitrary")),
        name="peer_dense",
    )(hbt, u_blk, v_blk, e1, nrow, e2, rank2, x, g2)


def _final_norm_kernel(x_ref, g_ref, o_ref):
    x = x_ref[...]
    inv = lax.rsqrt(jnp.mean(x * x, axis=-1, keepdims=True) + EPS)
    o_ref[...] = (x * inv) * g_ref[...]


def _final_norm(x, gain, tm=1024):
    t, d = x.shape
    return pl.pallas_call(
        _final_norm_kernel,
        grid=(t // tm,),
        in_specs=[pl.BlockSpec((tm, d), lambda i: (i, 0)), pl.BlockSpec((1, d), lambda i: (0, 0))],
        out_specs=pl.BlockSpec((tm, d), lambda i: (i, 0)),
        out_shape=jax.ShapeDtypeStruct((t, d), F32),
        compiler_params=_cparams(("arbitrary",)),
        name="final_norm",
    )(x, gain.reshape(1, d))


def _peer_layer(x, gain, sc, sh, g2, w_q, keys, u_tab, v_tab, seq):
    wq1, wq2 = _split2(w_q)
    kf = keys.reshape(-1, PEER_NKEYS, keys.shape[-1])
    k1, k2 = _split2(kf)
    hb, scores_t = _peer_q(x, gain, sc, sh, wq1, wq2, k1, k2, seq)
    e1, nrow, e2, rank2 = _peer_topk(scores_t)
    rows = PEER_PB * PEER_NKEYS
    ne, d = u_tab.shape
    u_blk = u_tab.astype(BF16).reshape(ne // rows, rows, d)
    v_blk = v_tab.astype(BF16).reshape(ne // rows, rows, d)
    return _peer_dense(hb, u_blk, v_blk, e1, nrow, e2, rank2, x, g2, seq)


def kernel(x, c, w_ada, b_ada, norm_g, hg_w_in, hg_lb, hg_norm_g, hg_w_out, lru_w_in, lru_conv_w,
           lru_conv_b, lru_w_a, lru_b_a, lru_w_x, lru_b_x, lru_lam, lru_w_out, peer_w_q, peer_keys,
           peer_u, peer_v, final_g):
    batch, seq, d = x.shape
    depth = w_ada.shape[0]
    n_mixers = 2
    mod = _ada_mod(c, w_ada, b_ada)
    xt = x.reshape(batch * seq, d)
    for i in range(depth):
        parts = [mod[i, :, k * d:(k + 1) * d].reshape(batch, 1, d) for k in range(6)]
        sh1, sc1, g1, sh2, sc2, g2 = parts
        j = i // n_mixers
        if i % n_mixers == 0:
            proj = _norm_mod_matmul(xt, norm_g[i, 0], sc1, sh1, hg_w_in[j].astype(BF16), seq,
                                    head_major=True)
            o_fw, o_bw = _hgrn_scan(proj, hg_lb, j, batch, seq)
            xt = _hgrn_out(o_fw, o_bw, proj, hg_norm_g[j], hg_w_out[j].astype(BF16), xt, g1, seq)
        else:
            proj = _norm_mod_matmul(xt, norm_g[i, 0], sc1, sh1, lru_w_in[j].astype(BF16), seq,
                                    head_major=False)
            hs = _lru_scan(proj, lru_conv_w[j], lru_conv_b[j], lru_w_a[j], lru_b_a[j], lru_w_x[j],
                           lru_b_x[j], lru_lam[j], batch, seq)
            xt = _lru_out(hs, proj, lru_w_out[j].astype(BF16), xt, g1, seq)
        xt = _peer_layer(xt, norm_g[i, 1], sc2, sh2, g2, peer_w_q[i], peer_keys[i], peer_u[i],
                         peer_v[i], seq)
    return _final_norm(xt, final_g).reshape(batch, seq, d)
```

```python
import functools

import numpy as np
import jax
import jax.numpy as jnp
from jax import lax
from jax.experimental import pallas as pl
from jax.experimental.pallas import tpu as pltpu

F32 = jnp.float32
BF16 = jnp.bfloat16

SUBLANES = 8
LANES = 128
VMEM_LIMIT = 56 * 1024 * 1024

EPS = 1e-6
HG_HEADS = 8
HG_CHUNK = 128
LRU_BLOCKS = 4
LRU_C = 8.0
CONV_W = 4
CONV_LEFT = 2
PEER_HEADS = 8
PEER_NKEYS = 128
PEER_TOPK = 16
HG_LEVELS = 7
NEG_INF = float("-inf")
LOG2_E = 1.4426950408889634


def _cparams(sem):
    return pltpu.CompilerParams(dimension_semantics=sem, vmem_limit_bytes=VMEM_LIMIT)


def _split3(x):
    x1 = x.astype(BF16)
    r1 = x - x1.astype(F32)
    x2 = r1.astype(BF16)
    x3 = (r1 - x2.astype(F32)).astype(BF16)
    return x1, x2, x3


def _split2(x):
    x1 = x.astype(BF16)
    x2 = (x - x1.astype(F32)).astype(BF16)
    return x1, x2


def _dot(a, b):
    return jnp.dot(a, b, preferred_element_type=F32)


def _dot_nt(a, b):
    return lax.dot_general(a, b, (((1,), (1,)), ((), ())), preferred_element_type=F32)


def _dot_tn(a, b):
    return lax.dot_general(a, b, (((0,), (0,)), ((), ())), preferred_element_type=F32)


def _norm_mod(x, gain, sc, sh):
    inv = lax.rsqrt(jnp.mean(x * x, axis=-1, keepdims=True) + EPS)
    return (x * inv) * gain * (1.0 + sc) + sh


def _ada_kernel(c_ref, w_ref, b_ref, o_ref):
    c = c_ref[...]
    cond = c * jax.nn.sigmoid(c)
    c1, c2, c3 = _split3(cond)
    w1, w2, w3 = _split3(w_ref[0])
    acc = _dot(c1, w1) + _dot(c1, w2) + _dot(c2, w1)
    acc = acc + _dot(c1, w3) + _dot(c2, w2) + _dot(c3, w1)
    o_ref[0] = acc + b_ref[0]


def _ada_mod(c, w_ada, b_ada):
    depth, d, n = w_ada.shape
    b = c.shape[0]
    cp = jnp.zeros((SUBLANES, d), F32).at[:b].set(c)
    tn = 1536
    out = pl.pallas_call(
        _ada_kernel,
        grid=(depth, n // tn),
        in_specs=[pl.BlockSpec((SUBLANES, d), lambda i, j: (0, 0)),
                  pl.BlockSpec((1, d, tn), lambda i, j: (i, 0, j)),
                  pl.BlockSpec((1, 1, tn), lambda i, j: (i, 0, j))],
        out_specs=pl.BlockSpec((1, SUBLANES, tn), lambda i, j: (i, 0, j)),
        out_shape=jax.ShapeDtypeStruct((depth, SUBLANES, n), F32),
        compiler_params=_cparams(("arbitrary", "arbitrary")),
        name="ada_mod",
    )(cp, w_ada, b_ada.reshape(depth, 1, n))
    return out[:, :b]


def _nmm_kernel(x_ref, g_ref, sc_ref, sh_ref, w_ref, o_ref, h_scr, *, head_major):
    @pl.when(pl.program_id(1) == 0)
    def _():
        h_scr[...] = _norm_mod(x_ref[...], g_ref[...], sc_ref[0], sh_ref[0]).astype(BF16)

    res = _dot(h_scr[...], w_ref[...])
    if head_major:
        for k in range(o_ref.shape[0]):
            o_ref[k] = res[:, k * LANES:(k + 1) * LANES]
    else:
        o_ref[...] = res


def _norm_mod_matmul(x, gain, sc, sh, w_bf16, seq, *, head_major, tm=1024, tn=1024):
    t, d = x.shape
    n = w_bf16.shape[1]
    tpb = seq // tm
    if head_major:
        out_shape = jax.ShapeDtypeStruct((n // LANES, t, LANES), F32)
        out_spec = pl.BlockSpec((tn // LANES, tm, LANES), lambda i, j: (j, i, 0))
    else:
        out_shape = jax.ShapeDtypeStruct((t, n), F32)
        out_spec = pl.BlockSpec((tm, tn), lambda i, j: (i, j))
    return pl.pallas_call(
        functools.partial(_nmm_kernel, head_major=head_major),
        grid=(t // tm, n // tn),
        in_specs=[pl.BlockSpec((tm, d), lambda i, j: (i, 0)),
                  pl.BlockSpec((1, d), lambda i, j: (0, 0)),
                  pl.BlockSpec((1, 1, d), lambda i, j: (i // tpb, 0, 0)),
                  pl.BlockSpec((1, 1, d), lambda i, j: (i // tpb, 0, 0)),
                  pl.BlockSpec((d, tn), lambda i, j: (0, j))],
        out_specs=out_spec,
        out_shape=out_shape,
        scratch_shapes=[pltpu.VMEM((tm, d), BF16)],
        compiler_params=_cparams(("arbitrary", "arbitrary")),
        name="norm_mod_matmul",
    )(x, gain.reshape(1, d), sc, sh, w_bf16)


def _hgrn_sum_mats(reverse):
    c = HG_CHUNK
    mats = []
    for lvl in range(1, HG_LEVELS + 1):
        m = 1 << lvl
        half = m >> 1
        a = np.zeros((c, c), np.float32)
        for t in range(c):
            mid = (t // m) * m + half
            if not reverse:
                if t >= mid:
                    a[t, mid:t + 1] = 1.0
                else:
                    a[t, t + 1:mid] = 1.0
            else:
                if t < mid:
                    a[t, t:mid] = 1.0
                else:
                    a[t, mid:t] = 1.0
        mats.append(a)
    ones = np.ones((c, c), np.float32)
    if not reverse:
        mats += [np.tril(ones), np.triu(ones, 1)]
    else:
        mats += [np.triu(ones), np.tril(ones, -1)]
    return np.concatenate(mats, axis=0)


def _hgrn_chain(q, flog, v, lb, a_ref, st_ref, d, h, reverse):
    c = HG_CHUNK
    f = lb + (1.0 - lb) * jax.nn.sigmoid(flog)
    g = jnp.log(f) * LOG2_E
    k = 1.0 - f
    g1, g2 = _split2(g)
    e2 = _dot(a_ref[...], jnp.concatenate([g1, g2], axis=1))
    e = e2[:, :LANES] + e2[:, LANES:]
    xdec = jnp.exp2(e)
    rows = lax.broadcasted_iota(jnp.int32, (c, c), 0)
    cols = lax.broadcasted_iota(jnp.int32, (c, c), 1)
    attn = jnp.where(rows == cols, jnp.sum(q * k, axis=-1, keepdims=True), 0.0)
    for lvl in range(1, HG_LEVELS + 1):
        xl = xdec[(lvl - 1) * c:lvl * c]
        row_q = (((rows >> (lvl - 1)) & 1) == 1) != reverse
        col_k = (((cols >> (lvl - 1)) & 1) == 0) != reverse
        mixed = (jnp.where(row_q, q, k) * xl).astype(BF16)
        p = _dot_nt(mixed, mixed)
        pick = jnp.logical_and((rows >> lvl) == (cols >> lvl), jnp.logical_and(row_q, col_k))
        attn = jnp.where(pick, p, attn)
    eq = e[HG_LEVELS * c:(HG_LEVELS + 1) * c]
    qd = (q * xdec[HG_LEVELS * c:(HG_LEVELS + 1) * c]).astype(BF16)
    kd = (k * xdec[(HG_LEVELS + 1) * c:(HG_LEVELS + 2) * c]).astype(BF16)
    tot = eq[0:1] if reverse else eq[c - 1:c]
    st = st_ref[d, h]
    vb = v.astype(BF16)
    o = _dot(attn.astype(BF16), vb) + _dot_nt(qd, st.astype(BF16))
    st_ref[d, h] = st * jnp.exp2(tot) + _dot_tn(vb, kd)
    return o


def _hgrn_scan_kernel(lb_ref, afw_ref, abw_ref, qf_ref, ff_ref, vf_ref, qb_ref, fb_ref, vb_ref,
                      of_ref, ob_ref, st_ref, lbs_ref, *, layer):
    @pl.when(pl.program_id(1) == 0)
    def _():
        st_ref[...] = jnp.zeros_like(st_ref)
        lb = lb_ref[...]
        ex = jnp.exp(lb - jnp.max(lb, axis=0, keepdims=True))
        p = ex / jnp.sum(ex, axis=0, keepdims=True)
        acc = p[0]
        for jj in range(1, layer + 1):
            acc = acc + p[jj]
        lbs_ref[...] = acc - p[0]

    def body(h, carry):
        lb = lbs_ref[pl.ds(h, 1), :]
        of_ref[h] = _hgrn_chain(qf_ref[h], ff_ref[h], vf_ref[h], lb, afw_ref, st_ref, 0, h, False)
        ob_ref[h] = _hgrn_chain(qb_ref[h], fb_ref[h], vb_ref[h], lb, abw_ref, st_ref, 1, h, True)
        return carry

    lax.fori_loop(0, HG_HEADS, body, 0, unroll=8)


def _hgrn_scan(proj_hm, hg_lb, layer, batch, seq):
    hh = HG_HEADS
    t = proj_hm.shape[1]
    nc = seq // HG_CHUNK
    c = HG_CHUNK
    na = hg_lb.shape[0]
    afw = jnp.asarray(_hgrn_sum_mats(False), BF16)
    abw = jnp.asarray(_hgrn_sum_mats(True), BF16)
    nrow = afw.shape[0]

    def fwd(part):
        return pl.BlockSpec((hh, c, LANES), lambda b, cc: (part, b * nc + cc, 0))

    def bwd(part):
        return pl.BlockSpec((hh, c, LANES), lambda b, cc: (part, b * nc + nc - 1 - cc, 0))

    out_sd = jax.ShapeDtypeStruct((hh, t, LANES), F32)
    return pl.pallas_call(
        functools.partial(_hgrn_scan_kernel, layer=layer),
        grid=(batch, nc),
        in_specs=[pl.BlockSpec((na, hh, LANES), lambda b, cc: (0, 0, 0)),
                  pl.BlockSpec((nrow, c), lambda b, cc: (0, 0)),
                  pl.BlockSpec((nrow, c), lambda b, cc: (0, 0)),
                  fwd(0), fwd(1), fwd(3), bwd(0), bwd(2), bwd(3)],
        out_specs=[pl.BlockSpec((hh, c, LANES), lambda b, cc: (0, b * nc + cc, 0)),
                   pl.BlockSpec((hh, c, LANES), lambda b, cc: (0, b * nc + nc - 1 - cc, 0))],
        out_shape=[out_sd, out_sd],
        scratch_shapes=[pltpu.VMEM((2, hh, LANES, LANES), F32), pltpu.VMEM((hh, LANES), F32)],
        compiler_params=_cparams(("arbitrary", "arbitrary")),
        name="hgrn_scan",
    )(hg_lb.reshape(na, hh, LANES), afw, abw, proj_hm, proj_hm, proj_hm, proj_hm, proj_hm, proj_hm)


def _hgrn_out_kernel(of_ref, ob_ref, gg_ref, ng_ref, w_ref, x_ref, g1_ref, o_ref):
    parts = []
    for h in range(HG_HEADS):
        o = of_ref[h] + ob_ref[h]
        inv = lax.rsqrt(jnp.mean(o * o, axis=-1, keepdims=True) + EPS)
        gg = gg_ref[h]
        parts.append(((o * inv) * ng_ref[pl.ds(h, 1), :] * (gg * jax.nn.sigmoid(gg))).astype(BF16))
    y = _dot(jnp.concatenate(parts, axis=-1), w_ref[...])
    o_ref[...] = x_ref[...] + g1_ref[0] * y


def _hgrn_out(o_fw, o_bw, proj_hm, norm_g, w_out_bf16, x, g1, seq, tm=512):
    hh = HG_HEADS
    t, d = x.shape
    tpb = seq // tm
    return pl.pallas_call(
        _hgrn_out_kernel,
        grid=(t // tm,),
        in_specs=[pl.BlockSpec((hh, tm, LANES), lambda i: (0, i, 0)),
                  pl.BlockSpec((hh, tm, LANES), lambda i: (0, i, 0)),
                  pl.BlockSpec((hh, tm, LANES), lambda i: (4, i, 0)),
                  pl.BlockSpec((hh, LANES), lambda i: (0, 0)),
                  pl.BlockSpec((d, d), lambda i: (0, 0)),
                  pl.BlockSpec((tm, d), lambda i: (i, 0)),
                  pl.BlockSpec((1, 1, d), lambda i: (i // tpb, 0, 0))],
        out_specs=pl.BlockSpec((tm, d), lambda i: (i, 0)),
        out_shape=jax.ShapeDtypeStruct((t, d), F32),
        compiler_params=_cparams(("arbitrary",)),
        name="hgrn_out",
    )(o_fw, o_bw, proj_hm, norm_g.reshape(hh, LANES), w_out_bf16, x, g1)


def _lru_group_scan(a, x, carry, rows, reverse):
    for s in (1, 2, 4):
        if not reverse:
            keep = rows >= s
            a_sh = jnp.where(keep, pltpu.roll(a, s, 0), 1.0)
            x_sh = jnp.where(keep, pltpu.roll(x, s, 0), 0.0)
        else:
            keep = rows < SUBLANES - s
            a_sh = jnp.where(keep, pltpu.roll(a, SUBLANES - s, 0), 1.0)
            x_sh = jnp.where(keep, pltpu.roll(x, SUBLANES - s, 0), 0.0)
        x = x + a * x_sh
        a = a * a_sh
    hs = x + a * carry
    new_carry = hs[0:1] if reverse else hs[SUBLANES - 1:SUBLANES]
    return hs, new_carry


def _lru_kernel(xc_ref, xp_ref, xn_ref, cw_ref, cb_ref, wa_ref, ba_ref, wx_ref, bx_ref, lam_ref,
                o_ref, a_scr, b_scr, carry_scr, *, tm, nchunk):
    d = pl.program_id(0)
    cc = pl.program_id(2)
    chunk = jnp.where(d == 0, cc, nchunk - 1 - cc)

    @pl.when(cc == 0)
    def _():
        carry_scr[...] = jnp.zeros_like(carry_scr)

    xcur = xc_ref[...]
    prev = jnp.where(chunk == 0, 0.0, xp_ref[...])
    nxt = jnp.where(chunk == nchunk - 1, 0.0, xn_ref[...])
    ext = jnp.concatenate([prev, xcur, nxt], axis=0)
    cw = cw_ref[...]
    xc = cb_ref[...]
    for j in range(CONV_W):
        off = SUBLANES - CONV_LEFT + j
        xc = xc + ext[off:off + tm] * cw[j:j + 1]

    xcb = xc.astype(BF16)
    bw = xcb.shape[1] // LRU_BLOCKS
    ra, rx = [], []
    for n in range(LRU_BLOCKS):
        blk = xcb[:, n * bw:(n + 1) * bw]
        ra.append(_dot(blk, wa_ref[0, n]))
        rx.append(_dot(blk, wx_ref[0, n]))
    r = jax.nn.sigmoid(jnp.concatenate(ra, axis=-1) + ba_ref[0])
    ig = jax.nn.sigmoid(jnp.concatenate(rx, axis=-1) + bx_ref[0])
    lam = lam_ref[0]
    softplus_neg = jnp.maximum(-lam, 0.0) + jnp.log(1.0 + jnp.exp(-jnp.abs(lam)))
    log_a = -LRU_C * r * softplus_neg
    a = jnp.exp(log_a)
    a_scr[...] = a
    b_scr[...] = jnp.sqrt(-jnp.tanh(log_a) * (a * a + 1.0)) * ig * xc

    ngroups = tm // SUBLANES
    rows = lax.broadcasted_iota(jnp.int32, (SUBLANES, xc.shape[1]), 0)

    def run(reverse):
        def body(i, carry):
            gi = (ngroups - 1 - i) if reverse else i
            sl = pl.ds(pl.multiple_of(gi * SUBLANES, SUBLANES), SUBLANES)
            hs, carry = _lru_group_scan(a_scr[sl, :], b_scr[sl, :], carry, rows, reverse)
            o_ref[0, sl, :] = hs
            return carry
        carry_scr[...] = lax.fori_loop(0, ngroups, body, carry_scr[...])

    @pl.when(d == 0)
    def _():
        run(False)

    @pl.when(d == 1)
    def _():
        run(True)


def _lru_scan(proj, conv_w, conv_b, w_a, b_a, w_x, b_x, lam, batch, seq, tm=256):
    t = proj.shape[0]
    w = proj.shape[1] // 2
    nchunk = seq // tm
    hb = tm // SUBLANES
    nhalo = t // SUBLANES

    def pos(d, b, cc):
        return b * nchunk + jnp.where(d == 0, cc, nchunk - 1 - cc)

    return pl.pallas_call(
        functools.partial(_lru_kernel, tm=tm, nchunk=nchunk),
        grid=(2, batch, nchunk),
        in_specs=[pl.BlockSpec((tm, w), lambda d, b, cc: (pos(d, b, cc), 0)),
                  pl.BlockSpec((SUBLANES, w), lambda d, b, cc: (jnp.maximum(pos(d, b, cc) * hb - 1, 0), 0)),
                  pl.BlockSpec((SUBLANES, w),
                               lambda d, b, cc: (jnp.minimum((pos(d, b, cc) + 1) * hb, nhalo - 1), 0)),
                  pl.BlockSpec((CONV_W, w), lambda d, b, cc: (0, 0)),
                  pl.BlockSpec((1, w), lambda d, b, cc: (0, 0)),
                  pl.BlockSpec((1, LRU_BLOCKS, w // LRU_BLOCKS, w // LRU_BLOCKS), lambda d, b, cc: (d, 0, 0, 0)),
                  pl.BlockSpec((1, 1, w), lambda d, b, cc: (d, 0, 0)),
                  pl.BlockSpec((1, LRU_BLOCKS, w // LRU_BLOCKS, w // LRU_BLOCKS), lambda d, b, cc: (d, 0, 0, 0)),
                  pl.BlockSpec((1, 1, w), lambda d, b, cc: (d, 0, 0)),
                  pl.BlockSpec((1, 1, w), lambda d, b, cc: (d, 0, 0))],
        out_specs=pl.BlockSpec((1, tm, w), lambda d, b, cc: (d, pos(d, b, cc), 0)),
        out_shape=jax.ShapeDtypeStruct((2, t, w), F32),
        scratch_shapes=[pltpu.VMEM((tm, w), F32), pltpu.VMEM((tm, w), F32), pltpu.VMEM((1, w), F32)],
        compiler_params=_cparams(("arbitrary", "arbitrary", "arbitrary")),
        name="lru_scan",
    )(proj, proj, proj, conv_w, conv_b.reshape(1, w), w_a.astype(BF16), b_a.reshape(2, 1, w),
      w_x.astype(BF16), b_x.reshape(2, 1, w), lam.reshape(2, 1, w))


def _lru_out_kernel(hs_ref, yb_ref, w_ref, x_ref, g1_ref, o_ref):
    u = (hs_ref[0] + hs_ref[1]) * jax.nn.gelu(yb_ref[...])
    o_ref[...] = x_ref[...] + g1_ref[0] * _dot(u.astype(BF16), w_ref[...])


def _lru_out(hs, proj, w_out_bf16, x, g1, seq, tm=512):
    t, d = x.shape
    w = hs.shape[2]
    tpb = seq // tm
    return pl.pallas_call(
        _lru_out_kernel,
        grid=(t // tm,),
        in_specs=[pl.BlockSpec((2, tm, w), lambda i: (0, i, 0)),
                  pl.BlockSpec((tm, w), lambda i: (i, 1)),
                  pl.BlockSpec((w, d), lambda i: (0, 0)),
                  pl.BlockSpec((tm, d), lambda i: (i, 0)),
                  pl.BlockSpec((1, 1, d), lambda i: (i // tpb, 0, 0))],
        out_specs=pl.BlockSpec((tm, d), lambda i: (i, 0)),
        out_shape=jax.ShapeDtypeStruct((t, d), F32),
        compiler_params=_cparams(("arbitrary",)),
        name="lru_out",
    )(hs, proj, w_out_bf16, x, g1)


def _peer_q_kernel(x_ref, g_ref, sc_ref, sh_ref, w1_ref, w2_ref, k1_ref, k2_ref, hb_ref, sc_out_ref):
    h = _norm_mod(x_ref[...], g_ref[...], sc_ref[0], sh_ref[0])
    h1, h2 = _split2(h)
    hb_ref[...] = h.T.astype(BF16)
    q = _dot(h1, w1_ref[...]) + _dot(h1, w2_ref[...]) + _dot(h2, w1_ref[...])
    nhp = k1_ref.shape[0]
    nch = sc_out_ref.shape[1]
    for hp in range(nhp):
        qa, qb = _split2(q[:, hp * LANES:(hp + 1) * LANES])
        s = _dot_nt(k1_ref[hp], qa) + _dot_nt(k1_ref[hp], qb) + _dot_nt(k2_ref[hp], qa)
        for ch in range(nch):
            sc_out_ref[hp, ch] = s[:, ch * LANES:(ch + 1) * LANES]


def _peer_q(x, gain, sc, sh, wq1, wq2, keys1, keys2, seq, tm=256):
    t, d = x.shape
    nq = wq1.shape[1]
    nhp = keys1.shape[0]
    tpb = seq // tm
    nch = tm // LANES
    return pl.pallas_call(
        _peer_q_kernel,
        grid=(t // tm,),
        in_specs=[pl.BlockSpec((tm, d), lambda i: (i, 0)),
                  pl.BlockSpec((1, d), lambda i: (0, 0)),
                  pl.BlockSpec((1, 1, d), lambda i: (i // tpb, 0, 0)),
                  pl.BlockSpec((1, 1, d), lambda i: (i // tpb, 0, 0)),
                  pl.BlockSpec((d, nq), lambda i: (0, 0)),
                  pl.BlockSpec((d, nq), lambda i: (0, 0)),
                  pl.BlockSpec((nhp, PEER_NKEYS, LANES), lambda i: (0, 0, 0)),
                  pl.BlockSpec((nhp, PEER_NKEYS, LANES), lambda i: (0, 0, 0))],
        out_specs=[pl.BlockSpec((d, tm), lambda i: (0, i)),
                   pl.BlockSpec((nhp, nch, PEER_NKEYS, LANES), lambda i: (0, i, 0, 0))],
        out_shape=[jax.ShapeDtypeStruct((d, t), BF16),
                   jax.ShapeDtypeStruct((nhp, t // LANES, PEER_NKEYS, LANES), F32)],
        compiler_params=_cparams(("arbitrary",)),
        name="peer_q",
    )(x, gain.reshape(1, d), sc, sh, wq1, wq2, keys1, keys2)


def _top16(s, rows, exact):
    work = s
    rank = jnp.full(s.shape, PEER_TOPK, jnp.int32)
    vals = []
    for kk in range(PEER_TOPK):
        m = jnp.max(work, axis=0, keepdims=True)
        if exact:
            idx = jnp.min(jnp.where(work == m, rows, PEER_NKEYS), axis=0, keepdims=True)
            sel = rows == idx
        else:
            sel = work == m
        rank = jnp.where(sel, kk, rank)
        work = jnp.where(sel, NEG_INF, work)
        vals.append(m)
    return rank, jnp.concatenate(vals, axis=0)


def _col_count(mask):
    return jnp.sum(jnp.where(mask, 1.0, 0.0), axis=0, keepdims=True)


def _any_lane_differs(count, want):
    return jnp.max(jnp.where(count == want, 0.0, 1.0)) > 0.5


def _cand16(cand, flat, r16, mtop, exact):
    cnt = jnp.zeros((PEER_TOPK, cand.shape[1]), F32)
    z = jnp.zeros((1, cand.shape[1]), F32)
    for _ in range(PEER_TOPK):
        m = jnp.max(cand, axis=0, keepdims=True)
        if exact:
            idx = jnp.min(jnp.where(cand == m, flat, PEER_TOPK * PEER_TOPK), axis=0, keepdims=True)
            cand = jnp.where(flat == idx, NEG_INF, cand)
            cnt = cnt + jnp.where(r16 == (idx >> 4), 1.0, 0.0)
        else:
            cand = jnp.where(cand == m, NEG_INF, cand)
        z = z + jnp.exp(m - mtop)
    if not exact:
        gone = cand == NEG_INF
        cnt = jnp.concatenate(
            [_col_count(gone[0:PEER_TOPK])]
            + [_col_count(gone[PEER_TOPK + SUBLANES * (k1 - 1):PEER_TOPK + SUBLANES * k1]) for k1 in range(1, 8)]
            + [jnp.where(gone[PEER_TOPK + 7 * SUBLANES:], 1.0, 0.0)], axis=0)
    return cnt, z


def _bf16_pair(x):
    u = pltpu.bitcast(x.astype(BF16).astype(F32), jnp.uint32)
    return u | (u >> 16)


PEER_TOPK_GROUP = 4


def _peer_topk_kernel(s_ref, e1_ref, n_ref, e2_ref, r2_ref):
    nch = s_ref.shape[1]
    grp = PEER_TOPK_GROUP
    width = grp * LANES
    rows = lax.broadcasted_iota(jnp.int32, (PEER_NKEYS, width), 0)
    r16 = lax.broadcasted_iota(jnp.int32, (PEER_TOPK, width), 0)
    r8 = lax.broadcasted_iota(jnp.int32, (SUBLANES, width), 0)
    flat = jnp.concatenate([r16] + [r8 + PEER_TOPK * k1 for k1 in range(1, 8)]
                           + [(r8 + 8) * PEER_TOPK], axis=0)
    want = float(PEER_TOPK)

    def body(i, carry):
        chunks = [i * grp + k for k in range(grp)]
        s1 = jnp.concatenate([s_ref[0, ch] for ch in chunks], axis=1)
        s2 = jnp.concatenate([s_ref[1, ch] for ch in chunks], axis=1)
        rank1, v1 = _top16(s1, rows, False)
        rank2, v2 = _top16(s2, rows, False)
        tied = jnp.logical_or(_any_lane_differs(_col_count(rank1 < PEER_TOPK), want),
                              _any_lane_differs(_col_count(rank2 < PEER_TOPK), want))
        rank1, v1, rank2, v2 = lax.cond(
            tied, lambda: _top16(s1, rows, True) + _top16(s2, rows, True), lambda: (rank1, v1, rank2, v2))
        cand = jnp.concatenate([v1[0:1] + v2] + [v1[k1:k1 + 1] + v2[0:8] for k1 in range(1, 8)]
                               + [v1[8:16] + v2[0:1]], axis=0)
        mtop = v1[0:1] + v2[0:1]
        cnt, z = _cand16(cand, flat, r16, mtop, False)
        tied = _any_lane_differs(jnp.sum(cnt, axis=0, keepdims=True), want)
        cnt, z = lax.cond(tied, lambda: _cand16(cand, flat, r16, mtop, True), lambda: (cnt, z))
        nrow = jnp.zeros((PEER_NKEYS, width), F32)
        for kk in range(PEER_TOPK):
            nrow = jnp.where(rank1 == kk, cnt[kk:kk + 1], nrow)
        e1 = _bf16_pair(jnp.where(rank1 < PEER_TOPK, jnp.exp(s1 - v1[0:1]) / z, 0.0))
        nrow = _bf16_pair(nrow)
        e2 = jnp.where(rank2 < PEER_TOPK, jnp.exp(s2 - v2[0:1]), 0.0)
        e2 = pltpu.bitcast(e2.astype(BF16), jnp.uint32)
        r2 = pltpu.bitcast(rank2.astype(F32).astype(BF16), jnp.uint32)
        for k, ch in enumerate(chunks):
            cols = slice(k * LANES, (k + 1) * LANES)
            e1_ref[0, ch] = e1[:, cols]
            n_ref[0, ch] = nrow[:, cols]
            e2_ref[0, ch] = e2[:, cols]
            r2_ref[0, ch] = r2[:, cols]
        return carry

    lax.fori_loop(0, nch // grp, body, 0)


def _peer_topk(scores_t, nch=4):
    nhp, ntc, nk, _ = scores_t.shape
    hh = nhp // 2
    spec = pl.BlockSpec((1, nch, nk, LANES), lambda i, h: (h, i, 0, 0))
    spec_p = pl.BlockSpec((1, nch, nk // 2, LANES), lambda i, h: (h, i, 0, 0))
    sd_u = jax.ShapeDtypeStruct((hh, ntc, nk, LANES), jnp.uint32)
    sd_b = jax.ShapeDtypeStruct((hh, ntc, nk // 2, LANES), jnp.uint32)
    return pl.pallas_call(
        _peer_topk_kernel,
        grid=(ntc // nch, hh),
        in_specs=[pl.BlockSpec((2, nch, nk, LANES), lambda i, h: (h, i, 0, 0))],
        out_specs=[spec, spec, spec_p, spec_p],
        out_shape=[sd_u, sd_u, sd_b, sd_b],
        compiler_params=_cparams(("arbitrary", "arbitrary")),
        name="peer_topk",
    )(scores_t)


def _dup_rows(row_u32, nrows):
    return pltpu.bitcast(jnp.broadcast_to(row_u32, (nrows // 2, row_u32.shape[1])), BF16)


PEER_PB = 2


GELU_C1 = 0.7978845608028654
GELU_C2 = GELU_C1 * 0.044715


def _gelu_times(x, w):
    hw = (0.5 * x) * w
    return hw + hw * jnp.tanh(x * (GELU_C1 + GELU_C2 * (x * x)))


def _peer_dense_kernel(hbt_ref, u_ref, v_ref, e1_ref, n_ref, e2_ref, r2_ref, x_ref, g2_ref,
                       o_ref, acc_ref, act0, act1, ab_ref):
    j = pl.program_id(1)

    @pl.when(j == 0)
    def _():
        acc_ref[...] = jnp.zeros_like(acc_ref)

    nch = e2_ref.shape[1]
    nb = u_ref.shape[0]
    acts = (act0, act1)
    zero = jnp.zeros((), BF16)
    blk_rows = PEER_PB * PEER_NKEYS

    def act_mm(p):
        acts[p % 2][...] = _dot(pltpu.bitcast(u_ref[p], BF16), hbt_ref[...])

    def gate(p):
        for ch in range(nch):
            for a2 in range(PEER_PB):
                a = p * PEER_PB + a2
                w = jnp.zeros((PEER_NKEYS, LANES), BF16)
                for h in range(PEER_HEADS):
                    keep = pltpu.bitcast(r2_ref[h, ch], BF16) < _dup_rows(n_ref[h, ch, a:a + 1, :], PEER_NKEYS)
                    w = w + (jnp.where(keep, pltpu.bitcast(e2_ref[h, ch], BF16), zero)
                             * _dup_rows(e1_ref[h, ch, a:a + 1, :], PEER_NKEYS))
                rows = slice(a2 * PEER_NKEYS, (a2 + 1) * PEER_NKEYS)
                cols = slice(ch * LANES, (ch + 1) * LANES)
                out_rows = slice(p * blk_rows + a2 * PEER_NKEYS, p * blk_rows + (a2 + 1) * PEER_NKEYS)
                ab_ref[out_rows, cols] = _gelu_times(acts[p % 2][rows, cols].astype(BF16), w)

    act_mm(0)
    for p in range(nb):
        if p + 1 < nb:
            act_mm(p + 1)
        gate(p)
    acc_ref[...] += _dot_tn(ab_ref[...], pltpu.bitcast(v_ref[...], BF16))

    @pl.when(j == pl.num_programs(1) - 1)
    def _():
        o_ref[...] = x_ref[...] + g2_ref[0] * acc_ref[...]


def _pack_rows_kernel(x_ref, o_ref):
    o_ref[...] = pltpu.bitcast(x_ref[...].astype(BF16), jnp.uint32)


def _pack_rows(tab, tr=1024):
    n, d = tab.shape
    return pl.pallas_call(
        _pack_rows_kernel,
        grid=(n // tr,),
        in_specs=[pl.BlockSpec((tr, d), lambda i: (i, 0))],
        out_specs=pl.BlockSpec((tr // 2, d), lambda i: (i, 0)),
        out_shape=jax.ShapeDtypeStruct((n // 2, d), jnp.uint32),
        compiler_params=_cparams(("arbitrary",)),
        name="pack_rows",
    )(tab)


def _peer_dense(hbt, u_pk, v_pk, e1, nrow, e2, rank2, x, g2, seq, tm=512, na=16):
    t, d = x.shape
    ne = 2 * u_pk.shape[0]
    rows = PEER_PB * PEER_NKEYS
    nb = na // PEER_PB
    te = nb * rows
    nch = tm // LANES
    tpb = seq // tm
    hh = PEER_HEADS
    small = pl.BlockSpec((hh, nch, na, LANES), lambda i, j: (0, i, j, 0))
    big = pl.BlockSpec((hh, nch, PEER_NKEYS // 2, LANES), lambda i, j: (0, i, 0, 0))
    return pl.pallas_call(
        _peer_dense_kernel,
        grid=(t // tm, ne // te),
        in_specs=[pl.BlockSpec((d, tm), lambda i, j: (0, i)),
                  pl.BlockSpec((nb, rows // 2, d), lambda i, j: (j, 0, 0)),
                  pl.BlockSpec((te // 2, d), lambda i, j: (j, 0)),
                  small, small, big, big,
                  pl.BlockSpec((tm, d), lambda i, j: (i, 0)),
                  pl.BlockSpec((1, 1, d), lambda i, j: (i // tpb, 0, 0))],
        out_specs=pl.BlockSpec((tm, d), lambda i, j: (i, 0)),
        out_shape=jax.ShapeDtypeStruct((t, d), F32),
        scratch_shapes=[pltpu.VMEM((tm, d), F32), pltpu.VMEM((rows, tm), F32), pltpu.VMEM((rows, tm), F32),
                        pltpu.VMEM((te, tm), BF16)],
        compiler_params=_cparams(("arbitrary", "arbitrary")),
        name="peer_dense",
    )(hbt, u_pk.reshape(ne // rows, rows // 2, d), v_pk, e1, nrow, e2, rank2, x, g2)


def _final_norm_kernel(x_ref, g_ref, o_ref):
    x = x_ref[...]
    inv = lax.rsqrt(jnp.mean(x * x, axis=-1, keepdims=True) + EPS)
    o_ref[...] = (x * inv) * g_ref[...]


def _final_norm(x, gain, tm=1024):
    t, d = x.shape
    return pl.pallas_call(
        _final_norm_kernel,
        grid=(t // tm,),
        in_specs=[pl.BlockSpec((tm, d), lambda i: (i, 0)), pl.BlockSpec((1, d), lambda i: (0, 0))],
        out_specs=pl.BlockSpec((tm, d), lambda i: (i, 0)),
        out_shape=jax.ShapeDtypeStruct((t, d), F32),
        compiler_params=_cparams(("arbitrary",)),
        name="final_norm",
    )(x, gain.reshape(1, d))


def _peer_layer(x, gain, sc, sh, g2, w_q, keys, u_tab, v_tab, seq):
    wq1, wq2 = _split2(w_q)
    kf = keys.reshape(-1, PEER_NKEYS, keys.shape[-1])
    k1, k2 = _split2(kf)
    hb, scores_t = _peer_q(x, gain, sc, sh, wq1, wq2, k1, k2, seq)
    e1, nrow, e2, rank2 = _peer_topk(scores_t)
    return _peer_dense(hb, _pack_rows(u_tab), _pack_rows(v_tab), e1, nrow, e2, rank2, x, g2, seq)


def kernel(x, c, w_ada, b_ada, norm_g, hg_w_in, hg_lb, hg_norm_g, hg_w_out, lru_w_in, lru_conv_w,
           lru_conv_b, lru_w_a, lru_b_a, lru_w_x, lru_b_x, lru_lam, lru_w_out, peer_w_q, peer_keys,
           peer_u, peer_v, final_g):
    batch, seq, d = x.shape
    depth = w_ada.shape[0]
    n_mixers = 2
    mod = _ada_mod(c, w_ada, b_ada)
    xt = x.reshape(batch * seq, d)
    for i in range(depth):
        parts = [mod[i, :, k * d:(k + 1) * d].reshape(batch, 1, d) for k in range(6)]
        sh1, sc1, g1, sh2, sc2, g2 = parts
        j = i // n_mixers
        if i % n_mixers == 0:
            proj = _norm_mod_matmul(xt, norm_g[i, 0], sc1, sh1, hg_w_in[j].astype(BF16), seq,
                                    head_major=True)
            o_fw, o_bw = _hgrn_scan(proj, hg_lb, j, batch, seq)
            xt = _hgrn_out(o_fw, o_bw, proj, hg_norm_g[j], hg_w_out[j].astype(BF16), xt, g1, seq)
        else:
            proj = _norm_mod_matmul(xt, norm_g[i, 0], sc1, sh1, lru_w_in[j].astype(BF16), seq,
                                    head_major=False)
            hs = _lru_scan(proj, lru_conv_w[j], lru_conv_b[j], lru_w_a[j], lru_b_a[j], lru_w_x[j],
                           lru_b_x[j], lru_lam[j], batch, seq)
            xt = _lru_out(hs, proj, lru_w_out[j].astype(BF16), xt, g1, seq)
        xt = _peer_layer(xt, norm_g[i, 1], sc2, sh2, g2, peer_w_q[i], peer_keys[i], peer_u[i],
                         peer_v[i], seq)
    return _final_norm(xt, final_g).reshape(batch, seq, d)
```

```python
import functools

import numpy as np
import jax
import jax.numpy as jnp
from jax import lax
from jax.experimental import pallas as pl
from jax.experimental.pallas import tpu as pltpu

F32 = jnp.float32
BF16 = jnp.bfloat16

SUBLANES = 8
LANES = 128
VMEM_LIMIT = 56 * 1024 * 1024

EPS = 1e-6
HG_HEADS = 8
HG_CHUNK = 128
LRU_BLOCKS = 4
LRU_C = 8.0
CONV_W = 4
CONV_LEFT = 2
PEER_HEADS = 8
PEER_NKEYS = 128
PEER_TOPK = 16
HG_LEVELS = 7
NEG_INF = float("-inf")
LOG2_E = 1.4426950408889634


def _cparams(sem):
    return pltpu.CompilerParams(dimension_semantics=sem, vmem_limit_bytes=VMEM_LIMIT)


def _split3(x):
    x1 = x.astype(BF16)
    r1 = x - x1.astype(F32)
    x2 = r1.astype(BF16)
    x3 = (r1 - x2.astype(F32)).astype(BF16)
    return x1, x2, x3


def _split2(x):
    x1 = x.astype(BF16)
    x2 = (x - x1.astype(F32)).astype(BF16)
    return x1, x2


def _dot(a, b):
    return jnp.dot(a, b, preferred_element_type=F32)


def _dot_nt(a, b):
    return lax.dot_general(a, b, (((1,), (1,)), ((), ())), preferred_element_type=F32)


def _dot_tn(a, b):
    return lax.dot_general(a, b, (((0,), (0,)), ((), ())), preferred_element_type=F32)


def _norm_mod(x, gain, sc, sh):
    inv = lax.rsqrt(jnp.mean(x * x, axis=-1, keepdims=True) + EPS)
    return (x * inv) * gain * (1.0 + sc) + sh


def _ada_kernel(c_ref, w_ref, b_ref, o_ref):
    c = c_ref[...]
    cond = c * jax.nn.sigmoid(c)
    c1, c2, c3 = _split3(cond)
    w1, w2, w3 = _split3(w_ref[0])
    acc = _dot(c1, w1) + _dot(c1, w2) + _dot(c2, w1)
    acc = acc + _dot(c1, w3) + _dot(c2, w2) + _dot(c3, w1)
    o_ref[0] = acc + b_ref[0]


def _ada_mod(c, w_ada, b_ada):
    depth, d, n = w_ada.shape
    b = c.shape[0]
    cp = jnp.zeros((SUBLANES, d), F32).at[:b].set(c)
    tn = 1536
    out = pl.pallas_call(
        _ada_kernel,
        grid=(depth, n // tn),
        in_specs=[pl.BlockSpec((SUBLANES, d), lambda i, j: (0, 0)),
                  pl.BlockSpec((1, d, tn), lambda i, j: (i, 0, j)),
                  pl.BlockSpec((1, 1, tn), lambda i, j: (i, 0, j))],
        out_specs=pl.BlockSpec((1, SUBLANES, tn), lambda i, j: (i, 0, j)),
        out_shape=jax.ShapeDtypeStruct((depth, SUBLANES, n), F32),
        compiler_params=_cparams(("arbitrary", "arbitrary")),
        name="ada_mod",
    )(cp, w_ada, b_ada.reshape(depth, 1, n))
    return out[:, :b]


def _nmm_kernel(x_ref, g_ref, sc_ref, sh_ref, w_ref, o_ref, h_scr, *, head_major):
    @pl.when(pl.program_id(1) == 0)
    def _():
        h_scr[...] = _norm_mod(x_ref[...], g_ref[...], sc_ref[0], sh_ref[0]).astype(BF16)

    res = _dot(h_scr[...], w_ref[...])
    if head_major:
        for k in range(o_ref.shape[0]):
            o_ref[k] = res[:, k * LANES:(k + 1) * LANES]
    else:
        o_ref[...] = res


def _norm_mod_matmul(x, gain, sc, sh, w_bf16, seq, *, head_major, tm=1024, tn=1024):
    t, d = x.shape
    n = w_bf16.shape[1]
    tpb = seq // tm
    if head_major:
        out_shape = jax.ShapeDtypeStruct((n // LANES, t, LANES), F32)
        out_spec = pl.BlockSpec((tn // LANES, tm, LANES), lambda i, j: (j, i, 0))
    else:
        out_shape = jax.ShapeDtypeStruct((t, n), F32)
        out_spec = pl.BlockSpec((tm, tn), lambda i, j: (i, j))
    return pl.pallas_call(
        functools.partial(_nmm_kernel, head_major=head_major),
        grid=(t // tm, n // tn),
        in_specs=[pl.BlockSpec((tm, d), lambda i, j: (i, 0)),
                  pl.BlockSpec((1, d), lambda i, j: (0, 0)),
                  pl.BlockSpec((1, 1, d), lambda i, j: (i // tpb, 0, 0)),
                  pl.BlockSpec((1, 1, d), lambda i, j: (i // tpb, 0, 0)),
                  pl.BlockSpec((d, tn), lambda i, j: (0, j))],
        out_specs=out_spec,
        out_shape=out_shape,
        scratch_shapes=[pltpu.VMEM((tm, d), BF16)],
        compiler_params=_cparams(("arbitrary", "arbitrary")),
        name="norm_mod_matmul",
    )(x, gain.reshape(1, d), sc, sh, w_bf16)


def _hgrn_sum_mats(reverse):
    c = HG_CHUNK
    mats = []
    for lvl in range(1, HG_LEVELS + 1):
        m = 1 << lvl
        half = m >> 1
        a = np.zeros((c, c), np.float32)
        for t in range(c):
            mid = (t // m) * m + half
            if not reverse:
                if t >= mid:
                    a[t, mid:t + 1] = 1.0
                else:
                    a[t, t + 1:mid] = 1.0
            else:
                if t < mid:
                    a[t, t:mid] = 1.0
                else:
                    a[t, mid:t] = 1.0
        mats.append(a)
    ones = np.ones((c, c), np.float32)
    if not reverse:
        mats += [np.tril(ones), np.triu(ones, 1)]
    else:
        mats += [np.triu(ones), np.tril(ones, -1)]
    return np.concatenate(mats, axis=0)


def _hgrn_chain(q, flog, v, lb, a_ref, st_ref, d, h, reverse):
    c = HG_CHUNK
    f = lb + (1.0 - lb) * jax.nn.sigmoid(flog)
    g = jnp.log(f) * LOG2_E
    k = 1.0 - f
    g1, g2 = _split2(g)
    e2 = _dot(a_ref[...], jnp.concatenate([g1, g2], axis=1))
    e = e2[:, :LANES] + e2[:, LANES:]
    xdec = jnp.exp2(e)
    rows = lax.broadcasted_iota(jnp.int32, (c, c), 0)
    cols = lax.broadcasted_iota(jnp.int32, (c, c), 1)
    attn = jnp.where(rows == cols, jnp.sum(q * k, axis=-1, keepdims=True), 0.0)
    for lvl in range(1, HG_LEVELS + 1):
        xl = xdec[(lvl - 1) * c:lvl * c]
        row_q = (((rows >> (lvl - 1)) & 1) == 1) != reverse
        col_k = (((cols >> (lvl - 1)) & 1) == 0) != reverse
        mixed = (jnp.where(row_q, q, k) * xl).astype(BF16)
        p = _dot_nt(mixed, mixed)
        pick = jnp.logical_and((rows >> lvl) == (cols >> lvl), jnp.logical_and(row_q, col_k))
        attn = jnp.where(pick, p, attn)
    eq = e[HG_LEVELS * c:(HG_LEVELS + 1) * c]
    qd = (q * xdec[HG_LEVELS * c:(HG_LEVELS + 1) * c]).astype(BF16)
    kd = (k * xdec[(HG_LEVELS + 1) * c:(HG_LEVELS + 2) * c]).astype(BF16)
    tot = eq[0:1] if reverse else eq[c - 1:c]
    st = st_ref[d, h]
    vb = v.astype(BF16)
    o = _dot(attn.astype(BF16), vb) + _dot_nt(qd, st.astype(BF16))
    st_ref[d, h] = st * jnp.exp2(tot) + _dot_tn(vb, kd)
    return o


def _hgrn_scan_kernel(lb_ref, afw_ref, abw_ref, qf_ref, ff_ref, vf_ref, qb_ref, fb_ref, vb_ref,
                      of_ref, ob_ref, st_ref, lbs_ref, *, layer):
    @pl.when(pl.program_id(1) == 0)
    def _():
        st_ref[...] = jnp.zeros_like(st_ref)
        lb = lb_ref[...]
        ex = jnp.exp(lb - jnp.max(lb, axis=0, keepdims=True))
        p = ex / jnp.sum(ex, axis=0, keepdims=True)
        acc = p[0]
        for jj in range(1, layer + 1):
            acc = acc + p[jj]
        lbs_ref[...] = acc - p[0]

    def body(h, carry):
        lb = lbs_ref[pl.ds(h, 1), :]
        of_ref[h] = _hgrn_chain(qf_ref[h], ff_ref[h], vf_ref[h], lb, afw_ref, st_ref, 0, h, False)
        ob_ref[h] = _hgrn_chain(qb_ref[h], fb_ref[h], vb_ref[h], lb, abw_ref, st_ref, 1, h, True)
        return carry

    lax.fori_loop(0, HG_HEADS, body, 0, unroll=8)


def _hgrn_scan(proj_hm, hg_lb, layer, batch, seq):
    hh = HG_HEADS
    t = proj_hm.shape[1]
    nc = seq // HG_CHUNK
    c = HG_CHUNK
    na = hg_lb.shape[0]
    afw = jnp.asarray(_hgrn_sum_mats(False), BF16)
    abw = jnp.asarray(_hgrn_sum_mats(True), BF16)
    nrow = afw.shape[0]

    def fwd(part):
        return pl.BlockSpec((hh, c, LANES), lambda b, cc: (part, b * nc + cc, 0))

    def bwd(part):
        return pl.BlockSpec((hh, c, LANES), lambda b, cc: (part, b * nc + nc - 1 - cc, 0))

    out_sd = jax.ShapeDtypeStruct((hh, t, LANES), F32)
    return pl.pallas_call(
        functools.partial(_hgrn_scan_kernel, layer=layer),
        grid=(batch, nc),
        in_specs=[pl.BlockSpec((na, hh, LANES), lambda b, cc: (0, 0, 0)),
                  pl.BlockSpec((nrow, c), lambda b, cc: (0, 0)),
                  pl.BlockSpec((nrow, c), lambda b, cc: (0, 0)),
                  fwd(0), fwd(1), fwd(3), bwd(0), bwd(2), bwd(3)],
        out_specs=[pl.BlockSpec((hh, c, LANES), lambda b, cc: (0, b * nc + cc, 0)),
                   pl.BlockSpec((hh, c, LANES), lambda b, cc: (0, b * nc + nc - 1 - cc, 0))],
        out_shape=[out_sd, out_sd],
        scratch_shapes=[pltpu.VMEM((2, hh, LANES, LANES), F32), pltpu.VMEM((hh, LANES), F32)],
        compiler_params=_cparams(("arbitrary", "arbitrary")),
        name="hgrn_scan",
    )(hg_lb.reshape(na, hh, LANES), afw, abw, proj_hm, proj_hm, proj_hm, proj_hm, proj_hm, proj_hm)


def _hgrn_out_kernel(of_ref, ob_ref, gg_ref, ng_ref, w_ref, x_ref, g1_ref, o_ref):
    parts = []
    for h in range(HG_HEADS):
        o = of_ref[h] + ob_ref[h]
        inv = lax.rsqrt(jnp.mean(o * o, axis=-1, keepdims=True) + EPS)
        gg = gg_ref[h]
        parts.append(((o * inv) * ng_ref[pl.ds(h, 1), :] * (gg * jax.nn.sigmoid(gg))).astype(BF16))
    y = _dot(jnp.concatenate(parts, axis=-1), w_ref[...])
    o_ref[...] = x_ref[...] + g1_ref[0] * y


def _hgrn_out(o_fw, o_bw, proj_hm, norm_g, w_out_bf16, x, g1, seq, tm=512):
    hh = HG_HEADS
    t, d = x.shape
    tpb = seq // tm
    return pl.pallas_call(
        _hgrn_out_kernel,
        grid=(t // tm,),
        in_specs=[pl.BlockSpec((hh, tm, LANES), lambda i: (0, i, 0)),
                  pl.BlockSpec((hh, tm, LANES), lambda i: (0, i, 0)),
                  pl.BlockSpec((hh, tm, LANES), lambda i: (4, i, 0)),
                  pl.BlockSpec((hh, LANES), lambda i: (0, 0)),
                  pl.BlockSpec((d, d), lambda i: (0, 0)),
                  pl.BlockSpec((tm, d), lambda i: (i, 0)),
                  pl.BlockSpec((1, 1, d), lambda i: (i // tpb, 0, 0))],
        out_specs=pl.BlockSpec((tm, d), lambda i: (i, 0)),
        out_shape=jax.ShapeDtypeStruct((t, d), F32),
        compiler_params=_cparams(("arbitrary",)),
        name="hgrn_out",
    )(o_fw, o_bw, proj_hm, norm_g.reshape(hh, LANES), w_out_bf16, x, g1)


def _lru_group_scan(a, x, carry, rows, reverse):
    for s in (1, 2, 4):
        if not reverse:
            keep = rows >= s
            a_sh = jnp.where(keep, pltpu.roll(a, s, 0), 1.0)
            x_sh = jnp.where(keep, pltpu.roll(x, s, 0), 0.0)
        else:
            keep = rows < SUBLANES - s
            a_sh = jnp.where(keep, pltpu.roll(a, SUBLANES - s, 0), 1.0)
            x_sh = jnp.where(keep, pltpu.roll(x, SUBLANES - s, 0), 0.0)
        x = x + a * x_sh
        a = a * a_sh
    hs = x + a * carry
    new_carry = hs[0:1] if reverse else hs[SUBLANES - 1:SUBLANES]
    return hs, new_carry


def _lru_kernel(xc_ref, xp_ref, xn_ref, cw_ref, cb_ref, wa_ref, ba_ref, wx_ref, bx_ref, lam_ref,
                o_ref, a_scr, b_scr, carry_scr, *, tm, nchunk):
    d = pl.program_id(0)
    cc = pl.program_id(2)
    chunk = jnp.where(d == 0, cc, nchunk - 1 - cc)

    @pl.when(cc == 0)
    def _():
        carry_scr[...] = jnp.zeros_like(carry_scr)

    xcur = xc_ref[...]
    prev = jnp.where(chunk == 0, 0.0, xp_ref[...])
    nxt = jnp.where(chunk == nchunk - 1, 0.0, xn_ref[...])
    ext = jnp.concatenate([prev, xcur, nxt], axis=0)
    cw = cw_ref[...]
    xc = cb_ref[...]
    for j in range(CONV_W):
        off = SUBLANES - CONV_LEFT + j
        xc = xc + ext[off:off + tm] * cw[j:j + 1]

    xcb = xc.astype(BF16)
    bw = xcb.shape[1] // LRU_BLOCKS
    ra, rx = [], []
    for n in range(LRU_BLOCKS):
        blk = xcb[:, n * bw:(n + 1) * bw]
        ra.append(_dot(blk, wa_ref[0, n]))
        rx.append(_dot(blk, wx_ref[0, n]))
    r = jax.nn.sigmoid(jnp.concatenate(ra, axis=-1) + ba_ref[0])
    ig = jax.nn.sigmoid(jnp.concatenate(rx, axis=-1) + bx_ref[0])
    lam = lam_ref[0]
    softplus_neg = jnp.maximum(-lam, 0.0) + jnp.log(1.0 + jnp.exp(-jnp.abs(lam)))
    log_a = -LRU_C * r * softplus_neg
    a = jnp.exp(log_a)
    a_scr[...] = a
    b_scr[...] = jnp.sqrt(-jnp.tanh(log_a) * (a * a + 1.0)) * ig * xc

    ngroups = tm // SUBLANES
    rows = lax.broadcasted_iota(jnp.int32, (SUBLANES, xc.shape[1]), 0)

    def run(reverse):
        def body(i, carry):
            gi = (ngroups - 1 - i) if reverse else i
            sl = pl.ds(pl.multiple_of(gi * SUBLANES, SUBLANES), SUBLANES)
            hs, carry = _lru_group_scan(a_scr[sl, :], b_scr[sl, :], carry, rows, reverse)
            o_ref[0, sl, :] = hs
            return carry
        carry_scr[...] = lax.fori_loop(0, ngroups, body, carry_scr[...])

    @pl.when(d == 0)
    def _():
        run(False)

    @pl.when(d == 1)
    def _():
        run(True)


def _lru_scan(proj, conv_w, conv_b, w_a, b_a, w_x, b_x, lam, batch, seq, tm=256):
    t = proj.shape[0]
    w = proj.shape[1] // 2
    nchunk = seq // tm
    hb = tm // SUBLANES
    nhalo = t // SUBLANES

    def pos(d, b, cc):
        return b * nchunk + jnp.where(d == 0, cc, nchunk - 1 - cc)

    return pl.pallas_call(
        functools.partial(_lru_kernel, tm=tm, nchunk=nchunk),
        grid=(2, batch, nchunk),
        in_specs=[pl.BlockSpec((tm, w), lambda d, b, cc: (pos(d, b, cc), 0)),
                  pl.BlockSpec((SUBLANES, w), lambda d, b, cc: (jnp.maximum(pos(d, b, cc) * hb - 1, 0), 0)),
                  pl.BlockSpec((SUBLANES, w),
                               lambda d, b, cc: (jnp.minimum((pos(d, b, cc) + 1) * hb, nhalo - 1), 0)),
                  pl.BlockSpec((CONV_W, w), lambda d, b, cc: (0, 0)),
                  pl.BlockSpec((1, w), lambda d, b, cc: (0, 0)),
                  pl.BlockSpec((1, LRU_BLOCKS, w // LRU_BLOCKS, w // LRU_BLOCKS), lambda d, b, cc: (d, 0, 0, 0)),
                  pl.BlockSpec((1, 1, w), lambda d, b, cc: (d, 0, 0)),
                  pl.BlockSpec((1, LRU_BLOCKS, w // LRU_BLOCKS, w // LRU_BLOCKS), lambda d, b, cc: (d, 0, 0, 0)),
                  pl.BlockSpec((1, 1, w), lambda d, b, cc: (d, 0, 0)),
                  pl.BlockSpec((1, 1, w), lambda d, b, cc: (d, 0, 0))],
        out_specs=pl.BlockSpec((1, tm, w), lambda d, b, cc: (d, pos(d, b, cc), 0)),
        out_shape=jax.ShapeDtypeStruct((2, t, w), F32),
        scratch_shapes=[pltpu.VMEM((tm, w), F32), pltpu.VMEM((tm, w), F32), pltpu.VMEM((1, w), F32)],
        compiler_params=_cparams(("arbitrary", "arbitrary", "arbitrary")),
        name="lru_scan",
    )(proj, proj, proj, conv_w, conv_b.reshape(1, w), w_a.astype(BF16), b_a.reshape(2, 1, w),
      w_x.astype(BF16), b_x.reshape(2, 1, w), lam.reshape(2, 1, w))


def _lru_out_kernel(hs_ref, yb_ref, w_ref, x_ref, g1_ref, o_ref):
    u = (hs_ref[0] + hs_ref[1]) * jax.nn.gelu(yb_ref[...])
    o_ref[...] = x_ref[...] + g1_ref[0] * _dot(u.astype(BF16), w_ref[...])


def _lru_out(hs, proj, w_out_bf16, x, g1, seq, tm=512):
    t, d = x.shape
    w = hs.shape[2]
    tpb = seq // tm
    return pl.pallas_call(
        _lru_out_kernel,
        grid=(t // tm,),
        in_specs=[pl.BlockSpec((2, tm, w), lambda i: (0, i, 0)),
                  pl.BlockSpec((tm, w), lambda i: (i, 1)),
                  pl.BlockSpec((w, d), lambda i: (0, 0)),
                  pl.BlockSpec((tm, d), lambda i: (i, 0)),
                  pl.BlockSpec((1, 1, d), lambda i: (i // tpb, 0, 0))],
        out_specs=pl.BlockSpec((tm, d), lambda i: (i, 0)),
        out_shape=jax.ShapeDtypeStruct((t, d), F32),
        compiler_params=_cparams(("arbitrary",)),
        name="lru_out",
    )(hs, proj, w_out_bf16, x, g1)


def _peer_q_kernel(x_ref, g_ref, sc_ref, sh_ref, w1_ref, w2_ref, k1_ref, k2_ref, hb_ref, sc_out_ref):
    h = _norm_mod(x_ref[...], g_ref[...], sc_ref[0], sh_ref[0])
    h1, h2 = _split2(h)
    hb_ref[...] = h.T.astype(BF16)
    q = _dot(h1, w1_ref[...]) + _dot(h1, w2_ref[...]) + _dot(h2, w1_ref[...])
    nhp = k1_ref.shape[0]
    nch = sc_out_ref.shape[1]
    for hp in range(nhp):
        qa, qb = _split2(q[:, hp * LANES:(hp + 1) * LANES])
        s = _dot_nt(k1_ref[hp], qa) + _dot_nt(k1_ref[hp], qb) + _dot_nt(k2_ref[hp], qa)
        for ch in range(nch):
            sc_out_ref[hp, ch] = s[:, ch * LANES:(ch + 1) * LANES]


def _peer_q(x, gain, sc, sh, wq1, wq2, keys1, keys2, seq, tm=256):
    t, d = x.shape
    nq = wq1.shape[1]
    nhp = keys1.shape[0]
    tpb = seq // tm
    nch = tm // LANES
    return pl.pallas_call(
        _peer_q_kernel,
        grid=(t // tm,),
        in_specs=[pl.BlockSpec((tm, d), lambda i: (i, 0)),
                  pl.BlockSpec((1, d), lambda i: (0, 0)),
                  pl.BlockSpec((1, 1, d), lambda i: (i // tpb, 0, 0)),
                  pl.BlockSpec((1, 1, d), lambda i: (i // tpb, 0, 0)),
                  pl.BlockSpec((d, nq), lambda i: (0, 0)),
                  pl.BlockSpec((d, nq), lambda i: (0, 0)),
                  pl.BlockSpec((nhp, PEER_NKEYS, LANES), lambda i: (0, 0, 0)),
                  pl.BlockSpec((nhp, PEER_NKEYS, LANES), lambda i: (0, 0, 0))],
        out_specs=[pl.BlockSpec((d, tm), lambda i: (0, i)),
                   pl.BlockSpec((nhp, nch, PEER_NKEYS, LANES), lambda i: (0, i, 0, 0))],
        out_shape=[jax.ShapeDtypeStruct((d, t), BF16),
                   jax.ShapeDtypeStruct((nhp, t // LANES, PEER_NKEYS, LANES), F32)],
        compiler_params=_cparams(("arbitrary",)),
        name="peer_q",
    )(x, gain.reshape(1, d), sc, sh, wq1, wq2, keys1, keys2)


def _top16(s, rows, exact):
    work = s
    rank = jnp.full(s.shape, PEER_TOPK, jnp.int32)
    vals = []
    for kk in range(PEER_TOPK):
        m = jnp.max(work, axis=0, keepdims=True)
        if exact:
            idx = jnp.min(jnp.where(work == m, rows, PEER_NKEYS), axis=0, keepdims=True)
            sel = rows == idx
        else:
            sel = work == m
        rank = jnp.where(sel, kk, rank)
        work = jnp.where(sel, NEG_INF, work)
        vals.append(m)
    return rank, jnp.concatenate(vals, axis=0)


def _col_count(mask):
    return jnp.sum(jnp.where(mask, 1.0, 0.0), axis=0, keepdims=True)


def _any_lane_differs(count, want):
    return jnp.max(jnp.where(count == want, 0.0, 1.0)) > 0.5


def _top16_values(s):
    work = s
    vals = []
    for _ in range(PEER_TOPK):
        m = jnp.max(work, axis=0, keepdims=True)
        work = jnp.where(work == m, NEG_INF, work)
        vals.append(m)
    return jnp.concatenate(vals, axis=0), work


def _cand16(cand, flat, r16, mtop):
    cnt = jnp.zeros((PEER_TOPK, cand.shape[1]), F32)
    z = jnp.zeros((1, cand.shape[1]), F32)
    for _ in range(PEER_TOPK):
        m = jnp.max(cand, axis=0, keepdims=True)
        idx = jnp.min(jnp.where(cand == m, flat, PEER_TOPK * PEER_TOPK), axis=0, keepdims=True)
        cand = jnp.where(flat == idx, NEG_INF, cand)
        cnt = cnt + jnp.where(r16 == (idx >> 4), 1.0, 0.0)
        z = z + jnp.exp(m - mtop)
    return cnt, z


def _bf16_pair(x):
    u = pltpu.bitcast(x.astype(BF16).astype(F32), jnp.uint32)
    return u | (u >> 16)


PEER_TOPK_GROUP = 4


def _peer_topk_kernel(s_ref, e1_ref, n_ref, e2_ref, r2_ref):
    nch = s_ref.shape[1]
    grp = PEER_TOPK_GROUP
    width = grp * LANES
    rows = lax.broadcasted_iota(jnp.int32, (PEER_NKEYS, width), 0)
    r16 = lax.broadcasted_iota(jnp.int32, (PEER_TOPK, width), 0)
    r8 = lax.broadcasted_iota(jnp.int32, (SUBLANES, width), 0)
    flat = jnp.concatenate([r16] + [r8 + PEER_TOPK * k1 for k1 in range(1, 8)]
                           + [(r8 + 8) * PEER_TOPK], axis=0)
    want = float(PEER_TOPK)

    def body(i, carry):
        chunks = [i * grp + k for k in range(grp)]
        s1 = jnp.concatenate([s_ref[0, ch] for ch in chunks], axis=1)
        s2 = jnp.concatenate([s_ref[1, ch] for ch in chunks], axis=1)
        def candidates(v1, v2):
            return jnp.concatenate([v1[0:1] + v2] + [v1[k1:k1 + 1] + v2[0:8] for k1 in range(1, 8)]
                                   + [v1[8:16] + v2[0:1]], axis=0)

        def exact():
            rank1, v1 = _top16(s1, rows, True)
            rank2, v2 = _top16(s2, rows, True)
            mtop = v1[0:1] + v2[0:1]
            cnt, z = _cand16(candidates(v1, v2), flat, r16, mtop)
            nrow = jnp.zeros((PEER_NKEYS, width), F32)
            for kk in range(PEER_TOPK):
                nrow = jnp.where(rank1 == kk, cnt[kk:kk + 1], nrow)
            e1 = jnp.where(rank1 < PEER_TOPK, jnp.exp(s1 - v1[0:1]) / z, 0.0)
            return nrow, e1, rank2, v2

        v1, left1 = _top16_values(s1)
        rank2, v2 = _top16(s2, rows, False)
        cand = candidates(v1, v2)
        mtop = v1[0:1] + v2[0:1]
        z = jnp.zeros((1, width), F32)
        tau = mtop
        for _ in range(PEER_TOPK):
            tau = jnp.max(cand, axis=0, keepdims=True)
            cand = jnp.where(cand == tau, NEG_INF, cand)
            z = z + jnp.exp(tau - mtop)
        tied = jnp.logical_or(
            jnp.logical_or(_any_lane_differs(_col_count(left1 == NEG_INF), want),
                           _any_lane_differs(_col_count(rank2 < PEER_TOPK), want)),
            _any_lane_differs(_col_count(cand == NEG_INF), want))

        def has(vrow):
            return (s1 + vrow) >= tau

        c16 = has(v2[15:16])
        c8 = has(v2[7:8])
        c4 = has(jnp.where(c8, v2[11:12], v2[3:4]))
        c2 = has(jnp.where(c8, jnp.where(c4, v2[13:14], v2[9:10]), jnp.where(c4, v2[5:6], v2[1:2])))
        hi = jnp.where(c4, jnp.where(c2, v2[14:15], v2[12:13]), jnp.where(c2, v2[10:11], v2[8:9]))
        lo = jnp.where(c4, jnp.where(c2, v2[6:7], v2[4:5]), jnp.where(c2, v2[2:3], v2[0:1]))
        c1 = has(jnp.where(c8, hi, lo))
        nrow = (jnp.where(c8, 8.0, 0.0) + jnp.where(c4, 4.0, 0.0)) + (jnp.where(c2, 2.0, 0.0) + jnp.where(c1, 1.0, 0.0))
        nrow = jnp.where(c16, float(PEER_TOPK), nrow)
        e1 = jnp.exp(s1 - v1[0:1]) / z
        tied = jnp.logical_or(tied, _any_lane_differs(jnp.sum(nrow, axis=0, keepdims=True), want))

        nrow, e1, rank2, v2 = lax.cond(tied, exact, lambda: (nrow, e1, rank2, v2))
        e1 = _bf16_pair(e1)
        nrow = _bf16_pair(nrow)
        e2 = jnp.where(rank2 < PEER_TOPK, jnp.exp(s2 - v2[0:1]), 0.0)
        e2 = pltpu.bitcast(e2.astype(BF16), jnp.uint32)
        r2 = pltpu.bitcast(rank2.astype(F32).astype(BF16), jnp.uint32)
        for k, ch in enumerate(chunks):
            cols = slice(k * LANES, (k + 1) * LANES)
            e1_ref[0, ch] = e1[:, cols]
            n_ref[0, ch] = nrow[:, cols]
            e2_ref[0, ch] = e2[:, cols]
            r2_ref[0, ch] = r2[:, cols]
        return carry

    lax.fori_loop(0, nch // grp, body, 0)


def _peer_topk(scores_t, nch=4):
    nhp, ntc, nk, _ = scores_t.shape
    hh = nhp // 2
    spec = pl.BlockSpec((1, nch, nk, LANES), lambda i, h: (h, i, 0, 0))
    spec_p = pl.BlockSpec((1, nch, nk // 2, LANES), lambda i, h: (h, i, 0, 0))
    sd_u = jax.ShapeDtypeStruct((hh, ntc, nk, LANES), jnp.uint32)
    sd_b = jax.ShapeDtypeStruct((hh, ntc, nk // 2, LANES), jnp.uint32)
    return pl.pallas_call(
        _peer_topk_kernel,
        grid=(ntc // nch, hh),
        in_specs=[pl.BlockSpec((2, nch, nk, LANES), lambda i, h: (h, i, 0, 0))],
        out_specs=[spec, spec, spec_p, spec_p],
        out_shape=[sd_u, sd_u, sd_b, sd_b],
        compiler_params=_cparams(("arbitrary", "arbitrary")),
        name="peer_topk",
    )(scores_t)


def _dup_rows(row_u32, nrows):
    return pltpu.bitcast(jnp.broadcast_to(row_u32, (nrows // 2, row_u32.shape[1])), BF16)


PEER_PB = 2


GELU_C1 = 0.7978845608028654
GELU_C2 = GELU_C1 * 0.044715


def _gelu_times(x, w):
    hw = (0.5 * x) * w
    return hw + hw * jnp.tanh(x * (GELU_C1 + GELU_C2 * (x * x)))


def _peer_dense_kernel(hbt_ref, u_ref, v_ref, e1_ref, n_ref, e2_ref, r2_ref, x_ref, g2_ref,
                       o_ref, acc_ref, act0, act1, ab_ref):
    j = pl.program_id(1)

    @pl.when(j == 0)
    def _():
        acc_ref[...] = jnp.zeros_like(acc_ref)

    nch = e2_ref.shape[1]
    nb = u_ref.shape[0]
    acts = (act0, act1)
    zero = jnp.zeros((), BF16)
    blk_rows = PEER_PB * PEER_NKEYS

    def act_mm(p):
        acts[p % 2][...] = _dot(pltpu.bitcast(u_ref[p], BF16), hbt_ref[...])

    def gate(p):
        for ch in range(nch):
            for a2 in range(PEER_PB):
                a = p * PEER_PB + a2
                w = jnp.zeros((PEER_NKEYS, LANES), BF16)
                for h in range(PEER_HEADS):
                    keep = pltpu.bitcast(r2_ref[h, ch], BF16) < _dup_rows(n_ref[h, ch, a:a + 1, :], PEER_NKEYS)
                    w = w + (jnp.where(keep, pltpu.bitcast(e2_ref[h, ch], BF16), zero)
                             * _dup_rows(e1_ref[h, ch, a:a + 1, :], PEER_NKEYS))
                rows = slice(a2 * PEER_NKEYS, (a2 + 1) * PEER_NKEYS)
                cols = slice(ch * LANES, (ch + 1) * LANES)
                out_rows = slice(p * blk_rows + a2 * PEER_NKEYS, p * blk_rows + (a2 + 1) * PEER_NKEYS)
                ab_ref[out_rows, cols] = _gelu_times(acts[p % 2][rows, cols].astype(BF16), w)

    act_mm(0)
    for p in range(nb):
        if p + 1 < nb:
            act_mm(p + 1)
        gate(p)
    acc_ref[...] += _dot_tn(ab_ref[...], pltpu.bitcast(v_ref[...], BF16))

    @pl.when(j == pl.num_programs(1) - 1)
    def _():
        o_ref[...] = x_ref[...] + g2_ref[0] * acc_ref[...]


def _pack_rows_kernel(x_ref, o_ref):
    o_ref[...] = pltpu.bitcast(x_ref[...].astype(BF16), jnp.uint32)


def _pack_rows(tabs, layer, tr=1024):
    _, n, d = tabs.shape
    return pl.pallas_call(
        _pack_rows_kernel,
        grid=(n // tr,),
        in_specs=[pl.BlockSpec((None, tr, d), lambda i: (layer, i, 0))],
        out_specs=pl.BlockSpec((tr // 2, d), lambda i: (i, 0)),
        out_shape=jax.ShapeDtypeStruct((n // 2, d), jnp.uint32),
        compiler_params=_cparams(("arbitrary",)),
        name="pack_rows",
    )(tabs)


def _peer_dense(hbt, u_pk, v_pk, e1, nrow, e2, rank2, x, g2, seq, tm=512, na=16):
    t, d = x.shape
    ne = 2 * u_pk.shape[0]
    rows = PEER_PB * PEER_NKEYS
    nb = na // PEER_PB
    te = nb * rows
    nch = tm // LANES
    tpb = seq // tm
    hh = PEER_HEADS
    small = pl.BlockSpec((hh, nch, na, LANES), lambda i, j: (0, i, j, 0))
    big = pl.BlockSpec((hh, nch, PEER_NKEYS // 2, LANES), lambda i, j: (0, i, 0, 0))
    return pl.pallas_call(
        _peer_dense_kernel,
        grid=(t // tm, ne // te),
        in_specs=[pl.BlockSpec((d, tm), lambda i, j: (0, i)),
                  pl.BlockSpec((nb, rows // 2, d), lambda i, j: (j, 0, 0)),
                  pl.BlockSpec((te // 2, d), lambda i, j: (j, 0)),
                  small, small, big, big,
                  pl.BlockSpec((tm, d), lambda i, j: (i, 0)),
                  pl.BlockSpec((1, 1, d), lambda i, j: (i // tpb, 0, 0))],
        out_specs=pl.BlockSpec((tm, d), lambda i, j: (i, 0)),
        out_shape=jax.ShapeDtypeStruct((t, d), F32),
        scratch_shapes=[pltpu.VMEM((tm, d), F32), pltpu.VMEM((rows, tm), F32), pltpu.VMEM((rows, tm), F32),
                        pltpu.VMEM((te, tm), BF16)],
        compiler_params=_cparams(("arbitrary", "arbitrary")),
        name="peer_dense",
    )(hbt, u_pk.reshape(ne // rows, rows // 2, d), v_pk, e1, nrow, e2, rank2, x, g2)


def _final_norm_kernel(x_ref, g_ref, o_ref):
    x = x_ref[...]
    inv = lax.rsqrt(jnp.mean(x * x, axis=-1, keepdims=True) + EPS)
    o_ref[...] = (x * inv) * g_ref[...]


def _final_norm(x, gain, tm=1024):
    t, d = x.shape
    return pl.pallas_call(
        _final_norm_kernel,
        grid=(t // tm,),
        in_specs=[pl.BlockSpec((tm, d), lambda i: (i, 0)), pl.BlockSpec((1, d), lambda i: (0, 0))],
        out_specs=pl.BlockSpec((tm, d), lambda i: (i, 0)),
        out_shape=jax.ShapeDtypeStruct((t, d), F32),
        compiler_params=_cparams(("arbitrary",)),
        name="final_norm",
    )(x, gain.reshape(1, d))


def _peer_layer(x, gain, sc, sh, g2, w_q, keys, u_tabs, v_tabs, layer, seq):
    wq1, wq2 = _split2(w_q)
    kf = keys.reshape(-1, PEER_NKEYS, keys.shape[-1])
    k1, k2 = _split2(kf)
    hb, scores_t = _peer_q(x, gain, sc, sh, wq1, wq2, k1, k2, seq)
    e1, nrow, e2, rank2 = _peer_topk(scores_t)
    return _peer_dense(hb, _pack_rows(u_tabs, layer), _pack_rows(v_tabs, layer), e1, nrow, e2, rank2,
                       x, g2, seq)


def kernel(x, c, w_ada, b_ada, norm_g, hg_w_in, hg_lb, hg_norm_g, hg_w_out, lru_w_in, lru_conv_w,
           lru_conv_b, lru_w_a, lru_b_a, lru_w_x, lru_b_x, lru_lam, lru_w_out, peer_w_q, peer_keys,
           peer_u, peer_v, final_g):
    batch, seq, d = x.shape
    depth = w_ada.shape[0]
    n_mixers = 2
    mod = _ada_mod(c, w_ada, b_ada)
    xt = x.reshape(batch * seq, d)
    for i in range(depth):
        parts = [mod[i, :, k * d:(k + 1) * d].reshape(batch, 1, d) for k in range(6)]
        sh1, sc1, g1, sh2, sc2, g2 = parts
        j = i // n_mixers
        if i % n_mixers == 0:
            proj = _norm_mod_matmul(xt, norm_g[i, 0], sc1, sh1, hg_w_in[j].astype(BF16), seq,
                                    head_major=True)
            o_fw, o_bw = _hgrn_scan(proj, hg_lb, j, batch, seq)
            xt = _hgrn_out(o_fw, o_bw, proj, hg_norm_g[j], hg_w_out[j].astype(BF16), xt, g1, seq)
        else:
            proj = _norm_mod_matmul(xt, norm_g[i, 0], sc1, sh1, lru_w_in[j].astype(BF16), seq,
                                    head_major=False)
            hs = _lru_scan(proj, lru_conv_w[j], lru_conv_b[j], lru_w_a[j], lru_b_a[j], lru_w_x[j],
                           lru_b_x[j], lru_lam[j], batch, seq)
            xt = _lru_out(hs, proj, lru_w_out[j].astype(BF16), xt, g1, seq)
        xt = _peer_layer(xt, norm_g[i, 1], sc2, sh2, g2, peer_w_q[i], peer_keys[i], peer_u, peer_v,
                         i, seq)
    return _final_norm(xt, final_g).reshape(batch, seq, d)
```

```python
import functools

import numpy as np
import jax
import jax.numpy as jnp
from jax import lax
from jax.experimental import pallas as pl
from jax.experimental.pallas import tpu as pltpu

F32 = jnp.float32
BF16 = jnp.bfloat16

SUBLANES = 8
LANES = 128
VMEM_LIMIT = 56 * 1024 * 1024

EPS = 1e-6
HG_HEADS = 8
HG_CHUNK = 128
LRU_BLOCKS = 4
LRU_C = 8.0
CONV_W = 4
CONV_LEFT = 2
PEER_HEADS = 8
PEER_NKEYS = 128
PEER_TOPK = 16
HG_LEVELS = 7
NEG_INF = float("-inf")
LOG2_E = 1.4426950408889634


def _cparams(sem):
    return pltpu.CompilerParams(dimension_semantics=sem, vmem_limit_bytes=VMEM_LIMIT)


def _split3(x):
    x1 = x.astype(BF16)
    r1 = x - x1.astype(F32)
    x2 = r1.astype(BF16)
    x3 = (r1 - x2.astype(F32)).astype(BF16)
    return x1, x2, x3


def _split2(x):
    x1 = x.astype(BF16)
    x2 = (x - x1.astype(F32)).astype(BF16)
    return x1, x2


def _dot(a, b):
    return jnp.dot(a, b, preferred_element_type=F32)


def _dot_nt(a, b):
    return lax.dot_general(a, b, (((1,), (1,)), ((), ())), preferred_element_type=F32)


def _dot_tn(a, b):
    return lax.dot_general(a, b, (((0,), (0,)), ((), ())), preferred_element_type=F32)


def _norm_mod(x, gain, sc, sh):
    inv = lax.rsqrt(jnp.mean(x * x, axis=-1, keepdims=True) + EPS)
    return (x * inv) * gain * (1.0 + sc) + sh


def _ada_kernel(c_ref, w_ref, b_ref, o_ref):
    c = c_ref[...]
    cond = c * jax.nn.sigmoid(c)
    c1, c2, c3 = _split3(cond)
    w1, w2, w3 = _split3(w_ref[0])
    acc = _dot(c1, w1) + _dot(c1, w2) + _dot(c2, w1)
    acc = acc + _dot(c1, w3) + _dot(c2, w2) + _dot(c3, w1)
    o_ref[0] = acc + b_ref[0]


def _ada_mod(c, w_ada, b_ada):
    depth, d, n = w_ada.shape
    b = c.shape[0]
    cp = jnp.zeros((SUBLANES, d), F32).at[:b].set(c)
    tn = 1536
    out = pl.pallas_call(
        _ada_kernel,
        grid=(depth, n // tn),
        in_specs=[pl.BlockSpec((SUBLANES, d), lambda i, j: (0, 0)),
                  pl.BlockSpec((1, d, tn), lambda i, j: (i, 0, j)),
                  pl.BlockSpec((1, 1, tn), lambda i, j: (i, 0, j))],
        out_specs=pl.BlockSpec((1, SUBLANES, tn), lambda i, j: (i, 0, j)),
        out_shape=jax.ShapeDtypeStruct((depth, SUBLANES, n), F32),
        compiler_params=_cparams(("arbitrary", "arbitrary")),
        name="ada_mod",
    )(cp, w_ada, b_ada.reshape(depth, 1, n))
    return out[:, :b]


def _nmm_kernel(x_ref, g_ref, sc_ref, sh_ref, w_ref, o_ref, h_scr, *, head_major):
    @pl.when(pl.program_id(1) == 0)
    def _():
        h_scr[...] = _norm_mod(x_ref[...], g_ref[...], sc_ref[0], sh_ref[0]).astype(BF16)

    res = _dot(h_scr[...], w_ref[...])
    if head_major:
        for k in range(o_ref.shape[0]):
            o_ref[k] = res[:, k * LANES:(k + 1) * LANES]
    else:
        o_ref[...] = res


def _norm_mod_matmul(x, gain, sc, sh, w_bf16, seq, *, head_major, tm=1024, tn=1024):
    t, d = x.shape
    n = w_bf16.shape[1]
    tpb = seq // tm
    if head_major:
        out_shape = jax.ShapeDtypeStruct((n // LANES, t, LANES), F32)
        out_spec = pl.BlockSpec((tn // LANES, tm, LANES), lambda i, j: (j, i, 0))
    else:
        out_shape = jax.ShapeDtypeStruct((t, n), F32)
        out_spec = pl.BlockSpec((tm, tn), lambda i, j: (i, j))
    return pl.pallas_call(
        functools.partial(_nmm_kernel, head_major=head_major),
        grid=(t // tm, n // tn),
        in_specs=[pl.BlockSpec((tm, d), lambda i, j: (i, 0)),
                  pl.BlockSpec((1, d), lambda i, j: (0, 0)),
                  pl.BlockSpec((1, 1, d), lambda i, j: (i // tpb, 0, 0)),
                  pl.BlockSpec((1, 1, d), lambda i, j: (i // tpb, 0, 0)),
                  pl.BlockSpec((d, tn), lambda i, j: (0, j))],
        out_specs=out_spec,
        out_shape=out_shape,
        scratch_shapes=[pltpu.VMEM((tm, d), BF16)],
        compiler_params=_cparams(("arbitrary", "arbitrary")),
        name="norm_mod_matmul",
    )(x, gain.reshape(1, d), sc, sh, w_bf16)


def _hgrn_sum_mats(reverse):
    c = HG_CHUNK
    mats = []
    for lvl in range(2, HG_LEVELS + 1):
        m = 1 << lvl
        half = m >> 1
        a = np.zeros((c, c), np.float32)
        for t in range(c):
            mid = (t // m) * m + half
            if not reverse:
                if t >= mid:
                    a[t, mid:t + 1] = 1.0
                else:
                    a[t, t + 1:mid] = 1.0
            else:
                if t < mid:
                    a[t, t:mid] = 1.0
                else:
                    a[t, mid:t] = 1.0
        mats.append(a)
    ones = np.ones((c, c), np.float32)
    mats.append(np.triu(ones) if reverse else np.tril(ones))
    return np.concatenate(mats, axis=0)


def _hgrn_chain(q, flog, v, lb, a_ref, st_ref, d, h, reverse):
    c = HG_CHUNK
    f = lb + (1.0 - lb) * jax.nn.sigmoid(flog)
    g = jnp.log(f) * LOG2_E
    k = 1.0 - f
    g1, g2 = _split2(g)
    e2 = _dot(a_ref[...], jnp.concatenate([g1, g2], axis=1))
    e = e2[:, :LANES] + e2[:, LANES:]
    xdec = jnp.exp2(e)
    rows = lax.broadcasted_iota(jnp.int32, (c, c), 0)
    cols = lax.broadcasted_iota(jnp.int32, (c, c), 1)
    attn = jnp.where(rows == cols, jnp.sum(q * k, axis=-1, keepdims=True), 0.0)
    for lvl in range(1, HG_LEVELS + 1):
        row_q = (((rows >> (lvl - 1)) & 1) == 1) != reverse
        col_k = (((cols >> (lvl - 1)) & 1) == 0) != reverse
        if lvl == 1:
            mixed = jnp.where(row_q, q * f, k).astype(BF16)
        else:
            mixed = (jnp.where(row_q, q, k) * xdec[(lvl - 2) * c:(lvl - 1) * c]).astype(BF16)
        p = _dot_nt(mixed, mixed)
        pick = jnp.logical_and((rows >> lvl) == (cols >> lvl), jnp.logical_and(row_q, col_k))
        attn = jnp.where(pick, p, attn)
    eq = e[(HG_LEVELS - 1) * c:HG_LEVELS * c]
    tot = eq[0:1] if reverse else eq[c - 1:c]
    qd = (q * xdec[(HG_LEVELS - 1) * c:HG_LEVELS * c]).astype(BF16)
    kd = (k * jnp.exp2(tot - eq)).astype(BF16)
    st = st_ref[d, h]
    vb = v.astype(BF16)
    o = _dot(attn.astype(BF16), vb) + _dot_nt(qd, st.astype(BF16))
    st_ref[d, h] = st * jnp.exp2(tot) + _dot_tn(vb, kd)
    return o


def _hgrn_scan_kernel(lb_ref, afw_ref, abw_ref, qf_ref, ff_ref, vf_ref, qb_ref, fb_ref, vb_ref,
                      of_ref, ob_ref, st_ref, lbs_ref, *, layer):
    @pl.when(pl.program_id(1) == 0)
    def _():
        st_ref[...] = jnp.zeros_like(st_ref)
        lb = lb_ref[...]
        ex = jnp.exp(lb - jnp.max(lb, axis=0, keepdims=True))
        p = ex / jnp.sum(ex, axis=0, keepdims=True)
        acc = p[0]
        for jj in range(1, layer + 1):
            acc = acc + p[jj]
        lbs_ref[...] = acc - p[0]

    def body(h, carry):
        lb = lbs_ref[pl.ds(h, 1), :]
        of_ref[h] = _hgrn_chain(qf_ref[h], ff_ref[h], vf_ref[h], lb, afw_ref, st_ref, 0, h, False)
        ob_ref[h] = _hgrn_chain(qb_ref[h], fb_ref[h], vb_ref[h], lb, abw_ref, st_ref, 1, h, True)
        return carry

    lax.fori_loop(0, HG_HEADS, body, 0, unroll=8)


def _hgrn_scan(proj_hm, hg_lb, layer, batch, seq):
    hh = HG_HEADS
    t = proj_hm.shape[1]
    nc = seq // HG_CHUNK
    c = HG_CHUNK
    na = hg_lb.shape[0]
    afw = jnp.asarray(_hgrn_sum_mats(False), BF16)
    abw = jnp.asarray(_hgrn_sum_mats(True), BF16)
    nrow = afw.shape[0]

    def fwd(part):
        return pl.BlockSpec((hh, c, LANES), lambda b, cc: (part, b * nc + cc, 0))

    def bwd(part):
        return pl.BlockSpec((hh, c, LANES), lambda b, cc: (part, b * nc + nc - 1 - cc, 0))

    out_sd = jax.ShapeDtypeStruct((hh, t, LANES), F32)
    return pl.pallas_call(
        functools.partial(_hgrn_scan_kernel, layer=layer),
        grid=(batch, nc),
        in_specs=[pl.BlockSpec((na, hh, LANES), lambda b, cc: (0, 0, 0)),
                  pl.BlockSpec((nrow, c), lambda b, cc: (0, 0)),
                  pl.BlockSpec((nrow, c), lambda b, cc: (0, 0)),
                  fwd(0), fwd(1), fwd(3), bwd(0), bwd(2), bwd(3)],
        out_specs=[pl.BlockSpec((hh, c, LANES), lambda b, cc: (0, b * nc + cc, 0)),
                   pl.BlockSpec((hh, c, LANES), lambda b, cc: (0, b * nc + nc - 1 - cc, 0))],
        out_shape=[out_sd, out_sd],
        scratch_shapes=[pltpu.VMEM((2, hh, LANES, LANES), F32), pltpu.VMEM((hh, LANES), F32)],
        compiler_params=_cparams(("arbitrary", "arbitrary")),
        name="hgrn_scan",
    )(hg_lb.reshape(na, hh, LANES), afw, abw, proj_hm, proj_hm, proj_hm, proj_hm, proj_hm, proj_hm)


def _hgrn_out_kernel(of_ref, ob_ref, gg_ref, ng_ref, w_ref, x_ref, g1_ref, o_ref):
    parts = []
    for h in range(HG_HEADS):
        o = of_ref[h] + ob_ref[h]
        inv = lax.rsqrt(jnp.mean(o * o, axis=-1, keepdims=True) + EPS)
        gg = gg_ref[h]
        parts.append(((o * inv) * ng_ref[pl.ds(h, 1), :] * (gg * jax.nn.sigmoid(gg))).astype(BF16))
    y = _dot(jnp.concatenate(parts, axis=-1), w_ref[...])
    o_ref[...] = x_ref[...] + g1_ref[0] * y


def _hgrn_out(o_fw, o_bw, proj_hm, norm_g, w_out_bf16, x, g1, seq, tm=512):
    hh = HG_HEADS
    t, d = x.shape
    tpb = seq // tm
    return pl.pallas_call(
        _hgrn_out_kernel,
        grid=(t // tm,),
        in_specs=[pl.BlockSpec((hh, tm, LANES), lambda i: (0, i, 0)),
                  pl.BlockSpec((hh, tm, LANES), lambda i: (0, i, 0)),
                  pl.BlockSpec((hh, tm, LANES), lambda i: (4, i, 0)),
                  pl.BlockSpec((hh, LANES), lambda i: (0, 0)),
                  pl.BlockSpec((d, d), lambda i: (0, 0)),
                  pl.BlockSpec((tm, d), lambda i: (i, 0)),
                  pl.BlockSpec((1, 1, d), lambda i: (i // tpb, 0, 0))],
        out_specs=pl.BlockSpec((tm, d), lambda i: (i, 0)),
        out_shape=jax.ShapeDtypeStruct((t, d), F32),
        compiler_params=_cparams(("arbitrary",)),
        name="hgrn_out",
    )(o_fw, o_bw, proj_hm, norm_g.reshape(hh, LANES), w_out_bf16, x, g1)


def _lru_group_scan(a, x, carry, rows, reverse):
    for s in (1, 2, 4):
        if not reverse:
            keep = rows >= s
            a_sh = jnp.where(keep, pltpu.roll(a, s, 0), 1.0)
            x_sh = jnp.where(keep, pltpu.roll(x, s, 0), 0.0)
        else:
            keep = rows < SUBLANES - s
            a_sh = jnp.where(keep, pltpu.roll(a, SUBLANES - s, 0), 1.0)
            x_sh = jnp.where(keep, pltpu.roll(x, SUBLANES - s, 0), 0.0)
        x = x + a * x_sh
        a = a * a_sh
    hs = x + a * carry
    new_carry = hs[0:1] if reverse else hs[SUBLANES - 1:SUBLANES]
    return hs, new_carry


def _lru_kernel(xc_ref, xp_ref, xn_ref, cw_ref, cb_ref, wa_ref, ba_ref, wx_ref, bx_ref, lam_ref,
                o_ref, a_scr, b_scr, carry_scr, *, tm, nchunk):
    d = pl.program_id(0)
    cc = pl.program_id(2)
    chunk = jnp.where(d == 0, cc, nchunk - 1 - cc)

    @pl.when(cc == 0)
    def _():
        carry_scr[...] = jnp.zeros_like(carry_scr)

    xcur = xc_ref[...]
    prev = jnp.where(chunk == 0, 0.0, xp_ref[...])
    nxt = jnp.where(chunk == nchunk - 1, 0.0, xn_ref[...])
    ext = jnp.concatenate([prev, xcur, nxt], axis=0)
    cw = cw_ref[...]
    xc = cb_ref[...]
    for j in range(CONV_W):
        off = SUBLANES - CONV_LEFT + j
        xc = xc + ext[off:off + tm] * cw[j:j + 1]

    xcb = xc.astype(BF16)
    bw = xcb.shape[1] // LRU_BLOCKS
    ra, rx = [], []
    for n in range(LRU_BLOCKS):
        blk = xcb[:, n * bw:(n + 1) * bw]
        ra.append(_dot(blk, wa_ref[0, n]))
        rx.append(_dot(blk, wx_ref[0, n]))
    r = jax.nn.sigmoid(jnp.concatenate(ra, axis=-1) + ba_ref[0])
    ig = jax.nn.sigmoid(jnp.concatenate(rx, axis=-1) + bx_ref[0])
    lam = lam_ref[0]
    softplus_neg = jnp.maximum(-lam, 0.0) + jnp.log(1.0 + jnp.exp(-jnp.abs(lam)))
    log_a = -LRU_C * r * softplus_neg
    a = jnp.exp(log_a)
    a_scr[...] = a
    b_scr[...] = jnp.sqrt(-jnp.tanh(log_a) * (a * a + 1.0)) * ig * xc

    ngroups = tm // SUBLANES
    rows = lax.broadcasted_iota(jnp.int32, (SUBLANES, xc.shape[1]), 0)

    def run(reverse):
        def body(i, carry):
            gi = (ngroups - 1 - i) if reverse else i
            sl = pl.ds(pl.multiple_of(gi * SUBLANES, SUBLANES), SUBLANES)
            hs, carry = _lru_group_scan(a_scr[sl, :], b_scr[sl, :], carry, rows, reverse)
            o_ref[0, sl, :] = hs
            return carry
        carry_scr[...] = lax.fori_loop(0, ngroups, body, carry_scr[...])

    @pl.when(d == 0)
    def _():
        run(False)

    @pl.when(d == 1)
    def _():
        run(True)


def _lru_scan(proj, conv_w, conv_b, w_a, b_a, w_x, b_x, lam, batch, seq, tm=256):
    t = proj.shape[0]
    w = proj.shape[1] // 2
    nchunk = seq // tm
    hb = tm // SUBLANES
    nhalo = t // SUBLANES

    def pos(d, b, cc):
        return b * nchunk + jnp.where(d == 0, cc, nchunk - 1 - cc)

    return pl.pallas_call(
        functools.partial(_lru_kernel, tm=tm, nchunk=nchunk),
        grid=(2, batch, nchunk),
        in_specs=[pl.BlockSpec((tm, w), lambda d, b, cc: (pos(d, b, cc), 0)),
                  pl.BlockSpec((SUBLANES, w), lambda d, b, cc: (jnp.maximum(pos(d, b, cc) * hb - 1, 0), 0)),
                  pl.BlockSpec((SUBLANES, w),
                               lambda d, b, cc: (jnp.minimum((pos(d, b, cc) + 1) * hb, nhalo - 1), 0)),
                  pl.BlockSpec((CONV_W, w), lambda d, b, cc: (0, 0)),
                  pl.BlockSpec((1, w), lambda d, b, cc: (0, 0)),
                  pl.BlockSpec((1, LRU_BLOCKS, w // LRU_BLOCKS, w // LRU_BLOCKS), lambda d, b, cc: (d, 0, 0, 0)),
                  pl.BlockSpec((1, 1, w), lambda d, b, cc: (d, 0, 0)),
                  pl.BlockSpec((1, LRU_BLOCKS, w // LRU_BLOCKS, w // LRU_BLOCKS), lambda d, b, cc: (d, 0, 0, 0)),
                  pl.BlockSpec((1, 1, w), lambda d, b, cc: (d, 0, 0)),
                  pl.BlockSpec((1, 1, w), lambda d, b, cc: (d, 0, 0))],
        out_specs=pl.BlockSpec((1, tm, w), lambda d, b, cc: (d, pos(d, b, cc), 0)),
        out_shape=jax.ShapeDtypeStruct((2, t, w), F32),
        scratch_shapes=[pltpu.VMEM((tm, w), F32), pltpu.VMEM((tm, w), F32), pltpu.VMEM((1, w), F32)],
        compiler_params=_cparams(("arbitrary", "arbitrary", "arbitrary")),
        name="lru_scan",
    )(proj, proj, proj, conv_w, conv_b.reshape(1, w), w_a.astype(BF16), b_a.reshape(2, 1, w),
      w_x.astype(BF16), b_x.reshape(2, 1, w), lam.reshape(2, 1, w))


def _lru_out_kernel(hs_ref, yb_ref, w_ref, x_ref, g1_ref, o_ref):
    u = (hs_ref[0] + hs_ref[1]) * jax.nn.gelu(yb_ref[...])
    o_ref[...] = x_ref[...] + g1_ref[0] * _dot(u.astype(BF16), w_ref[...])


def _lru_out(hs, proj, w_out_bf16, x, g1, seq, tm=512):
    t, d = x.shape
    w = hs.shape[2]
    tpb = seq // tm
    return pl.pallas_call(
        _lru_out_kernel,
        grid=(t // tm,),
        in_specs=[pl.BlockSpec((2, tm, w), lambda i: (0, i, 0)),
                  pl.BlockSpec((tm, w), lambda i: (i, 1)),
                  pl.BlockSpec((w, d), lambda i: (0, 0)),
                  pl.BlockSpec((tm, d), lambda i: (i, 0)),
                  pl.BlockSpec((1, 1, d), lambda i: (i // tpb, 0, 0))],
        out_specs=pl.BlockSpec((tm, d), lambda i: (i, 0)),
        out_shape=jax.ShapeDtypeStruct((t, d), F32),
        compiler_params=_cparams(("arbitrary",)),
        name="lru_out",
    )(hs, proj, w_out_bf16, x, g1)


def _peer_q_kernel(x_ref, g_ref, sc_ref, sh_ref, w1_ref, w2_ref, k1_ref, k2_ref, hb_ref, sc_out_ref):
    h = _norm_mod(x_ref[...], g_ref[...], sc_ref[0], sh_ref[0])
    h1, h2 = _split2(h)
    hb_ref[...] = h.T.astype(BF16)
    q = _dot(h1, w1_ref[...]) + _dot(h1, w2_ref[...]) + _dot(h2, w1_ref[...])
    nhp = k1_ref.shape[0]
    nch = sc_out_ref.shape[1]
    for hp in range(nhp):
        qa, qb = _split2(q[:, hp * LANES:(hp + 1) * LANES])
        s = _dot_nt(k1_ref[hp], qa) + _dot_nt(k1_ref[hp], qb) + _dot_nt(k2_ref[hp], qa)
        for ch in range(nch):
            sc_out_ref[hp, ch] = s[:, ch * LANES:(ch + 1) * LANES]


def _peer_q(x, gain, sc, sh, wq1, wq2, keys1, keys2, seq, tm=256):
    t, d = x.shape
    nq = wq1.shape[1]
    nhp = keys1.shape[0]
    tpb = seq // tm
    nch = tm // LANES
    return pl.pallas_call(
        _peer_q_kernel,
        grid=(t // tm,),
        in_specs=[pl.BlockSpec((tm, d), lambda i: (i, 0)),
                  pl.BlockSpec((1, d), lambda i: (0, 0)),
                  pl.BlockSpec((1, 1, d), lambda i: (i // tpb, 0, 0)),
                  pl.BlockSpec((1, 1, d), lambda i: (i // tpb, 0, 0)),
                  pl.BlockSpec((d, nq), lambda i: (0, 0)),
                  pl.BlockSpec((d, nq), lambda i: (0, 0)),
                  pl.BlockSpec((nhp, PEER_NKEYS, LANES), lambda i: (0, 0, 0)),
                  pl.BlockSpec((nhp, PEER_NKEYS, LANES), lambda i: (0, 0, 0))],
        out_specs=[pl.BlockSpec((d, tm), lambda i: (0, i)),
                   pl.BlockSpec((nhp, nch, PEER_NKEYS, LANES), lambda i: (0, i, 0, 0))],
        out_shape=[jax.ShapeDtypeStruct((d, t), BF16),
                   jax.ShapeDtypeStruct((nhp, t // LANES, PEER_NKEYS, LANES), F32)],
        compiler_params=_cparams(("arbitrary",)),
        name="peer_q",
    )(x, gain.reshape(1, d), sc, sh, wq1, wq2, keys1, keys2)


def _top16(s, rows, exact):
    work = s
    rank = jnp.full(s.shape, PEER_TOPK, jnp.int32)
    vals = []
    for kk in range(PEER_TOPK):
        m = jnp.max(work, axis=0, keepdims=True)
        if exact:
            idx = jnp.min(jnp.where(work == m, rows, PEER_NKEYS), axis=0, keepdims=True)
            sel = rows == idx
        else:
            sel = work == m
        rank = jnp.where(sel, kk, rank)
        work = jnp.where(sel, NEG_INF, work)
        vals.append(m)
    return rank, jnp.concatenate(vals, axis=0)


def _col_count(mask):
    return jnp.sum(jnp.where(mask, 1.0, 0.0), axis=0, keepdims=True)


def _any_lane_differs(count, want):
    return jnp.max(jnp.where(count == want, 0.0, 1.0)) > 0.5


def _top16_values(s):
    work = s
    vals = []
    for _ in range(PEER_TOPK):
        m = jnp.max(work, axis=0, keepdims=True)
        work = jnp.where(work == m, NEG_INF, work)
        vals.append(m)
    return jnp.concatenate(vals, axis=0), work


def _cand16(cand, flat, r16, mtop):
    cnt = jnp.zeros((PEER_TOPK, cand.shape[1]), F32)
    z = jnp.zeros((1, cand.shape[1]), F32)
    for _ in range(PEER_TOPK):
        m = jnp.max(cand, axis=0, keepdims=True)
        idx = jnp.min(jnp.where(cand == m, flat, PEER_TOPK * PEER_TOPK), axis=0, keepdims=True)
        cand = jnp.where(flat == idx, NEG_INF, cand)
        cnt = cnt + jnp.where(r16 == (idx >> 4), 1.0, 0.0)
        z = z + jnp.exp(m - mtop)
    return cnt, z


def _bf16_pair(x):
    u = pltpu.bitcast(x.astype(BF16).astype(F32), jnp.uint32)
    return u | (u >> 16)


PEER_TOPK_GROUP = 4


def _peer_topk_kernel(s_ref, e1_ref, n_ref, e2_ref, r2_ref):
    nch = s_ref.shape[1]
    grp = PEER_TOPK_GROUP
    width = grp * LANES
    rows = lax.broadcasted_iota(jnp.int32, (PEER_NKEYS, width), 0)
    r16 = lax.broadcasted_iota(jnp.int32, (PEER_TOPK, width), 0)
    r8 = lax.broadcasted_iota(jnp.int32, (SUBLANES, width), 0)
    flat = jnp.concatenate([r16] + [r8 + PEER_TOPK * k1 for k1 in range(1, 8)]
                           + [(r8 + 8) * PEER_TOPK], axis=0)
    want = float(PEER_TOPK)

    def body(i, carry):
        chunks = [i * grp + k for k in range(grp)]
        s1 = jnp.concatenate([s_ref[0, ch] for ch in chunks], axis=1)
        s2 = jnp.concatenate([s_ref[1, ch] for ch in chunks], axis=1)
        def candidates(v1, v2):
            return jnp.concatenate([v1[0:1] + v2] + [v1[k1:k1 + 1] + v2[0:8] for k1 in range(1, 8)]
                                   + [v1[8:16] + v2[0:1]], axis=0)

        def exact():
            rank1, v1 = _top16(s1, rows, True)
            rank2, v2 = _top16(s2, rows, True)
            mtop = v1[0:1] + v2[0:1]
            cnt, z = _cand16(candidates(v1, v2), flat, r16, mtop)
            nrow = jnp.zeros((PEER_NKEYS, width), F32)
            for kk in range(PEER_TOPK):
                nrow = jnp.where(rank1 == kk, cnt[kk:kk + 1], nrow)
            e1 = jnp.where(rank1 < PEER_TOPK, jnp.exp(s1 - v1[0:1]) / z, 0.0)
            return nrow, e1, rank2, v2

        v1, left1 = _top16_values(s1)
        rank2, v2 = _top16(s2, rows, False)
        cand = candidates(v1, v2)
        mtop = v1[0:1] + v2[0:1]
        z = jnp.zeros((1, width), F32)
        tau = mtop
        for _ in range(PEER_TOPK):
            tau = jnp.max(cand, axis=0, keepdims=True)
            cand = jnp.where(cand == tau, NEG_INF, cand)
            z = z + jnp.exp(tau - mtop)
        tied = jnp.logical_or(
            jnp.logical_or(_any_lane_differs(_col_count(left1 == NEG_INF), want),
                           _any_lane_differs(_col_count(rank2 < PEER_TOPK), want)),
            _any_lane_differs(_col_count(cand == NEG_INF), want))

        def has(vrow):
            return (s1 + vrow) >= tau

        c16 = has(v2[15:16])
        c8 = has(v2[7:8])
        c4 = has(jnp.where(c8, v2[11:12], v2[3:4]))
        c2 = has(jnp.where(c8, jnp.where(c4, v2[13:14], v2[9:10]), jnp.where(c4, v2[5:6], v2[1:2])))
        hi = jnp.where(c4, jnp.where(c2, v2[14:15], v2[12:13]), jnp.where(c2, v2[10:11], v2[8:9]))
        lo = jnp.where(c4, jnp.where(c2, v2[6:7], v2[4:5]), jnp.where(c2, v2[2:3], v2[0:1]))
        c1 = has(jnp.where(c8, hi, lo))
        nrow = (jnp.where(c8, 8.0, 0.0) + jnp.where(c4, 4.0, 0.0)) + (jnp.where(c2, 2.0, 0.0) + jnp.where(c1, 1.0, 0.0))
        nrow = jnp.where(c16, float(PEER_TOPK), nrow)
        e1 = jnp.exp(s1 - v1[0:1]) / z
        tied = jnp.logical_or(tied, _any_lane_differs(jnp.sum(nrow, axis=0, keepdims=True), want))

        nrow, e1, rank2, v2 = lax.cond(tied, exact, lambda: (nrow, e1, rank2, v2))
        e1 = _bf16_pair(e1)
        nrow = _bf16_pair(nrow)
        e2 = jnp.where(rank2 < PEER_TOPK, jnp.exp(s2 - v2[0:1]), 0.0)
        e2 = pltpu.bitcast(e2.astype(BF16), jnp.uint32)
        r2 = pltpu.bitcast(rank2.astype(F32).astype(BF16), jnp.uint32)
        for k, ch in enumerate(chunks):
            cols = slice(k * LANES, (k + 1) * LANES)
            e1_ref[0, ch] = e1[:, cols]
            n_ref[0, ch] = nrow[:, cols]
            e2_ref[0, ch] = e2[:, cols]
            r2_ref[0, ch] = r2[:, cols]
        return carry

    lax.fori_loop(0, nch // grp, body, 0)


def _peer_topk(scores_t, nch=4):
    nhp, ntc, nk, _ = scores_t.shape
    hh = nhp // 2
    spec = pl.BlockSpec((1, nch, nk, LANES), lambda i, h: (h, i, 0, 0))
    spec_p = pl.BlockSpec((1, nch, nk // 2, LANES), lambda i, h: (h, i, 0, 0))
    sd_u = jax.ShapeDtypeStruct((hh, ntc, nk, LANES), jnp.uint32)
    sd_b = jax.ShapeDtypeStruct((hh, ntc, nk // 2, LANES), jnp.uint32)
    return pl.pallas_call(
        _peer_topk_kernel,
        grid=(ntc // nch, hh),
        in_specs=[pl.BlockSpec((2, nch, nk, LANES), lambda i, h: (h, i, 0, 0))],
        out_specs=[spec, spec, spec_p, spec_p],
        out_shape=[sd_u, sd_u, sd_b, sd_b],
        compiler_params=_cparams(("arbitrary", "arbitrary")),
        name="peer_topk",
    )(scores_t)


def _dup_rows(row_u32, nrows):
    return pltpu.bitcast(jnp.broadcast_to(row_u32, (nrows // 2, row_u32.shape[1])), BF16)


PEER_PB = 2


GELU_C1 = 0.7978845608028654
GELU_C2 = GELU_C1 * 0.044715


def _gelu_times(x, w):
    hw = (0.5 * x) * w
    return hw + hw * jnp.tanh(x * (GELU_C1 + GELU_C2 * (x * x)))


def _peer_dense_kernel(hbt_ref, u_ref, v_ref, e1_ref, n_ref, e2_ref, r2_ref, x_ref, g2_ref,
                       o_ref, acc_ref, act0, act1, ab_ref):
    j = pl.program_id(1)

    @pl.when(j == 0)
    def _():
        acc_ref[...] = jnp.zeros_like(acc_ref)

    nch = e2_ref.shape[1]
    nb = u_ref.shape[0]
    acts = (act0, act1)
    zero = jnp.zeros((), BF16)
    blk_rows = PEER_PB * PEER_NKEYS

    def act_mm(p):
        acts[p % 2][...] = _dot(pltpu.bitcast(u_ref[p], BF16), hbt_ref[...])

    def gate(p):
        for ch in range(nch):
            for a2 in range(PEER_PB):
                a = p * PEER_PB + a2
                w = jnp.zeros((PEER_NKEYS, LANES), BF16)
                for h in range(PEER_HEADS):
                    keep = pltpu.bitcast(r2_ref[h, ch], BF16) < _dup_rows(n_ref[h, ch, a:a + 1, :], PEER_NKEYS)
                    w = w + (jnp.where(keep, pltpu.bitcast(e2_ref[h, ch], BF16), zero)
                             * _dup_rows(e1_ref[h, ch, a:a + 1, :], PEER_NKEYS))
                rows = slice(a2 * PEER_NKEYS, (a2 + 1) * PEER_NKEYS)
                cols = slice(ch * LANES, (ch + 1) * LANES)
                out_rows = slice(p * blk_rows + a2 * PEER_NKEYS, p * blk_rows + (a2 + 1) * PEER_NKEYS)
                ab_ref[out_rows, cols] = _gelu_times(acts[p % 2][rows, cols].astype(BF16), w)

    act_mm(0)
    for p in range(nb):
        if p + 1 < nb:
            act_mm(p + 1)
        gate(p)
    acc_ref[...] += _dot_tn(ab_ref[...], pltpu.bitcast(v_ref[...], BF16))

    @pl.when(j == pl.num_programs(1) - 1)
    def _():
        o_ref[...] = x_ref[...] + g2_ref[0] * acc_ref[...]


def _pack_rows_kernel(x_ref, o_ref):
    o_ref[...] = pltpu.bitcast(x_ref[...].astype(BF16), jnp.uint32)


def _pack_rows(tabs, layer, tr=1024):
    _, n, d = tabs.shape
    return pl.pallas_call(
        _pack_rows_kernel,
        grid=(n // tr,),
        in_specs=[pl.BlockSpec((None, tr, d), lambda i: (layer, i, 0))],
        out_specs=pl.BlockSpec((tr // 2, d), lambda i: (i, 0)),
        out_shape=jax.ShapeDtypeStruct((n // 2, d), jnp.uint32),
        compiler_params=_cparams(("arbitrary",)),
        name="pack_rows",
    )(tabs)


def _peer_dense(hbt, u_pk, v_pk, e1, nrow, e2, rank2, x, g2, seq, tm=512, na=16):
    t, d = x.shape
    ne = 2 * u_pk.shape[0]
    rows = PEER_PB * PEER_NKEYS
    nb = na // PEER_PB
    te = nb * rows
    nch = tm // LANES
    tpb = seq // tm
    hh = PEER_HEADS
    small = pl.BlockSpec((hh, nch, na, LANES), lambda i, j: (0, i, j, 0))
    big = pl.BlockSpec((hh, nch, PEER_NKEYS // 2, LANES), lambda i, j: (0, i, 0, 0))
    return pl.pallas_call(
        _peer_dense_kernel,
        grid=(t // tm, ne // te),
        in_specs=[pl.BlockSpec((d, tm), lambda i, j: (0, i)),
                  pl.BlockSpec((nb, rows // 2, d), lambda i, j: (j, 0, 0)),
                  pl.BlockSpec((te // 2, d), lambda i, j: (j, 0)),
                  small, small, big, big,
                  pl.BlockSpec((tm, d), lambda i, j: (i, 0)),
                  pl.BlockSpec((1, 1, d), lambda i, j: (i // tpb, 0, 0))],
        out_specs=pl.BlockSpec((tm, d), lambda i, j: (i, 0)),
        out_shape=jax.ShapeDtypeStruct((t, d), F32),
        scratch_shapes=[pltpu.VMEM((tm, d), F32), pltpu.VMEM((rows, tm), F32), pltpu.VMEM((rows, tm), F32),
                        pltpu.VMEM((te, tm), BF16)],
        compiler_params=_cparams(("arbitrary", "arbitrary")),
        name="peer_dense",
    )(hbt, u_pk.reshape(ne // rows, rows // 2, d), v_pk, e1, nrow, e2, rank2, x, g2)


def _final_norm_kernel(x_ref, g_ref, o_ref):
    x = x_ref[...]
    inv = lax.rsqrt(jnp.mean(x * x, axis=-1, keepdims=True) + EPS)
    o_ref[...] = (x * inv) * g_ref[...]


def _final_norm(x, gain, tm=1024):
    t, d = x.shape
    return pl.pallas_call(
        _final_norm_kernel,
        grid=(t // tm,),
        in_specs=[pl.BlockSpec((tm, d), lambda i: (i, 0)), pl.BlockSpec((1, d), lambda i: (0, 0))],
        out_specs=pl.BlockSpec((tm, d), lambda i: (i, 0)),
        out_shape=jax.ShapeDtypeStruct((t, d), F32),
        compiler_params=_cparams(("arbitrary",)),
        name="final_norm",
    )(x, gain.reshape(1, d))


def _peer_layer(x, gain, sc, sh, g2, w_q, keys, u_tabs, v_tabs, layer, seq):
    wq1, wq2 = _split2(w_q)
    kf = keys.reshape(-1, PEER_NKEYS, keys.shape[-1])
    k1, k2 = _split2(kf)
    hb, scores_t = _peer_q(x, gain, sc, sh, wq1, wq2, k1, k2, seq)
    e1, nrow, e2, rank2 = _peer_topk(scores_t)
    return _peer_dense(hb, _pack_rows(u_tabs, layer), _pack_rows(v_tabs, layer), e1, nrow, e2, rank2,
                       x, g2, seq)


def kernel(x, c, w_ada, b_ada, norm_g, hg_w_in, hg_lb, hg_norm_g, hg_w_out, lru_w_in, lru_conv_w,
           lru_conv_b, lru_w_a, lru_b_a, lru_w_x, lru_b_x, lru_lam, lru_w_out, peer_w_q, peer_keys,
           peer_u, peer_v, final_g):
    batch, seq, d = x.shape
    depth = w_ada.shape[0]
    n_mixers = 2
    mod = _ada_mod(c, w_ada, b_ada)
    xt = x.reshape(batch * seq, d)
    for i in range(depth):
        parts = [mod[i, :, k * d:(k + 1) * d].reshape(batch, 1, d) for k in range(6)]
        sh1, sc1, g1, sh2, sc2, g2 = parts
        j = i // n_mixers
        if i % n_mixers == 0:
            proj = _norm_mod_matmul(xt, norm_g[i, 0], sc1, sh1, hg_w_in[j].astype(BF16), seq,
                                    head_major=True)
            o_fw, o_bw = _hgrn_scan(proj, hg_lb, j, batch, seq)
            xt = _hgrn_out(o_fw, o_bw, proj, hg_norm_g[j], hg_w_out[j].astype(BF16), xt, g1, seq)
        else:
            proj = _norm_mod_matmul(xt, norm_g[i, 0], sc1, sh1, lru_w_in[j].astype(BF16), seq,
                                    head_major=False)
            hs = _lru_scan(proj, lru_conv_w[j], lru_conv_b[j], lru_w_a[j], lru_b_a[j], lru_w_x[j],
                           lru_b_x[j], lru_lam[j], batch, seq)
            xt = _lru_out(hs, proj, lru_w_out[j].astype(BF16), xt, g1, seq)
        xt = _peer_layer(xt, norm_g[i, 1], sc2, sh2, g2, peer_w_q[i], peer_keys[i], peer_u, peer_v,
                         i, seq)
    return _final_norm(xt, final_g).reshape(batch, seq, d)
```

```python
import functools

import numpy as np
import jax
import jax.numpy as jnp
from jax import lax
from jax.experimental import pallas as pl
from jax.experimental.pallas import tpu as pltpu

F32 = jnp.float32
BF16 = jnp.bfloat16

SUBLANES = 8
LANES = 128
VMEM_LIMIT = 56 * 1024 * 1024

EPS = 1e-6
HG_HEADS = 8
HG_CHUNK = 128
LRU_BLOCKS = 4
LRU_C = 8.0
CONV_W = 4
CONV_LEFT = 2
PEER_HEADS = 8
PEER_NKEYS = 128
PEER_TOPK = 16
HG_LEVELS = 7
NEG_INF = float("-inf")
LOG2_E = 1.4426950408889634


def _cparams(sem):
    return pltpu.CompilerParams(dimension_semantics=sem, vmem_limit_bytes=VMEM_LIMIT)


def _split3(x):
    x1 = x.astype(BF16)
    r1 = x - x1.astype(F32)
    x2 = r1.astype(BF16)
    x3 = (r1 - x2.astype(F32)).astype(BF16)
    return x1, x2, x3


def _split2(x):
    x1 = x.astype(BF16)
    x2 = (x - x1.astype(F32)).astype(BF16)
    return x1, x2


def _dot(a, b):
    return jnp.dot(a, b, preferred_element_type=F32)


def _dot_nt(a, b):
    return lax.dot_general(a, b, (((1,), (1,)), ((), ())), preferred_element_type=F32)


def _dot_tn(a, b):
    return lax.dot_general(a, b, (((0,), (0,)), ((), ())), preferred_element_type=F32)


def _norm_mod(x, gain, sc, sh):
    inv = lax.rsqrt(jnp.mean(x * x, axis=-1, keepdims=True) + EPS)
    return (x * inv) * gain * (1.0 + sc) + sh


def _ada_kernel(c_ref, w_ref, b_ref, o_ref):
    c = c_ref[...]
    cond = c * jax.nn.sigmoid(c)
    c1, c2, c3 = _split3(cond)
    w1, w2, w3 = _split3(w_ref[0])
    acc = _dot(c1, w1) + _dot(c1, w2) + _dot(c2, w1)
    acc = acc + _dot(c1, w3) + _dot(c2, w2) + _dot(c3, w1)
    o_ref[0] = acc + b_ref[0]


def _ada_mod(c, w_ada, b_ada):
    depth, d, n = w_ada.shape
    b = c.shape[0]
    cp = jnp.zeros((SUBLANES, d), F32).at[:b].set(c)
    tn = 1536
    out = pl.pallas_call(
        _ada_kernel,
        grid=(depth, n // tn),
        in_specs=[pl.BlockSpec((SUBLANES, d), lambda i, j: (0, 0)),
                  pl.BlockSpec((1, d, tn), lambda i, j: (i, 0, j)),
                  pl.BlockSpec((1, 1, tn), lambda i, j: (i, 0, j))],
        out_specs=pl.BlockSpec((1, SUBLANES, tn), lambda i, j: (i, 0, j)),
        out_shape=jax.ShapeDtypeStruct((depth, SUBLANES, n), F32),
        compiler_params=_cparams(("arbitrary", "arbitrary")),
        name="ada_mod",
    )(cp, w_ada, b_ada.reshape(depth, 1, n))
    return out[:, :b]


def _nmm_kernel(x_ref, g_ref, sc_ref, sh_ref, w_ref, o_ref, h_scr, *, head_major):
    @pl.when(pl.program_id(1) == 0)
    def _():
        h_scr[...] = _norm_mod(x_ref[...], g_ref[...], sc_ref[0], sh_ref[0]).astype(BF16)

    res = _dot(h_scr[...], w_ref[...])
    if head_major:
        for k in range(o_ref.shape[0]):
            o_ref[k] = res[:, k * LANES:(k + 1) * LANES]
    else:
        o_ref[...] = res


def _norm_mod_matmul(x, gain, sc, sh, w_bf16, seq, *, head_major, tm=1024, tn=1024):
    t, d = x.shape
    n = w_bf16.shape[1]
    tpb = seq // tm
    if head_major:
        out_shape = jax.ShapeDtypeStruct((n // LANES, t, LANES), F32)
        out_spec = pl.BlockSpec((tn // LANES, tm, LANES), lambda i, j: (j, i, 0))
    else:
        out_shape = jax.ShapeDtypeStruct((t, n), F32)
        out_spec = pl.BlockSpec((tm, tn), lambda i, j: (i, j))
    return pl.pallas_call(
        functools.partial(_nmm_kernel, head_major=head_major),
        grid=(t // tm, n // tn),
        in_specs=[pl.BlockSpec((tm, d), lambda i, j: (i, 0)),
                  pl.BlockSpec((1, d), lambda i, j: (0, 0)),
                  pl.BlockSpec((1, 1, d), lambda i, j: (i // tpb, 0, 0)),
                  pl.BlockSpec((1, 1, d), lambda i, j: (i // tpb, 0, 0)),
                  pl.BlockSpec((d, tn), lambda i, j: (0, j))],
        out_specs=out_spec,
        out_shape=out_shape,
        scratch_shapes=[pltpu.VMEM((tm, d), BF16)],
        compiler_params=_cparams(("arbitrary", "arbitrary")),
        name="norm_mod_matmul",
    )(x, gain.reshape(1, d), sc, sh, w_bf16)


def _hgrn_sum_mats(reverse):
    c = HG_CHUNK
    mats = []
    for lvl in range(2, HG_LEVELS + 1):
        m = 1 << lvl
        half = m >> 1
        a = np.zeros((c, c), np.float32)
        for t in range(c):
            mid = (t // m) * m + half
            if not reverse:
                if t >= mid:
                    a[t, mid:t + 1] = 1.0
                else:
                    a[t, t + 1:mid] = 1.0
            else:
                if t < mid:
                    a[t, t:mid] = 1.0
                else:
                    a[t, mid:t] = 1.0
        mats.append(a)
    ones = np.ones((c, c), np.float32)
    mats.append(np.triu(ones) if reverse else np.tril(ones))
    return np.concatenate(mats, axis=0)


def _hgrn_chain(q, flog, v, lb, a_ref, st_ref, d, h, reverse):
    c = HG_CHUNK
    f = lb + (1.0 - lb) * jax.nn.sigmoid(flog)
    g = jnp.log(f) * LOG2_E
    k = 1.0 - f
    g1, g2 = _split2(g)
    e2 = _dot(a_ref[...], jnp.concatenate([g1, g2], axis=1))
    e = e2[:, :LANES] + e2[:, LANES:]
    xdec = jnp.exp2(e)
    rows = lax.broadcasted_iota(jnp.int32, (c, c), 0)
    cols = lax.broadcasted_iota(jnp.int32, (c, c), 1)
    attn = jnp.where(rows == cols, jnp.sum(q * k, axis=-1, keepdims=True), 0.0)
    for lvl in range(1, HG_LEVELS + 1):
        row_q = (((rows >> (lvl - 1)) & 1) == 1) != reverse
        col_k = (((cols >> (lvl - 1)) & 1) == 0) != reverse
        if lvl == 1:
            mixed = jnp.where(row_q, q * f, k).astype(BF16)
        else:
            mixed = (jnp.where(row_q, q, k) * xdec[(lvl - 2) * c:(lvl - 1) * c]).astype(BF16)
        p = _dot_nt(mixed, mixed)
        pick = jnp.logical_and((rows >> lvl) == (cols >> lvl), jnp.logical_and(row_q, col_k))
        attn = jnp.where(pick, p, attn)
    eq = e[(HG_LEVELS - 1) * c:HG_LEVELS * c]
    tot = eq[0:1] if reverse else eq[c - 1:c]
    qd = (q * xdec[(HG_LEVELS - 1) * c:HG_LEVELS * c]).astype(BF16)
    kd = (k * jnp.exp2(tot - eq)).astype(BF16)
    st = st_ref[d, h]
    vb = v.astype(BF16)
    o = _dot(attn.astype(BF16), vb) + _dot_nt(qd, st.astype(BF16))
    st_ref[d, h] = st * jnp.exp2(tot) + _dot_tn(vb, kd)
    return o


def _hgrn_scan_kernel(lb_ref, afw_ref, abw_ref, qf_ref, ff_ref, vf_ref, qb_ref, fb_ref, vb_ref,
                      of_ref, ob_ref, st_ref, lbs_ref, *, layer):
    @pl.when(pl.program_id(1) == 0)
    def _():
        st_ref[...] = jnp.zeros_like(st_ref)
        lb = lb_ref[...]
        ex = jnp.exp(lb - jnp.max(lb, axis=0, keepdims=True))
        p = ex / jnp.sum(ex, axis=0, keepdims=True)
        acc = p[0]
        for jj in range(1, layer + 1):
            acc = acc + p[jj]
        lbs_ref[...] = acc - p[0]

    def body(h, carry):
        lb = lbs_ref[pl.ds(h, 1), :]
        of_ref[h] = _hgrn_chain(qf_ref[h], ff_ref[h], vf_ref[h], lb, afw_ref, st_ref, 0, h, False)
        ob_ref[h] = _hgrn_chain(qb_ref[h], fb_ref[h], vb_ref[h], lb, abw_ref, st_ref, 1, h, True)
        return carry

    lax.fori_loop(0, HG_HEADS, body, 0, unroll=8)


def _hgrn_scan(proj_hm, hg_lb, layer, batch, seq):
    hh = HG_HEADS
    t = proj_hm.shape[1]
    nc = seq // HG_CHUNK
    c = HG_CHUNK
    na = hg_lb.shape[0]
    afw = jnp.asarray(_hgrn_sum_mats(False), BF16)
    abw = jnp.asarray(_hgrn_sum_mats(True), BF16)
    nrow = afw.shape[0]

    def fwd(part):
        return pl.BlockSpec((hh, c, LANES), lambda b, cc: (part, b * nc + cc, 0))

    def bwd(part):
        return pl.BlockSpec((hh, c, LANES), lambda b, cc: (part, b * nc + nc - 1 - cc, 0))

    out_sd = jax.ShapeDtypeStruct((hh, t, LANES), F32)
    return pl.pallas_call(
        functools.partial(_hgrn_scan_kernel, layer=layer),
        grid=(batch, nc),
        in_specs=[pl.BlockSpec((na, hh, LANES), lambda b, cc: (0, 0, 0)),
                  pl.BlockSpec((nrow, c), lambda b, cc: (0, 0)),
                  pl.BlockSpec((nrow, c), lambda b, cc: (0, 0)),
                  fwd(0), fwd(1), fwd(3), bwd(0), bwd(2), bwd(3)],
        out_specs=[pl.BlockSpec((hh, c, LANES), lambda b, cc: (0, b * nc + cc, 0)),
                   pl.BlockSpec((hh, c, LANES), lambda b, cc: (0, b * nc + nc - 1 - cc, 0))],
        out_shape=[out_sd, out_sd],
        scratch_shapes=[pltpu.VMEM((2, hh, LANES, LANES), F32), pltpu.VMEM((hh, LANES), F32)],
        compiler_params=_cparams(("arbitrary", "arbitrary")),
        name="hgrn_scan",
    )(hg_lb.reshape(na, hh, LANES), afw, abw, proj_hm, proj_hm, proj_hm, proj_hm, proj_hm, proj_hm)


def _hgrn_out_kernel(of_ref, ob_ref, gg_ref, ng_ref, w_ref, x_ref, g1_ref, o_ref):
    parts = []
    for h in range(HG_HEADS):
        o = of_ref[h] + ob_ref[h]
        inv = lax.rsqrt(jnp.mean(o * o, axis=-1, keepdims=True) + EPS)
        gg = gg_ref[h]
        parts.append(((o * inv) * ng_ref[pl.ds(h, 1), :] * (gg * jax.nn.sigmoid(gg))).astype(BF16))
    y = _dot(jnp.concatenate(parts, axis=-1), w_ref[...])
    o_ref[...] = x_ref[...] + g1_ref[0] * y


def _hgrn_out(o_fw, o_bw, proj_hm, norm_g, w_out_bf16, x, g1, seq, tm=512):
    hh = HG_HEADS
    t, d = x.shape
    tpb = seq // tm
    return pl.pallas_call(
        _hgrn_out_kernel,
        grid=(t // tm,),
        in_specs=[pl.BlockSpec((hh, tm, LANES), lambda i: (0, i, 0)),
                  pl.BlockSpec((hh, tm, LANES), lambda i: (0, i, 0)),
                  pl.BlockSpec((hh, tm, LANES), lambda i: (4, i, 0)),
                  pl.BlockSpec((hh, LANES), lambda i: (0, 0)),
                  pl.BlockSpec((d, d), lambda i: (0, 0)),
                  pl.BlockSpec((tm, d), lambda i: (i, 0)),
                  pl.BlockSpec((1, 1, d), lambda i: (i // tpb, 0, 0))],
        out_specs=pl.BlockSpec((tm, d), lambda i: (i, 0)),
        out_shape=jax.ShapeDtypeStruct((t, d), F32),
        compiler_params=_cparams(("arbitrary",)),
        name="hgrn_out",
    )(o_fw, o_bw, proj_hm, norm_g.reshape(hh, LANES), w_out_bf16, x, g1)


def _lru_group_scan(a, x, carry, rows, reverse):
    for s in (1, 2, 4):
        if not reverse:
            keep = rows >= s
            a_sh = jnp.where(keep, pltpu.roll(a, s, 0), 1.0)
            x_sh = jnp.where(keep, pltpu.roll(x, s, 0), 0.0)
        else:
            keep = rows < SUBLANES - s
            a_sh = jnp.where(keep, pltpu.roll(a, SUBLANES - s, 0), 1.0)
            x_sh = jnp.where(keep, pltpu.roll(x, SUBLANES - s, 0), 0.0)
        x = x + a * x_sh
        a = a * a_sh
    hs = x + a * carry
    new_carry = hs[0:1] if reverse else hs[SUBLANES - 1:SUBLANES]
    return hs, new_carry


def _lru_conv_kernel(xc_ref, xp_ref, xn_ref, cw_ref, cb_ref, o_ref, *, tm, nchunk):
    chunk = pl.program_id(0) % nchunk
    prev = jnp.where(chunk == 0, 0.0, xp_ref[...])
    nxt = jnp.where(chunk == nchunk - 1, 0.0, xn_ref[...])
    ext = jnp.concatenate([prev, xc_ref[...], nxt], axis=0)
    cw = cw_ref[...]
    xc = cb_ref[...]
    for j in range(CONV_W):
        off = SUBLANES - CONV_LEFT + j
        xc = xc + ext[off:off + tm] * cw[j:j + 1]
    o_ref[...] = xc


def _lru_conv(proj, conv_w, conv_b, seq, tm=512):
    t = proj.shape[0]
    w = proj.shape[1] // 2
    nchunk = seq // tm
    hb = tm // SUBLANES
    nhalo = t // SUBLANES
    return pl.pallas_call(
        functools.partial(_lru_conv_kernel, tm=tm, nchunk=nchunk),
        grid=(t // tm,),
        in_specs=[pl.BlockSpec((tm, w), lambda i: (i, 0)),
                  pl.BlockSpec((SUBLANES, w), lambda i: (jnp.maximum(i * hb - 1, 0), 0)),
                  pl.BlockSpec((SUBLANES, w), lambda i: (jnp.minimum((i + 1) * hb, nhalo - 1), 0)),
                  pl.BlockSpec((CONV_W, w), lambda i: (0, 0)),
                  pl.BlockSpec((1, w), lambda i: (0, 0))],
        out_specs=pl.BlockSpec((tm, w), lambda i: (i, 0)),
        out_shape=jax.ShapeDtypeStruct((t, w), F32),
        compiler_params=_cparams(("arbitrary",)),
        name="lru_conv",
    )(proj, proj, proj, conv_w, conv_b.reshape(1, w))


def _lru_kernel(xc_ref, wa_ref, ba_ref, wx_ref, bx_ref, lam_ref, o_ref, a_scr, b_scr, carry_scr, *, tm):
    d = pl.program_id(0)

    @pl.when(pl.program_id(2) == 0)
    def _():
        carry_scr[...] = jnp.zeros_like(carry_scr)

    xc = xc_ref[...]
    xcb = xc.astype(BF16)
    bw = xcb.shape[1] // LRU_BLOCKS
    ra, rx = [], []
    for n in range(LRU_BLOCKS):
        blk = xcb[:, n * bw:(n + 1) * bw]
        ra.append(_dot(blk, wa_ref[0, n]))
        rx.append(_dot(blk, wx_ref[0, n]))
    r = jax.nn.sigmoid(jnp.concatenate(ra, axis=-1) + ba_ref[0])
    ig = jax.nn.sigmoid(jnp.concatenate(rx, axis=-1) + bx_ref[0])
    lam = lam_ref[0]
    softplus_neg = jnp.maximum(-lam, 0.0) + jnp.log(1.0 + jnp.exp(-jnp.abs(lam)))
    log_a = -LRU_C * r * softplus_neg
    a = jnp.exp(log_a)
    a_scr[...] = a
    b_scr[...] = jnp.sqrt(-jnp.tanh(log_a) * (a * a + 1.0)) * ig * xc

    ngroups = tm // SUBLANES
    rows = lax.broadcasted_iota(jnp.int32, (SUBLANES, xc.shape[1]), 0)

    def run(reverse):
        def body(i, carry):
            gi = (ngroups - 1 - i) if reverse else i
            sl = pl.ds(pl.multiple_of(gi * SUBLANES, SUBLANES), SUBLANES)
            hs, carry = _lru_group_scan(a_scr[sl, :], b_scr[sl, :], carry, rows, reverse)
            o_ref[0, sl, :] = hs
            return carry
        carry_scr[...] = lax.fori_loop(0, ngroups, body, carry_scr[...])

    @pl.when(d == 0)
    def _():
        run(False)

    @pl.when(d == 1)
    def _():
        run(True)


def _lru_scan(xc, w_a, b_a, w_x, b_x, lam, batch, seq, tm=256):
    t, w = xc.shape
    nchunk = seq // tm

    def pos(d, b, cc):
        return b * nchunk + jnp.where(d == 0, cc, nchunk - 1 - cc)

    return pl.pallas_call(
        functools.partial(_lru_kernel, tm=tm),
        grid=(2, batch, nchunk),
        in_specs=[pl.BlockSpec((tm, w), lambda d, b, cc: (pos(d, b, cc), 0)),
                  pl.BlockSpec((1, LRU_BLOCKS, w // LRU_BLOCKS, w // LRU_BLOCKS), lambda d, b, cc: (d, 0, 0, 0)),
                  pl.BlockSpec((1, 1, w), lambda d, b, cc: (d, 0, 0)),
                  pl.BlockSpec((1, LRU_BLOCKS, w // LRU_BLOCKS, w // LRU_BLOCKS), lambda d, b, cc: (d, 0, 0, 0)),
                  pl.BlockSpec((1, 1, w), lambda d, b, cc: (d, 0, 0)),
                  pl.BlockSpec((1, 1, w), lambda d, b, cc: (d, 0, 0))],
        out_specs=pl.BlockSpec((1, tm, w), lambda d, b, cc: (d, pos(d, b, cc), 0)),
        out_shape=jax.ShapeDtypeStruct((2, t, w), F32),
        scratch_shapes=[pltpu.VMEM((tm, w), F32), pltpu.VMEM((tm, w), F32), pltpu.VMEM((1, w), F32)],
        compiler_params=_cparams(("arbitrary", "arbitrary", "arbitrary")),
        name="lru_scan",
    )(xc, w_a.astype(BF16), b_a.reshape(2, 1, w), w_x.astype(BF16), b_x.reshape(2, 1, w), lam.reshape(2, 1, w))


def _lru_out_kernel(hs_ref, yb_ref, w_ref, x_ref, g1_ref, o_ref):
    u = (hs_ref[0] + hs_ref[1]) * jax.nn.gelu(yb_ref[...])
    o_ref[...] = x_ref[...] + g1_ref[0] * _dot(u.astype(BF16), w_ref[...])


def _lru_out(hs, proj, w_out_bf16, x, g1, seq, tm=512):
    t, d = x.shape
    w = hs.shape[2]
    tpb = seq // tm
    return pl.pallas_call(
        _lru_out_kernel,
        grid=(t // tm,),
        in_specs=[pl.BlockSpec((2, tm, w), lambda i: (0, i, 0)),
                  pl.BlockSpec((tm, w), lambda i: (i, 1)),
                  pl.BlockSpec((w, d), lambda i: (0, 0)),
                  pl.BlockSpec((tm, d), lambda i: (i, 0)),
                  pl.BlockSpec((1, 1, d), lambda i: (i // tpb, 0, 0))],
        out_specs=pl.BlockSpec((tm, d), lambda i: (i, 0)),
        out_shape=jax.ShapeDtypeStruct((t, d), F32),
        compiler_params=_cparams(("arbitrary",)),
        name="lru_out",
    )(hs, proj, w_out_bf16, x, g1)


def _peer_q_kernel(x_ref, g_ref, sc_ref, sh_ref, w1_ref, w2_ref, k1_ref, k2_ref, hb_ref, sc_out_ref):
    h = _norm_mod(x_ref[...], g_ref[...], sc_ref[0], sh_ref[0])
    h1, h2 = _split2(h)
    hb_ref[...] = h.T.astype(BF16)
    q = _dot(h1, w1_ref[...]) + _dot(h1, w2_ref[...]) + _dot(h2, w1_ref[...])
    nhp = k1_ref.shape[0]
    nch = sc_out_ref.shape[1]
    for hp in range(nhp):
        qa, qb = _split2(q[:, hp * LANES:(hp + 1) * LANES])
        s = _dot_nt(k1_ref[hp], qa) + _dot_nt(k1_ref[hp], qb) + _dot_nt(k2_ref[hp], qa)
        for ch in range(nch):
            sc_out_ref[hp, ch] = s[:, ch * LANES:(ch + 1) * LANES]


def _peer_q(x, gain, sc, sh, wq1, wq2, keys1, keys2, seq, tm=256):
    t, d = x.shape
    nq = wq1.shape[1]
    nhp = keys1.shape[0]
    tpb = seq // tm
    nch = tm // LANES
    return pl.pallas_call(
        _peer_q_kernel,
        grid=(t // tm,),
        in_specs=[pl.BlockSpec((tm, d), lambda i: (i, 0)),
                  pl.BlockSpec((1, d), lambda i: (0, 0)),
                  pl.BlockSpec((1, 1, d), lambda i: (i // tpb, 0, 0)),
                  pl.BlockSpec((1, 1, d), lambda i: (i // tpb, 0, 0)),
                  pl.BlockSpec((d, nq), lambda i: (0, 0)),
                  pl.BlockSpec((d, nq), lambda i: (0, 0)),
                  pl.BlockSpec((nhp, PEER_NKEYS, LANES), lambda i: (0, 0, 0)),
                  pl.BlockSpec((nhp, PEER_NKEYS, LANES), lambda i: (0, 0, 0))],
        out_specs=[pl.BlockSpec((d, tm), lambda i: (0, i)),
                   pl.BlockSpec((nhp, nch, PEER_NKEYS, LANES), lambda i: (0, i, 0, 0))],
        out_shape=[jax.ShapeDtypeStruct((d, t), BF16),
                   jax.ShapeDtypeStruct((nhp, t // LANES, PEER_NKEYS, LANES), F32)],
        compiler_params=_cparams(("arbitrary",)),
        name="peer_q",
    )(x, gain.reshape(1, d), sc, sh, wq1, wq2, keys1, keys2)


def _top16(s, rows, exact):
    work = s
    rank = jnp.full(s.shape, PEER_TOPK, jnp.int32)
    vals = []
    for kk in range(PEER_TOPK):
        m = jnp.max(work, axis=0, keepdims=True)
        if exact:
            idx = jnp.min(jnp.where(work == m, rows, PEER_NKEYS), axis=0, keepdims=True)
            sel = rows == idx
        else:
            sel = work == m
        rank = jnp.where(sel, kk, rank)
        work = jnp.where(sel, NEG_INF, work)
        vals.append(m)
    return rank, jnp.concatenate(vals, axis=0)


def _col_count(mask):
    return jnp.sum(jnp.where(mask, 1.0, 0.0), axis=0, keepdims=True)


def _any_lane_differs(count, want):
    return jnp.max(jnp.where(count == want, 0.0, 1.0)) > 0.5


def _top16_values(s):
    work = s
    vals = []
    for _ in range(PEER_TOPK):
        m = jnp.max(work, axis=0, keepdims=True)
        work = jnp.where(work == m, NEG_INF, work)
        vals.append(m)
    return jnp.concatenate(vals, axis=0), work


def _cand16(cand, flat, r16, mtop):
    cnt = jnp.zeros((PEER_TOPK, cand.shape[1]), F32)
    z = jnp.zeros((1, cand.shape[1]), F32)
    for _ in range(PEER_TOPK):
        m = jnp.max(cand, axis=0, keepdims=True)
        idx = jnp.min(jnp.where(cand == m, flat, PEER_TOPK * PEER_TOPK), axis=0, keepdims=True)
        cand = jnp.where(flat == idx, NEG_INF, cand)
        cnt = cnt + jnp.where(r16 == (idx >> 4), 1.0, 0.0)
        z = z + jnp.exp(m - mtop)
    return cnt, z


def _bf16_pair(x):
    u = pltpu.bitcast(x.astype(BF16).astype(F32), jnp.uint32)
    return u | (u >> 16)


PEER_TOPK_GROUP = 4


def _peer_topk_kernel(s_ref, e1_ref, n_ref, e2_ref, r2_ref):
    nch = s_ref.shape[1]
    grp = PEER_TOPK_GROUP
    width = grp * LANES
    rows = lax.broadcasted_iota(jnp.int32, (PEER_NKEYS, width), 0)
    r16 = lax.broadcasted_iota(jnp.int32, (PEER_TOPK, width), 0)
    r8 = lax.broadcasted_iota(jnp.int32, (SUBLANES, width), 0)
    flat = jnp.concatenate([r16] + [r8 + PEER_TOPK * k1 for k1 in range(1, 8)]
                           + [(r8 + 8) * PEER_TOPK], axis=0)
    want = float(PEER_TOPK)

    def body(i, carry):
        chunks = [i * grp + k for k in range(grp)]
        s1 = jnp.concatenate([s_ref[0, ch] for ch in chunks], axis=1)
        s2 = jnp.concatenate([s_ref[1, ch] for ch in chunks], axis=1)
        def candidates(v1, v2):
            return jnp.concatenate([v1[0:1] + v2] + [v1[k1:k1 + 1] + v2[0:8] for k1 in range(1, 8)]
                                   + [v1[8:16] + v2[0:1]], axis=0)

        def exact():
            rank1, v1 = _top16(s1, rows, True)
            rank2, v2 = _top16(s2, rows, True)
            mtop = v1[0:1] + v2[0:1]
            cnt, z = _cand16(candidates(v1, v2), flat, r16, mtop)
            nrow = jnp.zeros((PEER_NKEYS, width), F32)
            for kk in range(PEER_TOPK):
                nrow = jnp.where(rank1 == kk, cnt[kk:kk + 1], nrow)
            e1 = jnp.where(rank1 < PEER_TOPK, jnp.exp(s1 - v1[0:1]) / z, 0.0)
            return nrow, e1, rank2, v2

        v1, left1 = _top16_values(s1)
        rank2, v2 = _top16(s2, rows, False)
        cand = candidates(v1, v2)
        mtop = v1[0:1] + v2[0:1]
        z = jnp.zeros((1, width), F32)
        tau = mtop
        for _ in range(PEER_TOPK):
            tau = jnp.max(cand, axis=0, keepdims=True)
            cand = jnp.where(cand == tau, NEG_INF, cand)
            z = z + jnp.exp(tau - mtop)
        tied = jnp.logical_or(
            jnp.logical_or(_any_lane_differs(_col_count(left1 == NEG_INF), want),
                           _any_lane_differs(_col_count(rank2 < PEER_TOPK), want)),
            _any_lane_differs(_col_count(cand == NEG_INF), want))

        def has(vrow):
            return (s1 + vrow) >= tau

        c16 = has(v2[15:16])
        c8 = has(v2[7:8])
        c4 = has(jnp.where(c8, v2[11:12], v2[3:4]))
        c2 = has(jnp.where(c8, jnp.where(c4, v2[13:14], v2[9:10]), jnp.where(c4, v2[5:6], v2[1:2])))
        hi = jnp.where(c4, jnp.where(c2, v2[14:15], v2[12:13]), jnp.where(c2, v2[10:11], v2[8:9]))
        lo = jnp.where(c4, jnp.where(c2, v2[6:7], v2[4:5]), jnp.where(c2, v2[2:3], v2[0:1]))
        c1 = has(jnp.where(c8, hi, lo))
        nrow = (jnp.where(c8, 8.0, 0.0) + jnp.where(c4, 4.0, 0.0)) + (jnp.where(c2, 2.0, 0.0) + jnp.where(c1, 1.0, 0.0))
        nrow = jnp.where(c16, float(PEER_TOPK), nrow)
        e1 = jnp.exp(s1 - v1[0:1]) / z
        tied = jnp.logical_or(tied, _any_lane_differs(jnp.sum(nrow, axis=0, keepdims=True), want))

        nrow, e1, rank2, v2 = lax.cond(tied, exact, lambda: (nrow, e1, rank2, v2))
        e1 = _bf16_pair(e1)
        nrow = _bf16_pair(nrow)
        e2 = jnp.where(rank2 < PEER_TOPK, jnp.exp(s2 - v2[0:1]), 0.0)
        e2 = pltpu.bitcast(e2.astype(BF16), jnp.uint32)
        r2 = pltpu.bitcast(rank2.astype(F32).astype(BF16), jnp.uint32)
        for k, ch in enumerate(chunks):
            cols = slice(k * LANES, (k + 1) * LANES)
            e1_ref[0, ch] = e1[:, cols]
            n_ref[0, ch] = nrow[:, cols]
            e2_ref[0, ch] = e2[:, cols]
            r2_ref[0, ch] = r2[:, cols]
        return carry

    lax.fori_loop(0, nch // grp, body, 0)


def _peer_topk(scores_t, nch=4):
    nhp, ntc, nk, _ = scores_t.shape
    hh = nhp // 2
    spec = pl.BlockSpec((1, nch, nk, LANES), lambda i, h: (h, i, 0, 0))
    spec_p = pl.BlockSpec((1, nch, nk // 2, LANES), lambda i, h: (h, i, 0, 0))
    sd_u = jax.ShapeDtypeStruct((hh, ntc, nk, LANES), jnp.uint32)
    sd_b = jax.ShapeDtypeStruct((hh, ntc, nk // 2, LANES), jnp.uint32)
    return pl.pallas_call(
        _peer_topk_kernel,
        grid=(ntc // nch, hh),
        in_specs=[pl.BlockSpec((2, nch, nk, LANES), lambda i, h: (h, i, 0, 0))],
        out_specs=[spec, spec, spec_p, spec_p],
        out_shape=[sd_u, sd_u, sd_b, sd_b],
        compiler_params=_cparams(("arbitrary", "arbitrary")),
        name="peer_topk",
    )(scores_t)


def _dup_rows(row_u32, nrows):
    return pltpu.bitcast(jnp.broadcast_to(row_u32, (nrows // 2, row_u32.shape[1])), BF16)


PEER_PB = 2


GELU_C1 = 0.7978845608028654
GELU_C2 = GELU_C1 * 0.044715


def _gelu_times(x, w):
    hw = (0.5 * x) * w
    return hw + hw * jnp.tanh(x * (GELU_C1 + GELU_C2 * (x * x)))


def _peer_dense_kernel(hbt_ref, u_ref, v_ref, e1_ref, n_ref, e2_ref, r2_ref, x_ref, g2_ref,
                       o_ref, acc_ref, act0, act1, ab_ref):
    j = pl.program_id(1)

    @pl.when(j == 0)
    def _():
        acc_ref[...] = jnp.zeros_like(acc_ref)

    nch = e2_ref.shape[1]
    nb = u_ref.shape[0]
    acts = (act0, act1)
    zero = jnp.zeros((), BF16)
    blk_rows = PEER_PB * PEER_NKEYS

    def act_mm(p):
        acts[p % 2][...] = _dot(pltpu.bitcast(u_ref[p], BF16), hbt_ref[...])

    def gate(p):
        for ch in range(nch):
            for a2 in range(PEER_PB):
                a = p * PEER_PB + a2
                w = jnp.zeros((PEER_NKEYS, LANES), BF16)
                for h in range(PEER_HEADS):
                    keep = pltpu.bitcast(r2_ref[h, ch], BF16) < _dup_rows(n_ref[h, ch, a:a + 1, :], PEER_NKEYS)
                    w = w + (jnp.where(keep, pltpu.bitcast(e2_ref[h, ch], BF16), zero)
                             * _dup_rows(e1_ref[h, ch, a:a + 1, :], PEER_NKEYS))
                rows = slice(a2 * PEER_NKEYS, (a2 + 1) * PEER_NKEYS)
                cols = slice(ch * LANES, (ch + 1) * LANES)
                out_rows = slice(p * blk_rows + a2 * PEER_NKEYS, p * blk_rows + (a2 + 1) * PEER_NKEYS)
                ab_ref[out_rows, cols] = _gelu_times(acts[p % 2][rows, cols].astype(BF16), w)

    act_mm(0)
    for p in range(nb):
        if p + 1 < nb:
            act_mm(p + 1)
        gate(p)
    acc_ref[...] += _dot_tn(ab_ref[...], pltpu.bitcast(v_ref[...], BF16))

    @pl.when(j == pl.num_programs(1) - 1)
    def _():
        o_ref[...] = x_ref[...] + g2_ref[0] * acc_ref[...]


def _pack_rows_kernel(x_ref, o_ref):
    o_ref[...] = pltpu.bitcast(x_ref[...].astype(BF16), jnp.uint32)


def _pack_rows(tabs, layer, tr=1024):
    _, n, d = tabs.shape
    return pl.pallas_call(
        _pack_rows_kernel,
        grid=(n // tr,),
        in_specs=[pl.BlockSpec((None, tr, d), lambda i: (layer, i, 0))],
        out_specs=pl.BlockSpec((tr // 2, d), lambda i: (i, 0)),
        out_shape=jax.ShapeDtypeStruct((n // 2, d), jnp.uint32),
        compiler_params=_cparams(("arbitrary",)),
        name="pack_rows",
    )(tabs)


def _peer_dense(hbt, u_pk, v_pk, e1, nrow, e2, rank2, x, g2, seq, tm=512, na=16):
    t, d = x.shape
    ne = 2 * u_pk.shape[0]
    rows = PEER_PB * PEER_NKEYS
    nb = na // PEER_PB
    te = nb * rows
    nch = tm // LANES
    tpb = seq // tm
    hh = PEER_HEADS
    small = pl.BlockSpec((hh, nch, na, LANES), lambda i, j: (0, i, j, 0))
    big = pl.BlockSpec((hh, nch, PEER_NKEYS // 2, LANES), lambda i, j: (0, i, 0, 0))
    return pl.pallas_call(
        _peer_dense_kernel,
        grid=(t // tm, ne // te),
        in_specs=[pl.BlockSpec((d, tm), lambda i, j: (0, i)),
                  pl.BlockSpec((nb, rows // 2, d), lambda i, j: (j, 0, 0)),
                  pl.BlockSpec((te // 2, d), lambda i, j: (j, 0)),
                  small, small, big, big,
                  pl.BlockSpec((tm, d), lambda i, j: (i, 0)),
                  pl.BlockSpec((1, 1, d), lambda i, j: (i // tpb, 0, 0))],
        out_specs=pl.BlockSpec((tm, d), lambda i, j: (i, 0)),
        out_shape=jax.ShapeDtypeStruct((t, d), F32),
        scratch_shapes=[pltpu.VMEM((tm, d), F32), pltpu.VMEM((rows, tm), F32), pltpu.VMEM((rows, tm), F32),
                        pltpu.VMEM((te, tm), BF16)],
        compiler_params=_cparams(("arbitrary", "arbitrary")),
        name="peer_dense",
    )(hbt, u_pk.reshape(ne // rows, rows // 2, d), v_pk, e1, nrow, e2, rank2, x, g2)


def _final_norm_kernel(x_ref, g_ref, o_ref):
    x = x_ref[...]
    inv = lax.rsqrt(jnp.mean(x * x, axis=-1, keepdims=True) + EPS)
    o_ref[...] = (x * inv) * g_ref[...]


def _final_norm(x, gain, tm=1024):
    t, d = x.shape
    return pl.pallas_call(
        _final_norm_kernel,
        grid=(t // tm,),
        in_specs=[pl.BlockSpec((tm, d), lambda i: (i, 0)), pl.BlockSpec((1, d), lambda i: (0, 0))],
        out_specs=pl.BlockSpec((tm, d), lambda i: (i, 0)),
        out_shape=jax.ShapeDtypeStruct((t, d), F32),
        compiler_params=_cparams(("arbitrary",)),
        name="final_norm",
    )(x, gain.reshape(1, d))


def _peer_layer(x, gain, sc, sh, g2, w_q, keys, u_tabs, v_tabs, layer, seq):
    wq1, wq2 = _split2(w_q)
    kf = keys.reshape(-1, PEER_NKEYS, keys.shape[-1])
    k1, k2 = _split2(kf)
    hb, scores_t = _peer_q(x, gain, sc, sh, wq1, wq2, k1, k2, seq)
    e1, nrow, e2, rank2 = _peer_topk(scores_t)
    return _peer_dense(hb, _pack_rows(u_tabs, layer), _pack_rows(v_tabs, layer), e1, nrow, e2, rank2,
                       x, g2, seq)


def kernel(x, c, w_ada, b_ada, norm_g, hg_w_in, hg_lb, hg_norm_g, hg_w_out, lru_w_in, lru_conv_w,
           lru_conv_b, lru_w_a, lru_b_a, lru_w_x, lru_b_x, lru_lam, lru_w_out, peer_w_q, peer_keys,
           peer_u, peer_v, final_g):
    batch, seq, d = x.shape
    depth = w_ada.shape[0]
    n_mixers = 2
    mod = _ada_mod(c, w_ada, b_ada)
    xt = x.reshape(batch * seq, d)
    for i in range(depth):
        parts = [mod[i, :, k * d:(k + 1) * d].reshape(batch, 1, d) for k in range(6)]
        sh1, sc1, g1, sh2, sc2, g2 = parts
        j = i // n_mixers
        if i % n_mixers == 0:
            proj = _norm_mod_matmul(xt, norm_g[i, 0], sc1, sh1, hg_w_in[j].astype(BF16), seq,
                                    head_major=True)
            o_fw, o_bw = _hgrn_scan(proj, hg_lb, j, batch, seq)
            xt = _hgrn_out(o_fw, o_bw, proj, hg_norm_g[j], hg_w_out[j].astype(BF16), xt, g1, seq)
        else:
            proj = _norm_mod_matmul(xt, norm_g[i, 0], sc1, sh1, lru_w_in[j].astype(BF16), seq,
                                    head_major=False)
            xc = _lru_conv(proj, lru_conv_w[j], lru_conv_b[j], seq)
            hs = _lru_scan(xc, lru_w_a[j], lru_b_a[j], lru_w_x[j], lru_b_x[j], lru_lam[j], batch, seq)
            xt = _lru_out(hs, proj, lru_w_out[j].astype(BF16), xt, g1, seq)
        xt = _peer_layer(xt, norm_g[i, 1], sc2, sh2, g2, peer_w_q[i], peer_keys[i], peer_u, peer_v,
                         i, seq)
    return _final_norm(xt, final_g).reshape(batch, seq, d)
```

```python
import functools

import numpy as np
import jax
import jax.numpy as jnp
from jax import lax
from jax.experimental import pallas as pl
from jax.experimental.pallas import tpu as pltpu

F32 = jnp.float32
BF16 = jnp.bfloat16

SUBLANES = 8
LANES = 128
VMEM_LIMIT = 56 * 1024 * 1024

EPS = 1e-6
HG_HEADS = 8
HG_CHUNK = 128
LRU_BLOCKS = 4
LRU_C = 8.0
CONV_W = 4
CONV_LEFT = 2
PEER_HEADS = 8
PEER_NKEYS = 128
PEER_TOPK = 16
HG_LEVELS = 7
NEG_INF = float("-inf")
LOG2_E = 1.4426950408889634


def _cparams(sem):
    return pltpu.CompilerParams(dimension_semantics=sem, vmem_limit_bytes=VMEM_LIMIT)


def _split3(x):
    x1 = x.astype(BF16)
    r1 = x - x1.astype(F32)
    x2 = r1.astype(BF16)
    x3 = (r1 - x2.astype(F32)).astype(BF16)
    return x1, x2, x3


def _split2(x):
    x1 = x.astype(BF16)
    x2 = (x - x1.astype(F32)).astype(BF16)
    return x1, x2


def _dot(a, b):
    return jnp.dot(a, b, preferred_element_type=F32)


def _dot_nt(a, b):
    return lax.dot_general(a, b, (((1,), (1,)), ((), ())), preferred_element_type=F32)


def _dot_tn(a, b):
    return lax.dot_general(a, b, (((0,), (0,)), ((), ())), preferred_element_type=F32)


def _norm_mod(x, gain, sc, sh):
    inv = lax.rsqrt(jnp.mean(x * x, axis=-1, keepdims=True) + EPS)
    return (x * inv) * gain * (1.0 + sc) + sh


def _ada_kernel(c_ref, w_ref, b_ref, o_ref):
    c = c_ref[...]
    cond = c * jax.nn.sigmoid(c)
    c1, c2, c3 = _split3(cond)
    w1, w2, w3 = _split3(w_ref[0])
    acc = _dot(c1, w1) + _dot(c1, w2) + _dot(c2, w1)
    acc = acc + _dot(c1, w3) + _dot(c2, w2) + _dot(c3, w1)
    o_ref[0] = acc + b_ref[0]


def _ada_mod(c, w_ada, b_ada):
    depth, d, n = w_ada.shape
    b = c.shape[0]
    cp = jnp.zeros((SUBLANES, d), F32).at[:b].set(c)
    tn = 1536
    out = pl.pallas_call(
        _ada_kernel,
        grid=(depth, n // tn),
        in_specs=[pl.BlockSpec((SUBLANES, d), lambda i, j: (0, 0)),
                  pl.BlockSpec((1, d, tn), lambda i, j: (i, 0, j)),
                  pl.BlockSpec((1, 1, tn), lambda i, j: (i, 0, j))],
        out_specs=pl.BlockSpec((1, SUBLANES, tn), lambda i, j: (i, 0, j)),
        out_shape=jax.ShapeDtypeStruct((depth, SUBLANES, n), F32),
        compiler_params=_cparams(("arbitrary", "arbitrary")),
        name="ada_mod",
    )(cp, w_ada, b_ada.reshape(depth, 1, n))
    return out[:, :b]


def _nmm_kernel(x_ref, g_ref, sc_ref, sh_ref, w_ref, o_ref, h_scr, *, head_major):
    @pl.when(pl.program_id(1) == 0)
    def _():
        h_scr[...] = _norm_mod(x_ref[...], g_ref[...], sc_ref[0], sh_ref[0]).astype(BF16)

    res = _dot(h_scr[...], w_ref[...])
    if head_major:
        for k in range(o_ref.shape[0]):
            o_ref[k] = res[:, k * LANES:(k + 1) * LANES]
    else:
        o_ref[...] = res


def _norm_mod_matmul(x, gain, sc, sh, w_bf16, seq, *, head_major, tm=1024, tn=1024):
    t, d = x.shape
    n = w_bf16.shape[1]
    tpb = seq // tm
    if head_major:
        out_shape = jax.ShapeDtypeStruct((n // LANES, t, LANES), F32)
        out_spec = pl.BlockSpec((tn // LANES, tm, LANES), lambda i, j: (j, i, 0))
    else:
        out_shape = jax.ShapeDtypeStruct((t, n), F32)
        out_spec = pl.BlockSpec((tm, tn), lambda i, j: (i, j))
    return pl.pallas_call(
        functools.partial(_nmm_kernel, head_major=head_major),
        grid=(t // tm, n // tn),
        in_specs=[pl.BlockSpec((tm, d), lambda i, j: (i, 0)),
                  pl.BlockSpec((1, d), lambda i, j: (0, 0)),
                  pl.BlockSpec((1, 1, d), lambda i, j: (i // tpb, 0, 0)),
                  pl.BlockSpec((1, 1, d), lambda i, j: (i // tpb, 0, 0)),
                  pl.BlockSpec((d, tn), lambda i, j: (0, j))],
        out_specs=out_spec,
        out_shape=out_shape,
        scratch_shapes=[pltpu.VMEM((tm, d), BF16)],
        compiler_params=_cparams(("arbitrary", "arbitrary")),
        name="norm_mod_matmul",
    )(x, gain.reshape(1, d), sc, sh, w_bf16)


def _hgrn_sum_mats(reverse):
    c = HG_CHUNK
    mats = []
    for lvl in range(2, HG_LEVELS + 1):
        m = 1 << lvl
        half = m >> 1
        a = np.zeros((c, c), np.float32)
        for t in range(c):
            mid = (t // m) * m + half
            if not reverse:
                if t >= mid:
                    a[t, mid:t + 1] = 1.0
                else:
                    a[t, t + 1:mid] = 1.0
            else:
                if t < mid:
                    a[t, t:mid] = 1.0
                else:
                    a[t, mid:t] = 1.0
        mats.append(a)
    ones = np.ones((c, c), np.float32)
    mats.append(np.triu(ones) if reverse else np.tril(ones))
    return np.concatenate(mats, axis=0)


def _hgrn_chain(q, flog, v, lb, a_ref, st_ref, d, h, reverse):
    c = HG_CHUNK
    f = lb + (1.0 - lb) * jax.nn.sigmoid(flog)
    g = jnp.log(f) * LOG2_E
    k = 1.0 - f
    g1, g2 = _split2(g)
    e2 = _dot(a_ref[...], jnp.concatenate([g1, g2], axis=1))
    e = e2[:, :LANES] + e2[:, LANES:]
    xdec = jnp.exp2(e)
    rows = lax.broadcasted_iota(jnp.int32, (c, c), 0)
    cols = lax.broadcasted_iota(jnp.int32, (c, c), 1)
    attn = jnp.where(rows == cols, jnp.sum(q * k, axis=-1, keepdims=True), 0.0)
    for lvl in range(1, HG_LEVELS + 1):
        row_q = (((rows >> (lvl - 1)) & 1) == 1) != reverse
        col_k = (((cols >> (lvl - 1)) & 1) == 0) != reverse
        if lvl == 1:
            mixed = jnp.where(row_q, q * f, k).astype(BF16)
        else:
            mixed = (jnp.where(row_q, q, k) * xdec[(lvl - 2) * c:(lvl - 1) * c]).astype(BF16)
        p = _dot_nt(mixed, mixed)
        pick = jnp.logical_and((rows >> lvl) == (cols >> lvl), jnp.logical_and(row_q, col_k))
        attn = jnp.where(pick, p, attn)
    eq = e[(HG_LEVELS - 1) * c:HG_LEVELS * c]
    tot = eq[0:1] if reverse else eq[c - 1:c]
    qd = (q * xdec[(HG_LEVELS - 1) * c:HG_LEVELS * c]).astype(BF16)
    kd = (k * jnp.exp2(tot - eq)).astype(BF16)
    st = st_ref[d, h]
    vb = v.astype(BF16)
    o = _dot(attn.astype(BF16), vb) + _dot_nt(qd, st.astype(BF16))
    st_ref[d, h] = st * jnp.exp2(tot) + _dot_tn(vb, kd)
    return o


def _hgrn_scan_kernel(lb_ref, afw_ref, abw_ref, qf_ref, ff_ref, vf_ref, qb_ref, fb_ref, vb_ref,
                      of_ref, ob_ref, st_ref, lbs_ref, *, layer):
    @pl.when(pl.program_id(1) == 0)
    def _():
        st_ref[...] = jnp.zeros_like(st_ref)
        lb = lb_ref[...]
        ex = jnp.exp(lb - jnp.max(lb, axis=0, keepdims=True))
        p = ex / jnp.sum(ex, axis=0, keepdims=True)
        acc = p[0]
        for jj in range(1, layer + 1):
            acc = acc + p[jj]
        lbs_ref[...] = acc - p[0]

    def body(h, carry):
        lb = lbs_ref[pl.ds(h, 1), :]
        of_ref[h] = _hgrn_chain(qf_ref[h], ff_ref[h], vf_ref[h], lb, afw_ref, st_ref, 0, h, False)
        ob_ref[h] = _hgrn_chain(qb_ref[h], fb_ref[h], vb_ref[h], lb, abw_ref, st_ref, 1, h, True)
        return carry

    lax.fori_loop(0, HG_HEADS, body, 0, unroll=8)


def _hgrn_scan(proj_hm, hg_lb, layer, batch, seq):
    hh = HG_HEADS
    t = proj_hm.shape[1]
    nc = seq // HG_CHUNK
    c = HG_CHUNK
    na = hg_lb.shape[0]
    afw = jnp.asarray(_hgrn_sum_mats(False), BF16)
    abw = jnp.asarray(_hgrn_sum_mats(True), BF16)
    nrow = afw.shape[0]

    def fwd(part):
        return pl.BlockSpec((hh, c, LANES), lambda b, cc: (part, b * nc + cc, 0))

    def bwd(part):
        return pl.BlockSpec((hh, c, LANES), lambda b, cc: (part, b * nc + nc - 1 - cc, 0))

    out_sd = jax.ShapeDtypeStruct((hh, t, LANES), F32)
    return pl.pallas_call(
        functools.partial(_hgrn_scan_kernel, layer=layer),
        grid=(batch, nc),
        in_specs=[pl.BlockSpec((na, hh, LANES), lambda b, cc: (0, 0, 0)),
                  pl.BlockSpec((nrow, c), lambda b, cc: (0, 0)),
                  pl.BlockSpec((nrow, c), lambda b, cc: (0, 0)),
                  fwd(0), fwd(1), fwd(3), bwd(0), bwd(2), bwd(3)],
        out_specs=[pl.BlockSpec((hh, c, LANES), lambda b, cc: (0, b * nc + cc, 0)),
                   pl.BlockSpec((hh, c, LANES), lambda b, cc: (0, b * nc + nc - 1 - cc, 0))],
        out_shape=[out_sd, out_sd],
        scratch_shapes=[pltpu.VMEM((2, hh, LANES, LANES), F32), pltpu.VMEM((hh, LANES), F32)],
        compiler_params=_cparams(("arbitrary", "arbitrary")),
        name="hgrn_scan",
    )(hg_lb.reshape(na, hh, LANES), afw, abw, proj_hm, proj_hm, proj_hm, proj_hm, proj_hm, proj_hm)


def _hgrn_out_kernel(of_ref, ob_ref, gg_ref, ng_ref, w_ref, x_ref, g1_ref, o_ref):
    parts = []
    for h in range(HG_HEADS):
        o = of_ref[h] + ob_ref[h]
        inv = lax.rsqrt(jnp.mean(o * o, axis=-1, keepdims=True) + EPS)
        gg = gg_ref[h]
        parts.append(((o * inv) * ng_ref[pl.ds(h, 1), :] * (gg * jax.nn.sigmoid(gg))).astype(BF16))
    y = _dot(jnp.concatenate(parts, axis=-1), w_ref[...])
    o_ref[...] = x_ref[...] + g1_ref[0] * y


def _hgrn_out(o_fw, o_bw, proj_hm, norm_g, w_out_bf16, x, g1, seq, tm=512):
    hh = HG_HEADS
    t, d = x.shape
    tpb = seq // tm
    return pl.pallas_call(
        _hgrn_out_kernel,
        grid=(t // tm,),
        in_specs=[pl.BlockSpec((hh, tm, LANES), lambda i: (0, i, 0)),
                  pl.BlockSpec((hh, tm, LANES), lambda i: (0, i, 0)),
                  pl.BlockSpec((hh, tm, LANES), lambda i: (4, i, 0)),
                  pl.BlockSpec((hh, LANES), lambda i: (0, 0)),
                  pl.BlockSpec((d, d), lambda i: (0, 0)),
                  pl.BlockSpec((tm, d), lambda i: (i, 0)),
                  pl.BlockSpec((1, 1, d), lambda i: (i // tpb, 0, 0))],
        out_specs=pl.BlockSpec((tm, d), lambda i: (i, 0)),
        out_shape=jax.ShapeDtypeStruct((t, d), F32),
        compiler_params=_cparams(("arbitrary",)),
        name="hgrn_out",
    )(o_fw, o_bw, proj_hm, norm_g.reshape(hh, LANES), w_out_bf16, x, g1)


def _lru_group_scan(a, x, carry, rows, reverse):
    for s in (1, 2, 4):
        if not reverse:
            keep = rows >= s
            a_sh = jnp.where(keep, pltpu.roll(a, s, 0), 1.0)
            x_sh = jnp.where(keep, pltpu.roll(x, s, 0), 0.0)
        else:
            keep = rows < SUBLANES - s
            a_sh = jnp.where(keep, pltpu.roll(a, SUBLANES - s, 0), 1.0)
            x_sh = jnp.where(keep, pltpu.roll(x, SUBLANES - s, 0), 0.0)
        x = x + a * x_sh
        a = a * a_sh
    hs = x + a * carry
    new_carry = hs[0:1] if reverse else hs[SUBLANES - 1:SUBLANES]
    return hs, new_carry


def _lru_conv_kernel(xc_ref, xp_ref, xn_ref, cw_ref, cb_ref, o_ref, *, tm, nchunk):
    chunk = pl.program_id(0) % nchunk
    prev = jnp.where(chunk == 0, 0.0, xp_ref[...])
    nxt = jnp.where(chunk == nchunk - 1, 0.0, xn_ref[...])
    ext = jnp.concatenate([prev, xc_ref[...], nxt], axis=0)
    cw = cw_ref[...]
    xc = cb_ref[...]
    for j in range(CONV_W):
        off = SUBLANES - CONV_LEFT + j
        xc = xc + ext[off:off + tm] * cw[j:j + 1]
    o_ref[...] = xc


def _lru_conv(proj, conv_w, conv_b, seq, tm=512):
    t = proj.shape[0]
    w = proj.shape[1] // 2
    nchunk = seq // tm
    hb = tm // SUBLANES
    nhalo = t // SUBLANES
    return pl.pallas_call(
        functools.partial(_lru_conv_kernel, tm=tm, nchunk=nchunk),
        grid=(t // tm,),
        in_specs=[pl.BlockSpec((tm, w), lambda i: (i, 0)),
                  pl.BlockSpec((SUBLANES, w), lambda i: (jnp.maximum(i * hb - 1, 0), 0)),
                  pl.BlockSpec((SUBLANES, w), lambda i: (jnp.minimum((i + 1) * hb, nhalo - 1), 0)),
                  pl.BlockSpec((CONV_W, w), lambda i: (0, 0)),
                  pl.BlockSpec((1, w), lambda i: (0, 0))],
        out_specs=pl.BlockSpec((tm, w), lambda i: (i, 0)),
        out_shape=jax.ShapeDtypeStruct((t, w), F32),
        compiler_params=_cparams(("arbitrary",)),
        name="lru_conv",
    )(proj, proj, proj, conv_w, conv_b.reshape(1, w))


def _lru_kernel(xc_ref, wa_ref, ba_ref, wx_ref, bx_ref, lam_ref, o_ref, a_scr, b_scr, carry_scr, *, tm):
    d = pl.program_id(0)

    @pl.when(pl.program_id(2) == 0)
    def _():
        carry_scr[...] = jnp.zeros_like(carry_scr)

    xc = xc_ref[...]
    xcb = xc.astype(BF16)
    bw = xcb.shape[1] // LRU_BLOCKS
    ra, rx = [], []
    for n in range(LRU_BLOCKS):
        blk = xcb[:, n * bw:(n + 1) * bw]
        ra.append(_dot(blk, wa_ref[0, n]))
        rx.append(_dot(blk, wx_ref[0, n]))
    r = jax.nn.sigmoid(jnp.concatenate(ra, axis=-1) + ba_ref[0])
    ig = jax.nn.sigmoid(jnp.concatenate(rx, axis=-1) + bx_ref[0])
    lam = lam_ref[0]
    softplus_neg = jnp.maximum(-lam, 0.0) + jnp.log(1.0 + jnp.exp(-jnp.abs(lam)))
    log_a = -LRU_C * r * softplus_neg
    a = jnp.exp(log_a)
    a_scr[...] = a
    b_scr[...] = jnp.sqrt(-jnp.tanh(log_a) * (a * a + 1.0)) * ig * xc

    ngroups = tm // SUBLANES
    rows = lax.broadcasted_iota(jnp.int32, (SUBLANES, xc.shape[1]), 0)

    def run(reverse):
        def body(i, carry):
            gi = (ngroups - 1 - i) if reverse else i
            sl = pl.ds(pl.multiple_of(gi * SUBLANES, SUBLANES), SUBLANES)
            hs, carry = _lru_group_scan(a_scr[sl, :], b_scr[sl, :], carry, rows, reverse)
            o_ref[0, sl, :] = hs
            return carry
        carry_scr[...] = lax.fori_loop(0, ngroups, body, carry_scr[...])

    @pl.when(d == 0)
    def _():
        run(False)

    @pl.when(d == 1)
    def _():
        run(True)


def _lru_scan(xc, w_a, b_a, w_x, b_x, lam, batch, seq, tm=256):
    t, w = xc.shape
    nchunk = seq // tm

    def pos(d, b, cc):
        return b * nchunk + jnp.where(d == 0, cc, nchunk - 1 - cc)

    return pl.pallas_call(
        functools.partial(_lru_kernel, tm=tm),
        grid=(2, batch, nchunk),
        in_specs=[pl.BlockSpec((tm, w), lambda d, b, cc: (pos(d, b, cc), 0)),
                  pl.BlockSpec((1, LRU_BLOCKS, w // LRU_BLOCKS, w // LRU_BLOCKS), lambda d, b, cc: (d, 0, 0, 0)),
                  pl.BlockSpec((1, 1, w), lambda d, b, cc: (d, 0, 0)),
                  pl.BlockSpec((1, LRU_BLOCKS, w // LRU_BLOCKS, w // LRU_BLOCKS), lambda d, b, cc: (d, 0, 0, 0)),
                  pl.BlockSpec((1, 1, w), lambda d, b, cc: (d, 0, 0)),
                  pl.BlockSpec((1, 1, w), lambda d, b, cc: (d, 0, 0))],
        out_specs=pl.BlockSpec((1, tm, w), lambda d, b, cc: (d, pos(d, b, cc), 0)),
        out_shape=jax.ShapeDtypeStruct((2, t, w), F32),
        scratch_shapes=[pltpu.VMEM((tm, w), F32), pltpu.VMEM((tm, w), F32), pltpu.VMEM((1, w), F32)],
        compiler_params=_cparams(("arbitrary", "arbitrary", "arbitrary")),
        name="lru_scan",
    )(xc, w_a.astype(BF16), b_a.reshape(2, 1, w), w_x.astype(BF16), b_x.reshape(2, 1, w), lam.reshape(2, 1, w))


def _lru_out_kernel(hs_ref, yb_ref, w_ref, x_ref, g1_ref, o_ref):
    u = (hs_ref[0] + hs_ref[1]) * jax.nn.gelu(yb_ref[...])
    o_ref[...] = x_ref[...] + g1_ref[0] * _dot(u.astype(BF16), w_ref[...])


def _lru_out(hs, proj, w_out_bf16, x, g1, seq, tm=512):
    t, d = x.shape
    w = hs.shape[2]
    tpb = seq // tm
    return pl.pallas_call(
        _lru_out_kernel,
        grid=(t // tm,),
        in_specs=[pl.BlockSpec((2, tm, w), lambda i: (0, i, 0)),
                  pl.BlockSpec((tm, w), lambda i: (i, 1)),
                  pl.BlockSpec((w, d), lambda i: (0, 0)),
                  pl.BlockSpec((tm, d), lambda i: (i, 0)),
                  pl.BlockSpec((1, 1, d), lambda i: (i // tpb, 0, 0))],
        out_specs=pl.BlockSpec((tm, d), lambda i: (i, 0)),
        out_shape=jax.ShapeDtypeStruct((t, d), F32),
        compiler_params=_cparams(("arbitrary",)),
        name="lru_out",
    )(hs, proj, w_out_bf16, x, g1)


def _peer_q_kernel(x_ref, g_ref, sc_ref, sh_ref, w1_ref, w2_ref, k1_ref, k2_ref, hb_ref, sc_out_ref):
    h = _norm_mod(x_ref[...], g_ref[...], sc_ref[0], sh_ref[0])
    h1, h2 = _split2(h)
    hb_ref[...] = h.T.astype(BF16)
    q = _dot(h1, w1_ref[...]) + _dot(h1, w2_ref[...]) + _dot(h2, w1_ref[...])
    nhp = k1_ref.shape[0]
    nch = sc_out_ref.shape[1]
    for hp in range(nhp):
        qa, qb = _split2(q[:, hp * LANES:(hp + 1) * LANES])
        s = _dot_nt(k1_ref[hp], qa) + _dot_nt(k1_ref[hp], qb) + _dot_nt(k2_ref[hp], qa)
        for ch in range(nch):
            sc_out_ref[hp, ch] = s[:, ch * LANES:(ch + 1) * LANES]


def _peer_q(x, gain, sc, sh, wq1, wq2, keys1, keys2, seq, tm=256):
    t, d = x.shape
    nq = wq1.shape[1]
    nhp = keys1.shape[0]
    tpb = seq // tm
    nch = tm // LANES
    return pl.pallas_call(
        _peer_q_kernel,
        grid=(t // tm,),
        in_specs=[pl.BlockSpec((tm, d), lambda i: (i, 0)),
                  pl.BlockSpec((1, d), lambda i: (0, 0)),
                  pl.BlockSpec((1, 1, d), lambda i: (i // tpb, 0, 0)),
                  pl.BlockSpec((1, 1, d), lambda i: (i // tpb, 0, 0)),
                  pl.BlockSpec((d, nq), lambda i: (0, 0)),
                  pl.BlockSpec((d, nq), lambda i: (0, 0)),
                  pl.BlockSpec((nhp, PEER_NKEYS, LANES), lambda i: (0, 0, 0)),
                  pl.BlockSpec((nhp, PEER_NKEYS, LANES), lambda i: (0, 0, 0))],
        out_specs=[pl.BlockSpec((d, tm), lambda i: (0, i)),
                   pl.BlockSpec((nhp, nch, PEER_NKEYS, LANES), lambda i: (0, i, 0, 0))],
        out_shape=[jax.ShapeDtypeStruct((d, t), BF16),
                   jax.ShapeDtypeStruct((nhp, t // LANES, PEER_NKEYS, LANES), F32)],
        compiler_params=_cparams(("arbitrary",)),
        name="peer_q",
    )(x, gain.reshape(1, d), sc, sh, wq1, wq2, keys1, keys2)


def _top16(s, rows, exact):
    work = s
    rank = jnp.full(s.shape, PEER_TOPK, jnp.int32)
    vals = []
    for kk in range(PEER_TOPK):
        m = jnp.max(work, axis=0, keepdims=True)
        if exact:
            idx = jnp.min(jnp.where(work == m, rows, PEER_NKEYS), axis=0, keepdims=True)
            sel = rows == idx
        else:
            sel = work == m
        rank = jnp.where(sel, kk, rank)
        work = jnp.where(sel, NEG_INF, work)
        vals.append(m)
    return rank, jnp.concatenate(vals, axis=0)


def _col_count(mask):
    return jnp.sum(jnp.where(mask, 1.0, 0.0), axis=0, keepdims=True)


def _any_lane_differs(count, want):
    return jnp.max(jnp.where(count == want, 0.0, 1.0)) > 0.5


def _top16_values(s):
    work = s
    vals = []
    for _ in range(PEER_TOPK):
        m = jnp.max(work, axis=0, keepdims=True)
        work = jnp.where(work == m, NEG_INF, work)
        vals.append(m)
    return jnp.concatenate(vals, axis=0), work


def _cand16(cand, flat, r16, mtop):
    cnt = jnp.zeros((PEER_TOPK, cand.shape[1]), F32)
    z = jnp.zeros((1, cand.shape[1]), F32)
    for _ in range(PEER_TOPK):
        m = jnp.max(cand, axis=0, keepdims=True)
        idx = jnp.min(jnp.where(cand == m, flat, PEER_TOPK * PEER_TOPK), axis=0, keepdims=True)
        cand = jnp.where(flat == idx, NEG_INF, cand)
        cnt = cnt + jnp.where(r16 == (idx >> 4), 1.0, 0.0)
        z = z + jnp.exp(m - mtop)
    return cnt, z


def _bf16_pair(x):
    u = pltpu.bitcast(x.astype(BF16).astype(F32), jnp.uint32)
    return u | (u >> 16)


PEER_TOPK_GROUP = 4


def _peer_topk_kernel(s_ref, e1_ref, n_ref, e2_ref, r2_ref):
    nch = s_ref.shape[1]
    grp = PEER_TOPK_GROUP
    width = grp * LANES
    rows = lax.broadcasted_iota(jnp.int32, (PEER_NKEYS, width), 0)
    r16 = lax.broadcasted_iota(jnp.int32, (PEER_TOPK, width), 0)
    r8 = lax.broadcasted_iota(jnp.int32, (SUBLANES, width), 0)
    flat = jnp.concatenate([r16] + [r8 + PEER_TOPK * k1 for k1 in range(1, 8)]
                           + [(r8 + 8) * PEER_TOPK], axis=0)
    want = float(PEER_TOPK)

    def body(i, carry):
        chunks = [i * grp + k for k in range(grp)]
        s1 = jnp.concatenate([s_ref[0, ch] for ch in chunks], axis=1)
        s2 = jnp.concatenate([s_ref[1, ch] for ch in chunks], axis=1)
        def candidates(v1, v2):
            return jnp.concatenate([v1[0:1] + v2] + [v1[k1:k1 + 1] + v2[0:8] for k1 in range(1, 8)]
                                   + [v1[8:16] + v2[0:1]], axis=0)

        def exact():
            rank1, v1 = _top16(s1, rows, True)
            rank2, v2 = _top16(s2, rows, True)
            mtop = v1[0:1] + v2[0:1]
            cnt, z = _cand16(candidates(v1, v2), flat, r16, mtop)
            nrow = jnp.zeros((PEER_NKEYS, width), F32)
            for kk in range(PEER_TOPK):
                nrow = jnp.where(rank1 == kk, cnt[kk:kk + 1], nrow)
            e1 = jnp.where(rank1 < PEER_TOPK, jnp.exp(s1 - v1[0:1]) / z, 0.0)
            return nrow, e1, rank2, v2

        v1, left1 = _top16_values(s1)
        rank2, v2 = _top16(s2, rows, False)
        cand = candidates(v1, v2)
        mtop = v1[0:1] + v2[0:1]
        z = jnp.zeros((1, width), F32)
        tau = mtop
        for _ in range(PEER_TOPK):
            tau = jnp.max(cand, axis=0, keepdims=True)
            cand = jnp.where(cand == tau, NEG_INF, cand)
            z = z + jnp.exp(tau - mtop)
        tied_vals = jnp.logical_or(_any_lane_differs(_col_count(left1 == NEG_INF), want),
                                   _any_lane_differs(_col_count(rank2 < PEER_TOPK), want))
        tied_cand = _any_lane_differs(_col_count(cand == NEG_INF), want)

        def has(vrow):
            return (s1 + vrow) >= tau

        c16 = has(v2[15:16])
        c8 = has(v2[7:8])
        c4 = has(jnp.where(c8, v2[11:12], v2[3:4]))
        c2 = has(jnp.where(c8, jnp.where(c4, v2[13:14], v2[9:10]), jnp.where(c4, v2[5:6], v2[1:2])))
        hi = jnp.where(c4, jnp.where(c2, v2[14:15], v2[12:13]), jnp.where(c2, v2[10:11], v2[8:9]))
        lo = jnp.where(c4, jnp.where(c2, v2[6:7], v2[4:5]), jnp.where(c2, v2[2:3], v2[0:1]))
        c1 = has(jnp.where(c8, hi, lo))
        nrow = (jnp.where(c8, 8.0, 0.0) + jnp.where(c4, 4.0, 0.0)) + (jnp.where(c2, 2.0, 0.0) + jnp.where(c1, 1.0, 0.0))
        nrow = jnp.where(c16, float(PEER_TOPK), nrow)
        e1 = jnp.exp(s1 - v1[0:1]) / z
        tied_cand = jnp.logical_or(tied_cand, _any_lane_differs(jnp.sum(nrow, axis=0, keepdims=True), want))
        fast = (nrow, e1, rank2, v2)

        def cand_exact():
            cnt, zz = _cand16(candidates(v1, v2), flat, r16, mtop)
            nr = jnp.zeros((PEER_NKEYS, width), F32)
            for kk in range(PEER_TOPK):
                nr = jnp.where(s1 == v1[kk:kk + 1], cnt[kk:kk + 1], nr)
            return nr, jnp.exp(s1 - v1[0:1]) / zz, rank2, v2

        nrow, e1, rank2, v2 = lax.cond(
            tied_vals, exact, lambda: lax.cond(tied_cand, cand_exact, lambda: fast))
        e1 = _bf16_pair(e1)
        nrow = _bf16_pair(nrow)
        e2 = jnp.where(rank2 < PEER_TOPK, jnp.exp(s2 - v2[0:1]), 0.0)
        e2 = pltpu.bitcast(e2.astype(BF16), jnp.uint32)
        r2 = pltpu.bitcast(rank2.astype(F32).astype(BF16), jnp.uint32)
        for k, ch in enumerate(chunks):
            cols = slice(k * LANES, (k + 1) * LANES)
            e1_ref[0, ch] = e1[:, cols]
            n_ref[0, ch] = nrow[:, cols]
            e2_ref[0, ch] = e2[:, cols]
            r2_ref[0, ch] = r2[:, cols]
        return carry

    lax.fori_loop(0, nch // grp, body, 0)


def _peer_topk(scores_t, nch=4):
    nhp, ntc, nk, _ = scores_t.shape
    hh = nhp // 2
    spec = pl.BlockSpec((1, nch, nk, LANES), lambda i, h: (h, i, 0, 0))
    spec_p = pl.BlockSpec((1, nch, nk // 2, LANES), lambda i, h: (h, i, 0, 0))
    sd_u = jax.ShapeDtypeStruct((hh, ntc, nk, LANES), jnp.uint32)
    sd_b = jax.ShapeDtypeStruct((hh, ntc, nk // 2, LANES), jnp.uint32)
    return pl.pallas_call(
        _peer_topk_kernel,
        grid=(ntc // nch, hh),
        in_specs=[pl.BlockSpec((2, nch, nk, LANES), lambda i, h: (h, i, 0, 0))],
        out_specs=[spec, spec, spec_p, spec_p],
        out_shape=[sd_u, sd_u, sd_b, sd_b],
        compiler_params=_cparams(("arbitrary", "arbitrary")),
        name="peer_topk",
    )(scores_t)


def _dup_rows(row_u32, nrows):
    return pltpu.bitcast(jnp.broadcast_to(row_u32, (nrows // 2, row_u32.shape[1])), BF16)


PEER_PB = 2


GELU_C1 = 0.7978845608028654
GELU_C2 = GELU_C1 * 0.044715


def _gelu_times(x, w):
    hw = (0.5 * x) * w
    return hw + hw * jnp.tanh(x * (GELU_C1 + GELU_C2 * (x * x)))


def _peer_dense_kernel(hbt_ref, u_ref, v_ref, e1_ref, n_ref, e2_ref, r2_ref, x_ref, g2_ref, fg_ref,
                       o_ref, acc_ref, act0, act1, ab_ref, *, final_norm):
    j = pl.program_id(1)

    @pl.when(j == 0)
    def _():
        acc_ref[...] = jnp.zeros_like(acc_ref)

    nch = e2_ref.shape[1]
    nb = u_ref.shape[0]
    acts = (act0, act1)
    zero = jnp.zeros((), BF16)
    blk_rows = PEER_PB * PEER_NKEYS

    def act_mm(p):
        acts[p % 2][...] = _dot(pltpu.bitcast(u_ref[p], BF16), hbt_ref[...])

    def gate(p):
        for ch in range(nch):
            for a2 in range(PEER_PB):
                a = p * PEER_PB + a2
                w = jnp.zeros((PEER_NKEYS, LANES), BF16)
                for h in range(PEER_HEADS):
                    keep = pltpu.bitcast(r2_ref[h, ch], BF16) < _dup_rows(n_ref[h, ch, a:a + 1, :], PEER_NKEYS)
                    w = w + (jnp.where(keep, pltpu.bitcast(e2_ref[h, ch], BF16), zero)
                             * _dup_rows(e1_ref[h, ch, a:a + 1, :], PEER_NKEYS))
                rows = slice(a2 * PEER_NKEYS, (a2 + 1) * PEER_NKEYS)
                cols = slice(ch * LANES, (ch + 1) * LANES)
                out_rows = slice(p * blk_rows + a2 * PEER_NKEYS, p * blk_rows + (a2 + 1) * PEER_NKEYS)
                ab_ref[out_rows, cols] = _gelu_times(acts[p % 2][rows, cols].astype(BF16), w)

    act_mm(0)
    for p in range(nb):
        if p + 1 < nb:
            act_mm(p + 1)
        gate(p)
    acc_ref[...] += _dot_tn(ab_ref[...], pltpu.bitcast(v_ref[...], BF16))

    @pl.when(j == pl.num_programs(1) - 1)
    def _():
        y = x_ref[...] + g2_ref[0] * acc_ref[...]
        if final_norm:
            y = (y * lax.rsqrt(jnp.mean(y * y, axis=-1, keepdims=True) + EPS)) * fg_ref[...]
        o_ref[...] = y


def _pack_rows_kernel(x_ref, o_ref):
    o_ref[...] = pltpu.bitcast(x_ref[...].astype(BF16), jnp.uint32)


def _pack_rows(tabs, layer, tr=1024):
    _, n, d = tabs.shape
    return pl.pallas_call(
        _pack_rows_kernel,
        grid=(n // tr,),
        in_specs=[pl.BlockSpec((None, tr, d), lambda i: (layer, i, 0))],
        out_specs=pl.BlockSpec((tr // 2, d), lambda i: (i, 0)),
        out_shape=jax.ShapeDtypeStruct((n // 2, d), jnp.uint32),
        compiler_params=_cparams(("arbitrary",)),
        name="pack_rows",
    )(tabs)


def _peer_dense(hbt, u_pk, v_pk, e1, nrow, e2, rank2, x, g2, final_g, final_norm, seq, tm=512, na=16):
    t, d = x.shape
    ne = 2 * u_pk.shape[0]
    rows = PEER_PB * PEER_NKEYS
    nb = na // PEER_PB
    te = nb * rows
    nch = tm // LANES
    tpb = seq // tm
    hh = PEER_HEADS
    small = pl.BlockSpec((hh, nch, na, LANES), lambda i, j: (0, i, j, 0))
    big = pl.BlockSpec((hh, nch, PEER_NKEYS // 2, LANES), lambda i, j: (0, i, 0, 0))
    return pl.pallas_call(
        functools.partial(_peer_dense_kernel, final_norm=final_norm),
        grid=(t // tm, ne // te),
        in_specs=[pl.BlockSpec((d, tm), lambda i, j: (0, i)),
                  pl.BlockSpec((nb, rows // 2, d), lambda i, j: (j, 0, 0)),
                  pl.BlockSpec((te // 2, d), lambda i, j: (j, 0)),
                  small, small, big, big,
                  pl.BlockSpec((tm, d), lambda i, j: (i, 0)),
                  pl.BlockSpec((1, 1, d), lambda i, j: (i // tpb, 0, 0)),
                  pl.BlockSpec((1, d), lambda i, j: (0, 0))],
        out_specs=pl.BlockSpec((tm, d), lambda i, j: (i, 0)),
        out_shape=jax.ShapeDtypeStruct((t, d), F32),
        scratch_shapes=[pltpu.VMEM((tm, d), F32), pltpu.VMEM((rows, tm), F32), pltpu.VMEM((rows, tm), F32),
                        pltpu.VMEM((te, tm), BF16)],
        compiler_params=_cparams(("arbitrary", "arbitrary")),
        name="peer_dense",
    )(hbt, u_pk.reshape(ne // rows, rows // 2, d), v_pk, e1, nrow, e2, rank2, x, g2, final_g.reshape(1, d))


def _peer_layer(x, gain, sc, sh, g2, w_q, keys, u_tabs, v_tabs, layer, final_g, final_norm, seq):
    wq1, wq2 = _split2(w_q)
    kf = keys.reshape(-1, PEER_NKEYS, keys.shape[-1])
    k1, k2 = _split2(kf)
    hb, scores_t = _peer_q(x, gain, sc, sh, wq1, wq2, k1, k2, seq)
    e1, nrow, e2, rank2 = _peer_topk(scores_t)
    return _peer_dense(hb, _pack_rows(u_tabs, layer), _pack_rows(v_tabs, layer), e1, nrow, e2, rank2,
                       x, g2, final_g, final_norm, seq)


def kernel(x, c, w_ada, b_ada, norm_g, hg_w_in, hg_lb, hg_norm_g, hg_w_out, lru_w_in, lru_conv_w,
           lru_conv_b, lru_w_a, lru_b_a, lru_w_x, lru_b_x, lru_lam, lru_w_out, peer_w_q, peer_keys,
           peer_u, peer_v, final_g):
    batch, seq, d = x.shape
    depth = w_ada.shape[0]
    n_mixers = 2
    mod = _ada_mod(c, w_ada, b_ada)
    xt = x.reshape(batch * seq, d)
    for i in range(depth):
        parts = [mod[i, :, k * d:(k + 1) * d].reshape(batch, 1, d) for k in range(6)]
        sh1, sc1, g1, sh2, sc2, g2 = parts
        j = i // n_mixers
        if i % n_mixers == 0:
            proj = _norm_mod_matmul(xt, norm_g[i, 0], sc1, sh1, hg_w_in[j].astype(BF16), seq,
                                    head_major=True)
            o_fw, o_bw = _hgrn_scan(proj, hg_lb, j, batch, seq)
            xt = _hgrn_out(o_fw, o_bw, proj, hg_norm_g[j], hg_w_out[j].astype(BF16), xt, g1, seq)
        else:
            proj = _norm_mod_matmul(xt, norm_g[i, 0], sc1, sh1, lru_w_in[j].astype(BF16), seq,
                                    head_major=False)
            xc = _lru_conv(proj, lru_conv_w[j], lru_conv_b[j], seq)
            hs = _lru_scan(xc, lru_w_a[j], lru_b_a[j], lru_w_x[j], lru_b_x[j], lru_lam[j], batch, seq)
            xt = _lru_out(hs, proj, lru_w_out[j].astype(BF16), xt, g1, seq)
        xt = _peer_layer(xt, norm_g[i, 1], sc2, sh2, g2, peer_w_q[i], peer_keys[i], peer_u, peer_v,
                         i, final_g, i == depth - 1, seq)
    return xt.reshape(batch, seq, d)
```
